```python
import math
import jax
import jax.numpy as jnp
from jax import lax
import numpy as np

D_MODEL = 2048
BATCH = 4
SEQ = 2048
DEPTH = 2

D_FF = 5632
RG_HEADS = 8
RG_DK = 128
RG_DV = 128
RG_WIDTH = RG_HEADS * RG_DV
CHUNK = 64
ATT_HEADS = 8
ATT_KV_HEADS = 2
ATT_GROUP = ATT_HEADS // ATT_KV_HEADS
ATT_HD = 128
ATT_WIDTH = ATT_HEADS * ATT_HD
KV_WIDTH = ATT_KV_HEADS * ATT_HD
WINDOW = 128
Q_BLOCK = 128
IN_SIZES = (RG_WIDTH, RG_WIDTH, RG_WIDTH, RG_WIDTH, RG_WIDTH, ATT_WIDTH, KV_WIDTH, KV_WIDTH, D_MODEL, D_MODEL)
IN_COLS = 5 * RG_WIDTH + ATT_WIDTH + 2 * KV_WIDTH + 2 * D_MODEL
EPS = 1e-6

kernel_name = "hybrid_hgrn2_window_gqa_macaron_encoder"


def rmsnorm(x, gain):
    xf = x.astype(jnp.float32)
    y = xf * lax.rsqrt(jnp.mean(xf * xf, axis=-1, keepdims=True) + EPS)
    return (y * gain.astype(jnp.float32)).astype(x.dtype)


def swiglu(h, w_gate, w_up, w_down):
    return (jax.nn.silu(h @ w_gate) * (h @ w_up)) @ w_down


def lower_bounds(lb_logits):
    lb = jnp.cumsum(jax.nn.softmax(lb_logits.astype(jnp.float32), axis=0), axis=0)
    return lb - lb[0:1]


def hgrn2_direction(q, k, v, log_f):
    B, S, H, DK = q.shape
    DV = v.shape[-1]
    n = S // CHUNK

    def to_chunks(t):
        return t.reshape(B, n, CHUNK, H, t.shape[-1]).transpose(1, 0, 3, 2, 4)

    qc, kc, vc, ac = to_chunks(q), to_chunks(k), to_chunks(v), to_chunks(log_f)
    causal_in_chunk = jnp.tril(jnp.ones((CHUNK, CHUNK), dtype=bool))[:, :, None]

    def step(state, inp):
        qi, ki, vi, ai = inp
        b = jnp.cumsum(ai, axis=2)
        rel = b[:, :, :, None, :] - b[:, :, None, :, :]
        decay = jnp.exp(jnp.where(causal_in_chunk, rel, -jnp.inf))
        scores = jnp.einsum('bhtd,bhsd,bhtsd->bhts', qi, ki, decay)
        o_intra = jnp.einsum('bhts,bhsv->bhtv', scores, vi)
        o_inter = jnp.einsum('bhtd,bhdv->bhtv', qi * jnp.exp(b), state)
        b_last = b[:, :, -1:, :]
        k_dec = ki * jnp.exp(b_last - b)
        new_state = jnp.exp(b_last[:, :, 0, :])[..., None] * state + jnp.einsum('bhsd,bhsv->bhdv', k_dec, vi)
        return new_state, o_intra + o_inter

    state0 = jnp.zeros((B, H, DK, DV), jnp.float32)
    _, o = lax.scan(step, state0, (qc, kc, vc, ac))
    return o.transpose(1, 0, 3, 2, 4).reshape(B, S, H, DV)


def hgrn2_mixer(q, i, zf, zb, g, lb_f, lb_b, out_gain):
    B, S, _ = q.shape
    dt = q.dtype

    def heads(t):
        return t.astype(jnp.float32).reshape(B, S, RG_HEADS, -1)

    qh, vh = heads(q), heads(i)

    def gates(z, lb):
        z = heads(z)
        lb = lb.astype(jnp.float32).reshape(RG_HEADS, RG_DK)
        log_f = jnp.logaddexp(jnp.log(lb), jnp.log1p(-lb) + jax.nn.log_sigmoid(z))
        key = (1.0 - lb) * jax.nn.sigmoid(-z)
        return key, log_f

    k_f, a_f = gates(zf, lb_f)
    k_b, a_b = gates(zb, lb_b)
    o_fwd = hgrn2_direction(qh, k_f, vh, a_f)
    flip = lambda t: jnp.flip(t, axis=1)
    o_bwd = flip(hgrn2_direction(flip(qh), flip(k_b), flip(vh), flip(a_b)))
    o = o_fwd + o_bwd
    o = o * lax.rsqrt(jnp.mean(o * o, axis=-1, keepdims=True) + EPS)
    o = o * out_gain.astype(jnp.float32).reshape(RG_HEADS, RG_DV)
    o = o.reshape(B, S, RG_WIDTH) * jax.nn.silu(g.astype(jnp.float32))
    return o.astype(dt)


def window_gqa(q, k, v, sink):
    B, S, _ = q.shape
    dt = q.dtype
    nb = S // Q_BLOCK
    qb = q.reshape(B, nb, Q_BLOCK, ATT_KV_HEADS, ATT_GROUP, ATT_HD)

    def band(t):
        tp = jnp.pad(t.reshape(B, S, ATT_KV_HEADS, ATT_HD), ((0, 0), (Q_BLOCK, Q_BLOCK), (0, 0), (0, 0)))
        tb = tp.reshape(B, nb + 2, Q_BLOCK, ATT_KV_HEADS, ATT_HD)
        return jnp.concatenate([tb[:, :-2], tb[:, 1:-1], tb[:, 2:]], axis=2)

    kb, vb = band(k), band(v)
    qi = jnp.arange(Q_BLOCK)[:, None]
    kj = jnp.arange(3 * Q_BLOCK)[None, :]
    rel = kj - Q_BLOCK - qi
    kpos = jnp.arange(nb)[:, None, None] * Q_BLOCK + kj[None] - Q_BLOCK
    valid = (jnp.abs(rel) <= WINDOW)[None] & (kpos >= 0) & (kpos < S)

    slopes = 2.0 ** (-8.0 * jnp.arange(1, ATT_HEADS + 1, dtype=jnp.float32) / ATT_HEADS)
    slopes = slopes.reshape(ATT_KV_HEADS, ATT_GROUP)
    dist = jnp.abs(rel).astype(jnp.float32)

    s = jnp.einsum('bnqhgd,bnkhd->bnhgqk', qb, kb).astype(jnp.float32) * (1.0 / math.sqrt(ATT_HD))
    s = s - slopes[:, :, None, None] * dist
    s = jnp.where(valid[None, :, None, None], s, -jnp.inf)
    sink_logit = jnp.broadcast_to(sink.astype(jnp.float32).reshape(ATT_KV_HEADS, ATT_GROUP)[:, :, None, None],
                                  (B, nb, ATT_KV_HEADS, ATT_GROUP, Q_BLOCK, 1))
    p = jax.nn.softmax(jnp.concatenate([s, sink_logit], axis=-1), axis=-1)[..., :-1]
    o = jnp.einsum('bnhgqk,bnkhd->bnqhgd', p.astype(dt), vb)
    return o.reshape(B, S, ATT_WIDTH)


def setup_inputs(seed: int = 0) -> dict:
    key = jax.random.key(seed)
    ks = jax.random.split(key, 20)

    def w(k, shape, fan_in):
        return jax.random.normal(k, shape, jnp.float32) * (fan_in ** -0.5)

    def gain(k, shape):
        return 1.0 + 0.02 * jax.random.normal(k, shape, jnp.float32)

    return {
        "x": jax.random.normal(ks[0], (BATCH, SEQ, D_MODEL), jnp.float32),
        "ffn1_norm": gain(ks[1], (DEPTH, D_MODEL)),
        "ffn1_w_gate": w(ks[2], (DEPTH, D_MODEL, D_FF), D_MODEL),
        "ffn1_w_up": w(ks[3], (DEPTH, D_MODEL, D_FF), D_MODEL),
        "ffn1_w_down": w(ks[4], (DEPTH, D_FF, D_MODEL), D_FF),
        "mix_norm": gain(ks[5], (DEPTH, D_MODEL)),
        "w_in": w(ks[6], (DEPTH, D_MODEL, IN_COLS), D_MODEL),
        "lb_fwd_logits": 0.5 * jax.random.normal(ks[7], (DEPTH, RG_WIDTH), jnp.float32),
        "lb_bwd_logits": 0.5 * jax.random.normal(ks[8], (DEPTH, RG_WIDTH), jnp.float32),
        "rg_out_norm": gain(ks[9], (DEPTH, RG_WIDTH)),
        "attn_sink": 0.5 * jax.random.normal(ks[10], (DEPTH, ATT_HEADS), jnp.float32),
        "w_branch_a": w(ks[11], (DEPTH, RG_WIDTH, D_MODEL), RG_WIDTH),
        "w_branch_b": w(ks[12], (DEPTH, ATT_WIDTH, D_MODEL), ATT_WIDTH),
        "w_out": w(ks[13], (DEPTH, D_MODEL, D_MODEL), D_MODEL),
        "ffn2_norm": gain(ks[14], (DEPTH, D_MODEL)),
        "ffn2_w_gate": w(ks[15], (DEPTH, D_MODEL, D_FF), D_MODEL),
        "ffn2_w_up": w(ks[16], (DEPTH, D_MODEL, D_FF), D_MODEL),
        "ffn2_w_down": w(ks[17], (DEPTH, D_FF, D_MODEL), D_FF),
        "final_norm": gain(ks[18], (D_MODEL,)),
    }


def reference(x, ffn1_norm, ffn1_w_gate, ffn1_w_up, ffn1_w_down, mix_norm, w_in, lb_fwd_logits,
              lb_bwd_logits, rg_out_norm, attn_sink, w_branch_a, w_branch_b, w_out, ffn2_norm,
              ffn2_w_gate, ffn2_w_up, ffn2_w_down, final_norm):
    split_at = [int(c) for c in np.cumsum(IN_SIZES)[:-1]]
    lb_f_all = lower_bounds(lb_fwd_logits)
    lb_b_all = lower_bounds(lb_bwd_logits)
    for l in range(DEPTH):
        h = rmsnorm(x, ffn1_norm[l])
        x = x + 0.5 * swiglu(h, ffn1_w_gate[l], ffn1_w_up[l], ffn1_w_down[l])
        h = rmsnorm(x, mix_norm[l])
        p = h @ w_in[l]
        rq, ri, rzf, rzb, rg, aq, ak, av, ga, gb = jnp.split(p, split_at, axis=-1)
        o_a = hgrn2_mixer(rq, ri, rzf, rzb, rg, lb_f_all[l], lb_b_all[l], rg_out_norm[l])
        o_b = window_gqa(aq, ak, av, attn_sink[l])
        merged = jax.nn.sigmoid(ga) * (o_a @ w_branch_a[l]) + jax.nn.sigmoid(gb) * (o_b @ w_branch_b[l])
        x = x + merged @ w_out[l]
        h = rmsnorm(x, ffn2_norm[l])
        x = x + 0.5 * swiglu(h, ffn2_w_gate[l], ffn2_w_up[l], ffn2_w_down[l])
    return rmsnorm(x, final_norm)
```

```python
import functools
import math

import jax
import jax.numpy as jnp
from jax import lax
from jax.experimental import pallas as pl
from jax.experimental.pallas import tpu as pltpu

D_MODEL = 2048
D_FF = 5632
RG_HEADS = 8
RG_DK = 128
RG_DV = 128
RG_WIDTH = RG_HEADS * RG_DV
ATT_HEADS = 8
ATT_KV_HEADS = 2
ATT_GROUP = ATT_HEADS // ATT_KV_HEADS
ATT_HD = 128
ATT_WIDTH = ATT_HEADS * ATT_HD
KV_WIDTH = ATT_KV_HEADS * ATT_HD
WINDOW = 128
Q_BLOCK = 128
IN_COLS = 5 * RG_WIDTH + ATT_WIDTH + 2 * KV_WIDTH + 2 * D_MODEL
EPS = 1e-6

COL_RQ = 0
COL_RI = RG_WIDTH
COL_ZF = 2 * RG_WIDTH
COL_ZB = 3 * RG_WIDTH
COL_RG = 4 * RG_WIDTH
COL_AQ = 5 * RG_WIDTH
COL_AK = COL_AQ + ATT_WIDTH
COL_AV = COL_AK + KV_WIDTH
COL_GA = COL_AV + KV_WIDTH
COL_GB = COL_GA + D_MODEL

LANES = 128
HGRN_CHUNK = 64
VMEM_LIMIT_BYTES = 56 * 1024 * 1024

BF16 = jnp.bfloat16
F32 = jnp.float32


def _params(*sem):
    return pltpu.CompilerParams(dimension_semantics=sem, vmem_limit_bytes=VMEM_LIMIT_BYTES)


def _rmsnorm_rows(x, gain):
    ms = jnp.mean(x * x, axis=-1, keepdims=True)
    return x * lax.rsqrt(ms + EPS) * gain


def _ffn_up_kernel(x_ref, gain_ref, wg_ref, wu_ref, a_ref, h_scr):
    @pl.when(pl.program_id(1) == 0)
    def _():
        h_scr[...] = _rmsnorm_rows(x_ref[...], gain_ref[...]).astype(BF16)

    h = h_scr[...]
    g = jnp.dot(h, wg_ref[...].astype(BF16), preferred_element_type=F32)
    u = jnp.dot(h, wu_ref[...].astype(BF16), preferred_element_type=F32)
    a_ref[...] = (g * jax.nn.sigmoid(g) * u).astype(BF16)


def ffn_up(x, gain, wg, wu, *, tm, tn):
    t, d = x.shape
    f = wg.shape[1]
    return pl.pallas_call(
        _ffn_up_kernel,
        grid=(t // tm, f // tn),
        in_specs=[
            pl.BlockSpec((tm, d), lambda i, j: (i, 0)),
            pl.BlockSpec((1, d), lambda i, j: (0, 0)),
            pl.BlockSpec((d, tn), lambda i, j: (0, j)),
            pl.BlockSpec((d, tn), lambda i, j: (0, j)),
        ],
        out_specs=pl.BlockSpec((tm, tn), lambda i, j: (i, j)),
        out_shape=jax.ShapeDtypeStruct((t, f), BF16),
        scratch_shapes=[pltpu.VMEM((tm, d), BF16)],
        compiler_params=_params("parallel", "arbitrary"),
        name="ffn_up",
    )(x, gain.reshape(1, d), wg, wu)


def _resid_mm_kernel(a_ref, w_ref, x_ref, o_ref, acc_scr, *, scale, nk):
    k = pl.program_id(2)

    @pl.when(k == 0)
    def _():
        acc_scr[...] = jnp.zeros_like(acc_scr)

    acc_scr[...] += jnp.dot(a_ref[...], w_ref[...].astype(BF16), preferred_element_type=F32)

    @pl.when(k == nk - 1)
    def _():
        o_ref[...] = x_ref[...] + scale * acc_scr[...]


def resid_mm(a, w, x, *, scale, tm, tn, tk):
    t, kdim = a.shape
    n = w.shape[1]
    nk = kdim // tk
    return pl.pallas_call(
        functools.partial(_resid_mm_kernel, scale=scale, nk=nk),
        grid=(t // tm, n // tn, nk),
        in_specs=[
            pl.BlockSpec((tm, tk), lambda i, j, k: (i, k)),
            pl.BlockSpec((tk, tn), lambda i, j, k: (k, j)),
            pl.BlockSpec((tm, tn), lambda i, j, k: (i, j)),
        ],
        out_specs=pl.BlockSpec((tm, tn), lambda i, j, k: (i, j)),
        out_shape=jax.ShapeDtypeStruct((t, n), F32),
        scratch_shapes=[pltpu.VMEM((tm, tn), F32)],
        compiler_params=_params("parallel", "parallel", "arbitrary"),
        name="resid_mm",
    )(a, w, x)


def _in_proj_kernel(x_ref, gain_ref, w_ref, p_ref, h_scr):
    @pl.when(pl.program_id(1) == 0)
    def _():
        h_scr[...] = _rmsnorm_rows(x_ref[...], gain_ref[...]).astype(BF16)

    p_ref[...] = jnp.dot(h_scr[...], w_ref[...].astype(BF16), preferred_element_type=F32)


def in_proj(x, gain, w, *, tm, tn):
    t, d = x.shape
    n = w.shape[1]
    return pl.pallas_call(
        _in_proj_kernel,
        grid=(t // tm, n // tn),
        in_specs=[
            pl.BlockSpec((tm, d), lambda i, j: (i, 0)),
            pl.BlockSpec((1, d), lambda i, j: (0, 0)),
            pl.BlockSpec((d, tn), lambda i, j: (0, j)),
        ],
        out_specs=pl.BlockSpec((tm, tn), lambda i, j: (i, j)),
        out_shape=jax.ShapeDtypeStruct((t, n), F32),
        scratch_shapes=[pltpu.VMEM((tm, d), BF16)],
        compiler_params=_params("parallel", "arbitrary"),
        name="in_proj",
    )(x, gain.reshape(1, d), w)


def _block_ref_rows(p, h, r):
    c, n = p.shape
    blk = 2 * h
    if blk >= 8:
        p3 = p.reshape(c // blk, blk, n)
        return jnp.broadcast_to(p3[:, r:r + 1, :], p3.shape).reshape(c, n)
    pos = lax.broadcasted_iota(jnp.int32, p.shape, 0) & (blk - 1)
    out = p
    for src in range(blk):
        if src == r:
            continue
        shifted = pltpu.roll(p, (src - r) % c, axis=0)
        out = jnp.where(pos == src, shifted, out)
    return out


def _chunk_scores(q, key, a, masks_ref, mask_base, reverse):
    c = q.shape[0]
    row = lax.broadcasted_iota(jnp.int32, q.shape, 0)
    p = a
    scores = None
    h = 1
    level = 0
    while h < c:
        upper = (row & h) != 0
        qside = jnp.logical_not(upper) if reverse else upper
        t_rows = _block_ref_rows(p, h, h if reverse else h - 1)
        e = jnp.where(qside, p, t_rows - p)
        qk = (jnp.where(qside, q, key) * jnp.exp(e)).astype(BF16)
        s = lax.dot_general(qk, qk, (((1,), (1,)), ((), ())), preferred_element_type=F32)
        s = s * masks_ref[mask_base + level]
        scores = s if scores is None else scores + s
        p = p + jnp.where(qside, t_rows, 0.0)
        h *= 2
        level += 1
    return scores, p


def _log_decay_and_key(z, log_lb, log1m_lb, one_m_lb):
    log_sig = jnp.minimum(z, 0.0) - jnp.log1p(jnp.exp(-jnp.abs(z)))
    y = log1m_lb + log_sig
    amax = jnp.maximum(log_lb, y)
    log_f = amax + jnp.log1p(jnp.exp(-jnp.abs(log_lb - y)))
    key = one_m_lb * jax.nn.sigmoid(-z)
    return log_f, key


def _hgrn_kernel(q_ref, v_ref, zf_ref, zb_ref, g_ref, lbf_ref, lbb_ref, gain_ref, masks_ref, o_ref,
                 of_scr, ob_scr, qhf_scr, qhb_scr, kdf_scr, kdb_scr, decf_scr, decb_scr, sf_scr, sb_scr,
                 *, chunk, n_chunks, n_levels):
    c = chunk
    nt = (((1,), (1,)), ((), ()))
    tn = (((0,), (0,)), ((), ()))

    def rows_of(i):
        return pl.ds(pl.multiple_of(i * c, c), c)

    def phase1(i, carry):
        rows = rows_of(i)
        q = q_ref[0, rows, :]
        v = v_ref[0, rows, :]
        vb = v.astype(BF16)
        for reverse, z_ref, lb_ref, o_scr, qh_scr, kd_scr, dec_scr in (
                (False, zf_ref, lbf_ref, of_scr, qhf_scr, kdf_scr, decf_scr),
                (True, zb_ref, lbb_ref, ob_scr, qhb_scr, kdb_scr, decb_scr)):
            a, key = _log_decay_and_key(z_ref[0, rows, :], lb_ref[0, 0:1, :], lb_ref[0, 1:2, :],
                                        lb_ref[0, 2:3, :])
            scores, cum = _chunk_scores(q, key, a, masks_ref, n_levels if reverse else 0, reverse)
            diag = jnp.sum(q * key, axis=-1, keepdims=True)
            o_scr[rows, :] = jnp.dot(scores.astype(BF16), vb, preferred_element_type=F32) + diag * v
            total = cum[0:1, :] if reverse else cum[c - 1:c, :]
            qh_scr[rows, :] = (q * jnp.exp(cum)).astype(BF16)
            kd_scr[rows, :] = (key * jnp.exp(total - cum)).astype(BF16)
            dec_scr[i] = jnp.broadcast_to(jnp.exp(total), (8, LANES))
        return carry

    lax.fori_loop(0, n_chunks, phase1, 0)

    sf_scr[...] = jnp.zeros_like(sf_scr)
    sb_scr[...] = jnp.zeros_like(sb_scr)

    def phase2(i, carry):
        for ci, o_scr, qh_scr, kd_scr, dec_scr, s_scr in (
                (i, of_scr, qhf_scr, kdf_scr, decf_scr, sf_scr),
                (n_chunks - 1 - i, ob_scr, qhb_scr, kdb_scr, decb_scr, sb_scr)):
            rows = rows_of(ci)
            st = s_scr[...]
            o_scr[rows, :] += lax.dot_general(qh_scr[rows, :], st.astype(BF16), nt,
                                              preferred_element_type=F32)
            vb = v_ref[0, rows, :].astype(BF16)
            upd = lax.dot_general(vb, kd_scr[rows, :], tn, preferred_element_type=F32)
            s_scr[...] = st * dec_scr[ci][0:1, :] + upd
        return carry

    lax.fori_loop(0, n_chunks, phase2, 0)

    def phase3(i, carry):
        rows = rows_of(i)
        o = of_scr[rows, :] + ob_scr[rows, :]
        o = _rmsnorm_rows(o, gain_ref[0])
        g = g_ref[0, rows, :]
        o_ref[0, rows, :] = (o * (g * jax.nn.sigmoid(g))).astype(o_ref.dtype)
        return carry

    lax.fori_loop(0, n_chunks, phase3, 0)


def _level_masks(chunk):
    idx = jnp.arange(chunk)
    t, s = idx[:, None], idx[None, :]
    fwd, bwd = [], []
    h = 1
    while h < chunk:
        same = (t // (2 * h)) == (s // (2 * h))
        t_up = (t & h) != 0
        s_up = (s & h) != 0
        fwd.append(same & t_up & ~s_up)
        bwd.append(same & ~t_up & s_up)
        h *= 2
    return jnp.stack(fwd + bwd).astype(F32)


def hgrn2_mixer(p, lb_f, lb_b, out_gain, *, heads, chunk=HGRN_CHUNK):
    b, s, _ = p.shape
    n_chunks = s // chunk
    n_levels = int(math.log2(chunk))

    def lb_rows(lb):
        lb = lb.astype(F32).reshape(heads, 1, LANES)
        rows = jnp.concatenate([jnp.log(lb), jnp.log1p(-lb), 1.0 - lb], axis=1)
        return jnp.pad(rows, ((0, 0), (0, 5), (0, 0)))

    def col(block0):
        return pl.BlockSpec((1, s, LANES), lambda bi, hi: (bi, 0, block0 + hi))

    per_head = pl.BlockSpec((1, 8, LANES), lambda bi, hi: (hi, 0, 0))
    masks = _level_masks(chunk)
    kern = functools.partial(_hgrn_kernel, chunk=chunk, n_chunks=n_chunks, n_levels=n_levels)
    return pl.pallas_call(
        kern,
        grid=(b, heads),
        in_specs=[
            col(COL_RQ // LANES), col(COL_RI // LANES), col(COL_ZF // LANES), col(COL_ZB // LANES),
            col(COL_RG // LANES), per_head, per_head,
            pl.BlockSpec((1, 1, LANES), lambda bi, hi: (hi, 0, 0)),
            pl.BlockSpec(masks.shape, lambda bi, hi: (0, 0, 0)),
        ],
        out_specs=pl.BlockSpec((1, s, LANES), lambda bi, hi: (bi, 0, hi)),
        out_shape=jax.ShapeDtypeStruct((b, s, heads * LANES), BF16),
        scratch_shapes=[
            pltpu.VMEM((s, LANES), F32), pltpu.VMEM((s, LANES), F32),
            pltpu.VMEM((s, LANES), BF16), pltpu.VMEM((s, LANES), BF16),
            pltpu.VMEM((s, LANES), BF16), pltpu.VMEM((s, LANES), BF16),
            pltpu.VMEM((n_chunks, 8, LANES), F32), pltpu.VMEM((n_chunks, 8, LANES), F32),
            pltpu.VMEM((LANES, LANES), F32), pltpu.VMEM((LANES, LANES), F32),
        ],
        compiler_params=_params("parallel", "parallel"),
        name="hgrn2",
    )(p, p, p, p, p, lb_rows(lb_f), lb_rows(lb_b), out_gain.astype(F32).reshape(heads, 1, LANES), masks)


def _attn_kernel(sink_ref, q_ref, kp_ref, kc_ref, kn_ref, vp_ref, vc_ref, vn_ref, o_ref, *, n_blocks):
    n = pl.program_id(1)
    qb = Q_BLOCK
    qi = lax.broadcasted_iota(jnp.int32, (qb, 3 * qb), 0)
    kj = lax.broadcasted_iota(jnp.int32, (qb, 3 * qb), 1)
    rel = kj - qb - qi
    dist = jnp.abs(rel)
    kpos = n * qb + kj - qb
    valid = (dist <= WINDOW) & (kpos >= 0) & (kpos < n_blocks * qb)
    dist_f = dist.astype(F32)
    scale = 1.0 / math.sqrt(ATT_HD)
    nt = (((1,), (1,)), ((), ()))
    for kvh in range(ATT_KV_HEADS):
        cols = slice(kvh * ATT_HD, (kvh + 1) * ATT_HD)
        k = jnp.concatenate([kp_ref[0, :, cols], kc_ref[0, :, cols], kn_ref[0, :, cols]], axis=0).astype(BF16)
        v = jnp.concatenate([vp_ref[0, :, cols], vc_ref[0, :, cols], vn_ref[0, :, cols]], axis=0).astype(BF16)
        for g in range(ATT_GROUP):
            head = kvh * ATT_GROUP + g
            slope = 2.0 ** (-8.0 * (head + 1) / ATT_HEADS)
            q = q_ref[0, :, head * ATT_HD:(head + 1) * ATT_HD].astype(BF16)
            s = lax.dot_general(q, k, nt, preferred_element_type=F32) * scale - slope * dist_f
            s = jnp.where(valid, s, -jnp.inf)
            sink = sink_ref[head]
            m = jnp.maximum(jnp.max(s, axis=-1, keepdims=True), sink)
            e = jnp.exp(s - m)
            denom = jnp.sum(e, axis=-1, keepdims=True) + jnp.exp(sink - m)
            o = jnp.dot(e.astype(BF16), v, preferred_element_type=F32) / denom
            o_ref[0, :, head * ATT_HD:(head + 1) * ATT_HD] = o.astype(o_ref.dtype)


def window_gqa(p, sink):
    b, s, _ = p.shape
    nb = s // Q_BLOCK
    kcol, vcol = COL_AK // KV_WIDTH, COL_AV // KV_WIDTH

    def kv(col, off):
        return pl.BlockSpec((1, Q_BLOCK, KV_WIDTH),
                            lambda bi, n, sk: (bi, jnp.clip(n + off, 0, nb - 1), col))

    return pl.pallas_call(
        functools.partial(_attn_kernel, n_blocks=nb),
        grid_spec=pltpu.PrefetchScalarGridSpec(
            num_scalar_prefetch=1,
            grid=(b, nb),
            in_specs=[
                pl.BlockSpec((1, Q_BLOCK, ATT_WIDTH), lambda bi, n, sk: (bi, n, COL_AQ // ATT_WIDTH)),
                kv(kcol, -1), kv(kcol, 0), kv(kcol, 1),
                kv(vcol, -1), kv(vcol, 0), kv(vcol, 1),
            ],
            out_specs=pl.BlockSpec((1, Q_BLOCK, ATT_WIDTH), lambda bi, n, sk: (bi, n, 0)),
        ),
        out_shape=jax.ShapeDtypeStruct((b, s, ATT_WIDTH), BF16),
        compiler_params=_params("parallel", "parallel"),
        name="window_gqa",
    )(sink.astype(F32), p, p, p, p, p, p, p)


def _merge_kernel(oa_ref, ob_ref, wa_ref, wb_ref, ga_ref, gb_ref, m_ref):
    ya = jnp.dot(oa_ref[...], wa_ref[...].astype(BF16), preferred_element_type=F32)
    yb = jnp.dot(ob_ref[...], wb_ref[...].astype(BF16), preferred_element_type=F32)
    m_ref[...] = (jax.nn.sigmoid(ga_ref[...]) * ya + jax.nn.sigmoid(gb_ref[...]) * yb).astype(m_ref.dtype)


def merge_branches(oa, ob, wa, wb, p, *, tm, tn):
    t, ka = oa.shape
    kb = ob.shape[1]
    n = wa.shape[1]
    return pl.pallas_call(
        _merge_kernel,
        grid=(t // tm, n // tn),
        in_specs=[
            pl.BlockSpec((tm, ka), lambda i, j: (i, 0)),
            pl.BlockSpec((tm, kb), lambda i, j: (i, 0)),
            pl.BlockSpec((ka, tn), lambda i, j: (0, j)),
            pl.BlockSpec((kb, tn), lambda i, j: (0, j)),
            pl.BlockSpec((tm, tn), lambda i, j: (i, COL_GA // tn + j)),
            pl.BlockSpec((tm, tn), lambda i, j: (i, COL_GB // tn + j)),
        ],
        out_specs=pl.BlockSpec((tm, tn), lambda i, j: (i, j)),
        out_shape=jax.ShapeDtypeStruct((t, n), BF16),
        compiler_params=_params("parallel", "parallel"),
        name="merge_branches",
    )(oa, ob, wa, wb, p, p)


def _final_norm_kernel(x_ref, gain_ref, o_ref):
    o_ref[...] = _rmsnorm_rows(x_ref[...], gain_ref[...])


def final_rmsnorm(x, gain, *, tm):
    t, d = x.shape
    return pl.pallas_call(
        _final_norm_kernel,
        grid=(t // tm,),
        in_specs=[pl.BlockSpec((tm, d), lambda i: (i, 0)), pl.BlockSpec((1, d), lambda i: (0, 0))],
        out_specs=pl.BlockSpec((tm, d), lambda i: (i, 0)),
        out_shape=jax.ShapeDtypeStruct((t, d), F32),
        compiler_params=_params("parallel"),
        name="final_norm",
    )(x, gain.reshape(1, d))


def _lower_bounds(lb_logits):
    lb = jnp.cumsum(jax.nn.softmax(lb_logits.astype(F32), axis=0), axis=0)
    return lb - lb[0:1]


def kernel(x, ffn1_norm, ffn1_w_gate, ffn1_w_up, ffn1_w_down, mix_norm, w_in, lb_fwd_logits, lb_bwd_logits, rg_out_norm, attn_sink, w_branch_a, w_branch_b, w_out, ffn2_norm, ffn2_w_gate, ffn2_w_up, ffn2_w_down, final_norm):
    b, s, d = x.shape
    t = b * s
    depth = w_in.shape[0]
    assert COL_GA % 512 == 0 and COL_GB % 512 == 0
    lb_f_all = _lower_bounds(lb_fwd_logits)
    lb_b_all = _lower_bounds(lb_bwd_logits)
    xf = x.reshape(t, d)

    def ffn(xf, gain, wg, wu, wd):
        a = ffn_up(xf, gain, wg, wu, tm=1024, tn=256)
        return resid_mm(a, wd, xf, scale=0.5, tm=1024, tn=1024, tk=1408)

    for l in range(depth):
        xf = ffn(xf, ffn1_norm[l], ffn1_w_gate[l], ffn1_w_up[l], ffn1_w_down[l])
        p = in_proj(xf, mix_norm[l], w_in[l], tm=1024, tn=512)
        p3 = p.reshape(b, s, IN_COLS)
        o_a = hgrn2_mixer(p3, lb_f_all[l], lb_b_all[l], rg_out_norm[l], heads=RG_HEADS)
        o_b = window_gqa(p3, attn_sink[l])
        merged = merge_branches(o_a.reshape(t, RG_WIDTH), o_b.reshape(t, ATT_WIDTH),
                                w_branch_a[l], w_branch_b[l], p, tm=1024, tn=512)
        xf = resid_mm(merged, w_out[l], xf, scale=1.0, tm=1024, tn=1024, tk=1024)
        xf = ffn(xf, ffn2_norm[l], ffn2_w_gate[l], ffn2_w_up[l], ffn2_w_down[l])
    return final_rmsnorm(xf, final_norm, tm=512).reshape(b, s, d)
```

```python
import functools
import math

import jax
import jax.numpy as jnp
from jax import lax
from jax.experimental import pallas as pl
from jax.experimental.pallas import tpu as pltpu

D_MODEL = 2048
D_FF = 5632
RG_HEADS = 8
RG_DK = 128
RG_DV = 128
RG_WIDTH = RG_HEADS * RG_DV
ATT_HEADS = 8
ATT_KV_HEADS = 2
ATT_GROUP = ATT_HEADS // ATT_KV_HEADS
ATT_HD = 128
ATT_WIDTH = ATT_HEADS * ATT_HD
KV_WIDTH = ATT_KV_HEADS * ATT_HD
WINDOW = 128
Q_BLOCK = 128
IN_COLS = 5 * RG_WIDTH + ATT_WIDTH + 2 * KV_WIDTH + 2 * D_MODEL
EPS = 1e-6

COL_RQ = 0
COL_RI = RG_WIDTH
COL_ZF = 2 * RG_WIDTH
COL_ZB = 3 * RG_WIDTH
COL_RG = 4 * RG_WIDTH
COL_AQ = 5 * RG_WIDTH
COL_AK = COL_AQ + ATT_WIDTH
COL_AV = COL_AK + KV_WIDTH
COL_GA = COL_AV + KV_WIDTH
COL_GB = COL_GA + D_MODEL

LANES = 128
SUBLANES = 8
HGRN_CHUNK = 128
VMEM_LIMIT_BYTES = 56 * 1024 * 1024

TM = 2048
TM_MERGE = 1024
TN_FFN_UP = 256
TN_IN_PROJ = 512
TN_MERGE = 512
TN_RESID = 1024
TK_RESID = 512
TM_NORM = 512

BF16 = jnp.bfloat16
F32 = jnp.float32


def _params(*sem):
    return pltpu.CompilerParams(dimension_semantics=sem, vmem_limit_bytes=VMEM_LIMIT_BYTES)


def _rmsnorm_rows(x, gain):
    ms = jnp.mean(x * x, axis=-1, keepdims=True)
    return x * lax.rsqrt(ms + EPS) * gain


def _once(block_shape, index_map):
    return pl.BlockSpec(block_shape, index_map, pipeline_mode=pl.Buffered(1))


def _ffn_up_kernel(x_ref, gain_ref, wg_ref, wu_ref, a_ref, h_scr):
    @pl.when(pl.program_id(1) == 0)
    def _():
        h_scr[...] = _rmsnorm_rows(x_ref[...], gain_ref[...]).astype(BF16)

    h = h_scr[...]
    g = jnp.dot(h, wg_ref[...].astype(BF16), preferred_element_type=F32)
    u = jnp.dot(h, wu_ref[...].astype(BF16), preferred_element_type=F32)
    a_ref[...] = (g * jax.nn.sigmoid(g) * u).astype(BF16)


def ffn_up(x, gain, wg, wu, l, *, tm=TM, tn=TN_FFN_UP):
    t, d = x.shape
    f = wg.shape[2]
    return pl.pallas_call(
        _ffn_up_kernel,
        grid=(t // tm, f // tn),
        in_specs=[
            _once((tm, d), lambda i, j: (i, 0)),
            pl.BlockSpec((None, 1, d), lambda i, j: (l, 0, 0)),
            pl.BlockSpec((None, d, tn), lambda i, j: (l, 0, j)),
            pl.BlockSpec((None, d, tn), lambda i, j: (l, 0, j)),
        ],
        out_specs=pl.BlockSpec((tm, tn), lambda i, j: (i, j)),
        out_shape=jax.ShapeDtypeStruct((t, f), BF16),
        scratch_shapes=[pltpu.VMEM((tm, d), BF16)],
        compiler_params=_params("parallel", "arbitrary"),
        name="ffn_up",
    )(x, gain.reshape(gain.shape[0], 1, d), wg, wu)


def _resid_mm_kernel(a_ref, w_ref, x_ref, o_ref, *, scale):
    @pl.when(pl.program_id(2) == 0)
    def _():
        o_ref[...] = x_ref[...]

    y = jnp.dot(a_ref[...], w_ref[...].astype(BF16), preferred_element_type=F32)
    o_ref[...] += y if scale == 1.0 else scale * y


def resid_mm(a, w, x, l, *, scale, tm=TM, tn=TN_RESID, tk=TK_RESID):
    t, kdim = a.shape
    n = w.shape[2]
    return pl.pallas_call(
        functools.partial(_resid_mm_kernel, scale=scale),
        grid=(t // tm, n // tn, kdim // tk),
        in_specs=[
            pl.BlockSpec((tm, tk), lambda i, j, k: (i, k)),
            pl.BlockSpec((None, tk, tn), lambda i, j, k: (l, k, j)),
            _once((tm, tn), lambda i, j, k: (i, j)),
        ],
        out_specs=pl.BlockSpec((tm, tn), lambda i, j, k: (i, j)),
        out_shape=jax.ShapeDtypeStruct((t, n), F32),
        compiler_params=_params("parallel", "parallel", "arbitrary"),
        name="resid_mm",
    )(a, w, x)


def _in_proj_kernel(x_ref, gain_ref, w_ref, p_ref, h_scr):
    @pl.when(pl.program_id(1) == 0)
    def _():
        h_scr[...] = _rmsnorm_rows(x_ref[...], gain_ref[...]).astype(BF16)

    p_ref[...] = jnp.dot(h_scr[...], w_ref[...].astype(BF16), preferred_element_type=F32)


def in_proj(x, gain, w, l, *, tm=TM, tn=TN_IN_PROJ):
    t, d = x.shape
    n = w.shape[2]
    return pl.pallas_call(
        _in_proj_kernel,
        grid=(t // tm, n // tn),
        in_specs=[
            _once((tm, d), lambda i, j: (i, 0)),
            pl.BlockSpec((None, 1, d), lambda i, j: (l, 0, 0)),
            pl.BlockSpec((None, d, tn), lambda i, j: (l, 0, j)),
        ],
        out_specs=pl.BlockSpec((tm, tn), lambda i, j: (i, j)),
        out_shape=jax.ShapeDtypeStruct((t, n), F32),
        scratch_shapes=[pltpu.VMEM((tm, d), BF16)],
        compiler_params=_params("parallel", "arbitrary"),
        name="in_proj",
    )(x, gain.reshape(gain.shape[0], 1, d), w)


def _block_ref_rows(p, h, r):
    c, n = p.shape
    blk = 2 * h
    if blk >= SUBLANES:
        p3 = p.reshape(c // blk, blk, n)
        return jnp.broadcast_to(p3[:, r:r + 1, :], p3.shape).reshape(c, n)
    pos = lax.broadcasted_iota(jnp.int32, p.shape, 0) & (blk - 1)
    out = p
    for src in range(blk):
        if src == r:
            continue
        shifted = pltpu.roll(p, (src - r) % c, axis=0)
        out = jnp.where(pos == src, shifted, out)
    return out


def _chunk_scores(q, key, a, masks_ref, mask_base, reverse):
    c, n = q.shape
    row = lax.broadcasted_iota(jnp.int32, q.shape, 0)
    p = a
    scores = None
    h = 1
    level = 0
    while h < c:
        if h >= SUBLANES:
            nb = c // (2 * h)
            p4 = p.reshape(nb, 2, h, n)
            q4 = q.reshape(nb, 2, h, n)
            k4 = key.reshape(nb, 2, h, n)
            lo, hi = p4[:, 0], p4[:, 1]
            if reverse:
                t_row = hi[:, 0:1, :]
                qk = jnp.stack([q4[:, 0] * jnp.exp(lo), k4[:, 1] * jnp.exp(t_row - hi)], axis=1)
                p = jnp.stack([lo + t_row, hi], axis=1).reshape(c, n)
            else:
                t_row = lo[:, h - 1:h, :]
                qk = jnp.stack([k4[:, 0] * jnp.exp(t_row - lo), q4[:, 1] * jnp.exp(hi)], axis=1)
                p = jnp.stack([lo, hi + t_row], axis=1).reshape(c, n)
            qk = qk.reshape(c, n).astype(BF16)
        else:
            upper = (row & h) != 0
            qside = jnp.logical_not(upper) if reverse else upper
            t_rows = _block_ref_rows(p, h, h if reverse else h - 1)
            e = jnp.where(qside, p, t_rows - p)
            qk = (jnp.where(qside, q, key) * jnp.exp(e)).astype(BF16)
            p = p + jnp.where(qside, t_rows, 0.0)
        s = lax.dot_general(qk, qk, (((1,), (1,)), ((), ())), preferred_element_type=F32)
        s = s * masks_ref[mask_base + level]
        scores = s if scores is None else scores + s
        h *= 2
        level += 1
    return scores, p


def _log_decay_and_key(z, log_lb, log1m_lb, one_m_lb):
    u = jnp.exp(-jnp.abs(z))
    one_u = 1.0 + u
    y = log1m_lb + (jnp.minimum(z, 0.0) - jnp.log(one_u))
    log_f = jnp.maximum(log_lb, y) + jnp.log(1.0 + jnp.exp(-jnp.abs(log_lb - y)))
    key = one_m_lb * (jnp.where(z >= 0.0, u, 1.0) / one_u)
    return log_f, key


def _hgrn_kernel(q_ref, v_ref, zf_ref, zb_ref, g_ref, lbf_ref, lbb_ref, gain_ref, masks_ref, o_ref,
                 of_scr, ob_scr, qhf_scr, qhb_scr, updf_scr, updb_scr, decf_scr, decb_scr, stf_scr, stb_scr,
                 *, chunk, n_chunks, n_levels):
    c = chunk
    nt = (((1,), (1,)), ((), ()))
    tn = (((0,), (0,)), ((), ()))

    def rows_of(i):
        return pl.ds(pl.multiple_of(i * c, c), c)

    def phase1(i, carry):
        rows = rows_of(i)
        q = q_ref[0, rows, :]
        v = v_ref[0, rows, :]
        vb = v.astype(BF16)
        for reverse, z_ref, lb_ref, o_scr, qh_scr, upd_scr, dec_scr in (
                (False, zf_ref, lbf_ref, of_scr, qhf_scr, updf_scr, decf_scr),
                (True, zb_ref, lbb_ref, ob_scr, qhb_scr, updb_scr, decb_scr)):
            a, key = _log_decay_and_key(z_ref[0, rows, :], lb_ref[0, 0:1, :], lb_ref[0, 1:2, :],
                                        lb_ref[0, 2:3, :])
            scores, cum = _chunk_scores(q, key, a, masks_ref, n_levels if reverse else 0, reverse)
            diag = jnp.sum(q * key, axis=-1, keepdims=True)
            o_scr[rows, :] = jnp.dot(scores.astype(BF16), vb, preferred_element_type=F32) + diag * v
            total = cum[0:1, :] if reverse else cum[c - 1:c, :]
            qh_scr[rows, :] = (q * jnp.exp(cum)).astype(BF16)
            kd = (key * jnp.exp(total - cum)).astype(BF16)
            upd_scr[i] = lax.dot_general(vb, kd, tn, preferred_element_type=F32)
            dec_scr[i] = jnp.broadcast_to(jnp.exp(total), (SUBLANES, LANES))
        return carry

    lax.fori_loop(0, n_chunks, phase1, 0)

    def phase2(i, carry):
        s_f, s_b = carry
        j = n_chunks - 1 - i
        stf_scr[i] = s_f.astype(BF16)
        stb_scr[j] = s_b.astype(BF16)
        s_f = s_f * decf_scr[i][0:1, :] + updf_scr[i]
        s_b = s_b * decb_scr[j][0:1, :] + updb_scr[j]
        return s_f, s_b

    zero = jnp.zeros((LANES, LANES), F32)
    lax.fori_loop(0, n_chunks, phase2, (zero, zero))

    def phase3(i, carry):
        rows = rows_of(i)
        o = of_scr[rows, :] + ob_scr[rows, :]
        o += lax.dot_general(qhf_scr[rows, :], stf_scr[i], nt, preferred_element_type=F32)
        o += lax.dot_general(qhb_scr[rows, :], stb_scr[i], nt, preferred_element_type=F32)
        o = _rmsnorm_rows(o, gain_ref[0])
        g = g_ref[0, rows, :]
        o_ref[0, rows, :] = (o * (g * jax.nn.sigmoid(g))).astype(o_ref.dtype)
        return carry

    lax.fori_loop(0, n_chunks, phase3, 0)


def _level_masks(chunk):
    idx = jnp.arange(chunk)
    t, s = idx[:, None], idx[None, :]
    fwd, bwd = [], []
    h = 1
    while h < chunk:
        same = (t // (2 * h)) == (s // (2 * h))
        t_up = (t & h) != 0
        s_up = (s & h) != 0
        fwd.append(same & t_up & ~s_up)
        bwd.append(same & ~t_up & s_up)
        h *= 2
    return jnp.stack(fwd + bwd).astype(F32)


def hgrn2_mixer(p, lb_f, lb_b, out_gain, *, heads, chunk=HGRN_CHUNK):
    b, s, _ = p.shape
    n_chunks = s // chunk
    n_levels = int(math.log2(chunk))

    def lb_rows(lb):
        lb = lb.astype(F32).reshape(heads, 1, LANES)
        rows = jnp.concatenate([jnp.log(lb), jnp.log1p(-lb), 1.0 - lb], axis=1)
        return jnp.pad(rows, ((0, 0), (0, SUBLANES - 3), (0, 0)))

    def col(block0):
        return pl.BlockSpec((1, s, LANES), lambda bi, hi: (bi, 0, block0 + hi))

    per_head = pl.BlockSpec((1, SUBLANES, LANES), lambda bi, hi: (hi, 0, 0))
    masks = _level_masks(chunk)
    kern = functools.partial(_hgrn_kernel, chunk=chunk, n_chunks=n_chunks, n_levels=n_levels)
    seq_f32 = pltpu.VMEM((s, LANES), F32)
    seq_bf16 = pltpu.VMEM((s, LANES), BF16)
    return pl.pallas_call(
        kern,
        grid=(b, heads),
        in_specs=[
            col(COL_RQ // LANES), col(COL_RI // LANES), col(COL_ZF // LANES), col(COL_ZB // LANES),
            col(COL_RG // LANES), per_head, per_head,
            pl.BlockSpec((1, 1, LANES), lambda bi, hi: (hi, 0, 0)),
            pl.BlockSpec(masks.shape, lambda bi, hi: (0, 0, 0)),
        ],
        out_specs=pl.BlockSpec((1, s, LANES), lambda bi, hi: (bi, 0, hi)),
        out_shape=jax.ShapeDtypeStruct((b, s, heads * LANES), BF16),
        scratch_shapes=[
            seq_f32, seq_f32, seq_bf16, seq_bf16,
            pltpu.VMEM((n_chunks, LANES, LANES), F32), pltpu.VMEM((n_chunks, LANES, LANES), F32),
            pltpu.VMEM((n_chunks, SUBLANES, LANES), F32), pltpu.VMEM((n_chunks, SUBLANES, LANES), F32),
            pltpu.VMEM((n_chunks, LANES, LANES), BF16), pltpu.VMEM((n_chunks, LANES, LANES), BF16),
        ],
        compiler_params=_params("parallel", "parallel"),
        name="hgrn2",
    )(p, p, p, p, p, lb_rows(lb_f), lb_rows(lb_b), out_gain.astype(F32).reshape(heads, 1, LANES), masks)


def _attn_kernel(sink_ref, q_ref, kp_ref, kc_ref, kn_ref, vp_ref, vc_ref, vn_ref, o_ref, *, n_blocks):
    n = pl.program_id(1)
    qb = Q_BLOCK
    qi = lax.broadcasted_iota(jnp.int32, (qb, 3 * qb), 0)
    kj = lax.broadcasted_iota(jnp.int32, (qb, 3 * qb), 1)
    rel = kj - qb - qi
    dist = jnp.abs(rel)
    kpos = n * qb + kj - qb
    valid = (dist <= WINDOW) & (kpos >= 0) & (kpos < n_blocks * qb)
    dist_f = dist.astype(F32)
    scale = 1.0 / math.sqrt(ATT_HD)
    nt = (((1,), (1,)), ((), ()))
    for kvh in range(ATT_KV_HEADS):
        cols = slice(kvh * ATT_HD, (kvh + 1) * ATT_HD)
        k = jnp.concatenate([kp_ref[0, :, cols], kc_ref[0, :, cols], kn_ref[0, :, cols]], axis=0).astype(BF16)
        v = jnp.concatenate([vp_ref[0, :, cols], vc_ref[0, :, cols], vn_ref[0, :, cols]], axis=0).astype(BF16)
        for g in range(ATT_GROUP):
            head = kvh * ATT_GROUP + g
            slope = 2.0 ** (-8.0 * (head + 1) / ATT_HEADS)
            q = q_ref[0, :, head * ATT_HD:(head + 1) * ATT_HD].astype(BF16)
            s = lax.dot_general(q, k, nt, preferred_element_type=F32) * scale - slope * dist_f
            s = jnp.where(valid, s, -jnp.inf)
            sink = sink_ref[head]
            m = jnp.maximum(jnp.max(s, axis=-1, keepdims=True), sink)
            e = jnp.exp(s - m)
            denom = jnp.sum(e, axis=-1, keepdims=True) + jnp.exp(sink - m)
            o = jnp.dot(e.astype(BF16), v, preferred_element_type=F32) / denom
            o_ref[0, :, head * ATT_HD:(head + 1) * ATT_HD] = o.astype(o_ref.dtype)


def window_gqa(p, sink):
    b, s, _ = p.shape
    nb = s // Q_BLOCK
    kcol, vcol = COL_AK // KV_WIDTH, COL_AV // KV_WIDTH

    def kv(col, off):
        return pl.BlockSpec((1, Q_BLOCK, KV_WIDTH),
                            lambda bi, n, sk: (bi, jnp.clip(n + off, 0, nb - 1), col))

    return pl.pallas_call(
        functools.partial(_attn_kernel, n_blocks=nb),
        grid_spec=pltpu.PrefetchScalarGridSpec(
            num_scalar_prefetch=1,
            grid=(b, nb),
            in_specs=[
                pl.BlockSpec((1, Q_BLOCK, ATT_WIDTH), lambda bi, n, sk: (bi, n, COL_AQ // ATT_WIDTH)),
                kv(kcol, -1), kv(kcol, 0), kv(kcol, 1),
                kv(vcol, -1), kv(vcol, 0), kv(vcol, 1),
            ],
            out_specs=pl.BlockSpec((1, Q_BLOCK, ATT_WIDTH), lambda bi, n, sk: (bi, n, 0)),
        ),
        out_shape=jax.ShapeDtypeStruct((b, s, ATT_WIDTH), BF16),
        compiler_params=_params("parallel", "parallel"),
        name="window_gqa",
    )(sink.astype(F32), p, p, p, p, p, p, p)


def _merge_kernel(oa_ref, ob_ref, wa_ref, wb_ref, ga_ref, gb_ref, m_ref):
    ya = jnp.dot(oa_ref[...], wa_ref[...].astype(BF16), preferred_element_type=F32)
    yb = jnp.dot(ob_ref[...], wb_ref[...].astype(BF16), preferred_element_type=F32)
    m_ref[...] = (jax.nn.sigmoid(ga_ref[...]) * ya + jax.nn.sigmoid(gb_ref[...]) * yb).astype(m_ref.dtype)


def merge_branches(oa, ob, wa, wb, p, l, *, tm=TM_MERGE, tn=TN_MERGE):
    t, ka = oa.shape
    kb = ob.shape[1]
    n = wa.shape[2]
    assert COL_GA % tn == 0 and COL_GB % tn == 0
    return pl.pallas_call(
        _merge_kernel,
        grid=(t // tm, n // tn),
        in_specs=[
            pl.BlockSpec((tm, ka), lambda i, j: (i, 0)),
            pl.BlockSpec((tm, kb), lambda i, j: (i, 0)),
            pl.BlockSpec((None, ka, tn), lambda i, j: (l, 0, j)),
            pl.BlockSpec((None, kb, tn), lambda i, j: (l, 0, j)),
            pl.BlockSpec((tm, tn), lambda i, j: (i, COL_GA // tn + j)),
            pl.BlockSpec((tm, tn), lambda i, j: (i, COL_GB // tn + j)),
        ],
        out_specs=pl.BlockSpec((tm, tn), lambda i, j: (i, j)),
        out_shape=jax.ShapeDtypeStruct((t, n), BF16),
        compiler_params=_params("parallel", "parallel"),
        name="merge_branches",
    )(oa, ob, wa, wb, p, p)


def _final_norm_kernel(x_ref, gain_ref, o_ref):
    o_ref[...] = _rmsnorm_rows(x_ref[...], gain_ref[...])


def final_rmsnorm(x, gain, *, tm=TM_NORM):
    t, d = x.shape
    return pl.pallas_call(
        _final_norm_kernel,
        grid=(t // tm,),
        in_specs=[pl.BlockSpec((tm, d), lambda i: (i, 0)), pl.BlockSpec((1, d), lambda i: (0, 0))],
        out_specs=pl.BlockSpec((tm, d), lambda i: (i, 0)),
        out_shape=jax.ShapeDtypeStruct((t, d), F32),
        compiler_params=_params("parallel"),
        name="final_norm",
    )(x, gain.reshape(1, d))


def _lower_bounds(lb_logits):
    lb = jnp.cumsum(jax.nn.softmax(lb_logits.astype(F32), axis=0), axis=0)
    return lb - lb[0:1]


def kernel(x, ffn1_norm, ffn1_w_gate, ffn1_w_up, ffn1_w_down, mix_norm, w_in, lb_fwd_logits, lb_bwd_logits, rg_out_norm, attn_sink, w_branch_a, w_branch_b, w_out, ffn2_norm, ffn2_w_gate, ffn2_w_up, ffn2_w_down, final_norm):
    b, s, d = x.shape
    t = b * s
    depth = w_in.shape[0]
    lb_f_all = _lower_bounds(lb_fwd_logits)
    lb_b_all = _lower_bounds(lb_bwd_logits)
    xf = x.reshape(t, d)

    def ffn(xf, gain, wg, wu, wd, l):
        a = ffn_up(xf, gain, wg, wu, l)
        return resid_mm(a, wd, xf, l, scale=0.5)

    for l in range(depth):
        xf = ffn(xf, ffn1_norm, ffn1_w_gate, ffn1_w_up, ffn1_w_down, l)
        p = in_proj(xf, mix_norm, w_in, l)
        p3 = p.reshape(b, s, IN_COLS)
        o_a = hgrn2_mixer(p3, lb_f_all[l], lb_b_all[l], rg_out_norm[l], heads=RG_HEADS)
        o_b = window_gqa(p3, attn_sink[l])
        merged = merge_branches(o_a.reshape(t, RG_WIDTH), o_b.reshape(t, ATT_WIDTH),
                                w_branch_a, w_branch_b, p, l)
        xf = resid_mm(merged, w_out, xf, l, scale=1.0)
        xf = ffn(xf, ffn2_norm, ffn2_w_gate, ffn2_w_up, ffn2_w_down, l)
    return final_rmsnorm(xf, final_norm).reshape(b, s, d)
```

```python
import functools
import math

import jax
import jax.numpy as jnp
from jax import lax
from jax.experimental import pallas as pl
from jax.experimental.pallas import tpu as pltpu

D_MODEL = 2048
D_FF = 5632
RG_HEADS = 8
RG_DK = 128
RG_DV = 128
RG_WIDTH = RG_HEADS * RG_DV
ATT_HEADS = 8
ATT_KV_HEADS = 2
ATT_GROUP = ATT_HEADS // ATT_KV_HEADS
ATT_HD = 128
ATT_WIDTH = ATT_HEADS * ATT_HD
KV_WIDTH = ATT_KV_HEADS * ATT_HD
WINDOW = 128
Q_BLOCK = 128
IN_COLS = 5 * RG_WIDTH + ATT_WIDTH + 2 * KV_WIDTH + 2 * D_MODEL
EPS = 1e-6

COL_RQ = 0
COL_RI = RG_WIDTH
COL_ZF = 2 * RG_WIDTH
COL_ZB = 3 * RG_WIDTH
COL_RG = 4 * RG_WIDTH
COL_AQ = 5 * RG_WIDTH
COL_AK = COL_AQ + ATT_WIDTH
COL_AV = COL_AK + KV_WIDTH
COL_GA = COL_AV + KV_WIDTH
COL_GB = COL_GA + D_MODEL

LANES = 128
SUBLANES = 8
HGRN_CHUNK = 128
VMEM_LIMIT_BYTES = 56 * 1024 * 1024
LOG2E = 1.4426950408889634
SAFE_LOG2_RANGE = 240.0

TM = 2048
TM_DOWN = 1024
TM_MERGE = 1024
TN_FFN_UP = 256
TN_IN_PROJ = 512
TN_MERGE = 512
TN_DOWN = 256
TN_OUT = 512
TM_NORM = 512

BF16 = jnp.bfloat16
F32 = jnp.float32
NT_DIMS = (((1,), (1,)), ((), ()))
TN_DIMS = (((0,), (0,)), ((), ()))


def _params(*sem):
    return pltpu.CompilerParams(dimension_semantics=sem, vmem_limit_bytes=VMEM_LIMIT_BYTES)


def _rmsnorm_rows(x, gain):
    ms = jnp.mean(x * x, axis=-1, keepdims=True)
    return x * lax.rsqrt(ms + EPS) * gain


def _once(block_shape, index_map):
    return pl.BlockSpec(block_shape, index_map, pipeline_mode=pl.Buffered(1))


def _ffn_up_kernel(x_ref, gain_ref, wg_ref, wu_ref, a_ref, h_scr):
    @pl.when(pl.program_id(1) == 0)
    def _():
        h_scr[...] = _rmsnorm_rows(x_ref[...], gain_ref[...]).astype(BF16)

    h = h_scr[...]
    g = jnp.dot(h, wg_ref[...].astype(BF16), preferred_element_type=F32)
    u = jnp.dot(h, wu_ref[...].astype(BF16), preferred_element_type=F32)
    a_ref[...] = (g * jax.nn.sigmoid(g) * u).astype(BF16)


def ffn_up(x, gain, wg, wu, l, *, tm=TM, tn=TN_FFN_UP):
    t, d = x.shape
    f = wg.shape[2]
    return pl.pallas_call(
        _ffn_up_kernel,
        grid=(t // tm, f // tn),
        in_specs=[
            _once((tm, d), lambda i, j: (i, 0)),
            pl.BlockSpec((None, 1, d), lambda i, j: (l, 0, 0)),
            pl.BlockSpec((None, d, tn), lambda i, j: (l, 0, j)),
            pl.BlockSpec((None, d, tn), lambda i, j: (l, 0, j)),
        ],
        out_specs=pl.BlockSpec((tm, tn), lambda i, j: (i, j)),
        out_shape=jax.ShapeDtypeStruct((t, f), BF16),
        scratch_shapes=[pltpu.VMEM((tm, d), BF16)],
        compiler_params=_params("parallel", "arbitrary"),
        name="ffn_up",
    )(x, gain.reshape(gain.shape[0], 1, d), wg, wu)


def _resid_mm_kernel(a_ref, w_ref, x_ref, o_ref, *, scale):
    y = jnp.dot(a_ref[...], w_ref[...].astype(BF16), preferred_element_type=F32)
    o_ref[...] = x_ref[...] + (y if scale == 1.0 else scale * y)


def resid_mm(a, w, x, l, *, scale, tm, tn):
    t, kdim = a.shape
    n = w.shape[2]
    return pl.pallas_call(
        functools.partial(_resid_mm_kernel, scale=scale),
        grid=(t // tm, n // tn),
        in_specs=[
            pl.BlockSpec((tm, kdim), lambda i, j: (i, 0)),
            pl.BlockSpec((None, kdim, tn), lambda i, j: (l, 0, j)),
            pl.BlockSpec((tm, tn), lambda i, j: (i, j)),
        ],
        out_specs=pl.BlockSpec((tm, tn), lambda i, j: (i, j)),
        out_shape=jax.ShapeDtypeStruct((t, n), F32),
        compiler_params=_params("parallel", "arbitrary"),
        name="resid_mm",
    )(a, w, x)


def _in_proj_kernel(x_ref, gain_ref, w_ref, p_ref, h_scr):
    @pl.when(pl.program_id(1) == 0)
    def _():
        h_scr[...] = _rmsnorm_rows(x_ref[...], gain_ref[...]).astype(BF16)

    p_ref[...] = jnp.dot(h_scr[...], w_ref[...].astype(BF16), preferred_element_type=F32)


def in_proj(x, gain, w, l, *, tm=TM, tn=TN_IN_PROJ):
    t, d = x.shape
    n = w.shape[2]
    return pl.pallas_call(
        _in_proj_kernel,
        grid=(t // tm, n // tn),
        in_specs=[
            _once((tm, d), lambda i, j: (i, 0)),
            pl.BlockSpec((None, 1, d), lambda i, j: (l, 0, 0)),
            pl.BlockSpec((None, d, tn), lambda i, j: (l, 0, j)),
        ],
        out_specs=pl.BlockSpec((tm, tn), lambda i, j: (i, j)),
        out_shape=jax.ShapeDtypeStruct((t, n), F32),
        scratch_shapes=[pltpu.VMEM((tm, d), BF16)],
        compiler_params=_params("parallel", "arbitrary"),
        name="in_proj",
    )(x, gain.reshape(gain.shape[0], 1, d), w)


def _block_ref_rows(p, h, r):
    c, n = p.shape
    blk = 2 * h
    if blk >= SUBLANES:
        p3 = p.reshape(c // blk, blk, n)
        return jnp.broadcast_to(p3[:, r:r + 1, :], p3.shape).reshape(c, n)
    pos = lax.broadcasted_iota(jnp.int32, p.shape, 0) & (blk - 1)
    out = p
    for src in range(blk):
        if src == r:
            continue
        shifted = pltpu.roll(p, (src - r) % c, axis=0)
        out = jnp.where(pos == src, shifted, out)
    return out


def _chunk_scores(q, key, a, masks_ref, mask_base, reverse):
    c, n = q.shape
    row = lax.broadcasted_iota(jnp.int32, q.shape, 0)
    p = a
    scores = None
    h = 1
    level = 0
    while h < c:
        if h >= SUBLANES:
            nb = c // (2 * h)
            p4 = p.reshape(nb, 2, h, n)
            q4 = q.reshape(nb, 2, h, n)
            k4 = key.reshape(nb, 2, h, n)
            lo, hi = p4[:, 0], p4[:, 1]
            if reverse:
                t_row = hi[:, 0:1, :]
                qk = jnp.stack([q4[:, 0] * jnp.exp(lo), k4[:, 1] * jnp.exp(t_row - hi)], axis=1)
                p = jnp.stack([lo + t_row, hi], axis=1).reshape(c, n)
            else:
                t_row = lo[:, h - 1:h, :]
                qk = jnp.stack([k4[:, 0] * jnp.exp(t_row - lo), q4[:, 1] * jnp.exp(hi)], axis=1)
                p = jnp.stack([lo, hi + t_row], axis=1).reshape(c, n)
            qk = qk.reshape(c, n).astype(BF16)
        else:
            upper = (row & h) != 0
            qside = jnp.logical_not(upper) if reverse else upper
            t_rows = _block_ref_rows(p, h, h if reverse else h - 1)
            e = jnp.where(qside, p, t_rows - p)
            qk = (jnp.where(qside, q, key) * jnp.exp(e)).astype(BF16)
            p = p + jnp.where(qside, t_rows, 0.0)
        s = lax.dot_general(qk, qk, NT_DIMS, preferred_element_type=F32)
        s = s * masks_ref[mask_base + level]
        scores = s if scores is None else scores + s
        h *= 2
        level += 1
    return scores, p


def _log_decay_and_key(z, log_lb, log1m_lb, one_m_lb):
    u = jnp.exp(-jnp.abs(z))
    one_u = 1.0 + u
    y = log1m_lb + (jnp.minimum(z, 0.0) - jnp.log(one_u))
    log_f = jnp.maximum(log_lb, y) + jnp.log(1.0 + jnp.exp(-jnp.abs(log_lb - y)))
    key = one_m_lb * (jnp.where(z >= 0.0, u, 1.0) / one_u)
    return log_f, key


def _cumsum_rows(a, tri_bf16):
    a1 = a.astype(BF16)
    r1 = a - a1.astype(F32)
    a2 = r1.astype(BF16)
    a3 = (r1 - a2.astype(F32)).astype(BF16)
    out = jnp.dot(tri_bf16, jnp.concatenate([a1, a2, a3], axis=1), preferred_element_type=F32)
    n = a.shape[1]
    return (out[:, 2 * n:] + out[:, n:2 * n]) + out[:, :n]


def _hgrn_kernel(q_ref, v_ref, zf_ref, zb_ref, g_ref, lbf_ref, lbb_ref, gain_ref, masks_ref, tri_ref, o_ref,
                 of_scr, ob_scr, qhf_scr, qhb_scr, updf_scr, updb_scr, decf_scr, decb_scr, stf_scr, stb_scr,
                 cumf_scr, cumb_scr, keyf_scr, keyb_scr, totf_scr, totb_scr,
                 *, chunk, n_chunks, n_levels):
    c = chunk

    def rows_of(i):
        return pl.ds(pl.multiple_of(i * c, c), c)

    dirs = (
        (False, zf_ref, lbf_ref, of_scr, qhf_scr, updf_scr, decf_scr, cumf_scr, keyf_scr, totf_scr),
        (True, zb_ref, lbb_ref, ob_scr, qhb_scr, updb_scr, decb_scr, cumb_scr, keyb_scr, totb_scr),
    )

    def gates(z_ref, lb_ref, rows):
        return _log_decay_and_key(z_ref[0, rows, :], lb_ref[0, 0:1, :], lb_ref[0, 1:2, :], lb_ref[0, 2:3, :])

    def phase0(i, lowest):
        rows = rows_of(i)
        for reverse, z_ref, lb_ref, _, _, _, dec_scr, cum_scr, key_scr, tot_scr in dirs:
            a, key = gates(z_ref, lb_ref, rows)
            cum = _cumsum_rows(a * LOG2E, tri_ref[1 if reverse else 0])
            total = cum[0:1, :] if reverse else cum[c - 1:c, :]
            cum_scr[rows, :] = cum
            key_scr[rows, :] = key
            tot_scr[i] = jnp.broadcast_to(total, (SUBLANES, LANES))
            dec_scr[i] = jnp.broadcast_to(jnp.exp2(total), (SUBLANES, LANES))
            lowest = jnp.minimum(lowest, total)
        return lowest

    lowest = lax.fori_loop(0, n_chunks, phase0, jnp.zeros((1, LANES), F32), unroll=2)
    safe = jnp.min(lowest) > -SAFE_LOG2_RANGE

    @pl.when(safe)
    def _():
        ti = lax.broadcasted_iota(jnp.int32, (c, c), 0)
        si = lax.broadcasted_iota(jnp.int32, (c, c), 1)

        def body(i, carry):
            rows = rows_of(i)
            q = q_ref[0, rows, :]
            vb = v_ref[0, rows, :].astype(BF16)
            for reverse, _, _, o_scr, qh_scr, upd_scr, _, cum_scr, key_scr, tot_scr in dirs:
                cum = cum_scr[rows, :]
                half = tot_scr[i][0:1, :] * 0.5
                qt = q * jnp.exp2(cum - half)
                kt = key_scr[rows, :] * jnp.exp2(half - cum)
                s = lax.dot_general(qt.astype(BF16), kt.astype(BF16), NT_DIMS, preferred_element_type=F32)
                s = jnp.where((si >= ti) if reverse else (si <= ti), s, 0.0)
                o_scr[rows, :] = jnp.dot(s.astype(BF16), vb, preferred_element_type=F32)
                edge = jnp.exp2(half)
                qh_scr[rows, :] = (qt * edge).astype(BF16)
                upd_scr[i] = lax.dot_general(vb, (kt * edge).astype(BF16), TN_DIMS,
                                             preferred_element_type=F32)
            return carry

        lax.fori_loop(0, n_chunks, body, 0, unroll=2)

    @pl.when(jnp.logical_not(safe))
    def _():
        def body(i, carry):
            rows = rows_of(i)
            q = q_ref[0, rows, :]
            v = v_ref[0, rows, :]
            vb = v.astype(BF16)
            for reverse, z_ref, lb_ref, o_scr, qh_scr, upd_scr, _, _, _, _ in dirs:
                a, key = gates(z_ref, lb_ref, rows)
                scores, cum = _chunk_scores(q, key, a, masks_ref, n_levels if reverse else 0, reverse)
                diag = jnp.sum(q * key, axis=-1, keepdims=True)
                o_scr[rows, :] = jnp.dot(scores.astype(BF16), vb, preferred_element_type=F32) + diag * v
                total = cum[0:1, :] if reverse else cum[c - 1:c, :]
                qh_scr[rows, :] = (q * jnp.exp(cum)).astype(BF16)
                kd = (key * jnp.exp(total - cum)).astype(BF16)
                upd_scr[i] = lax.dot_general(vb, kd, TN_DIMS, preferred_element_type=F32)
            return carry

        lax.fori_loop(0, n_chunks, body, 0)

    def phase2(i, carry):
        s_f, s_b = carry
        j = n_chunks - 1 - i
        stf_scr[i] = s_f.astype(BF16)
        stb_scr[j] = s_b.astype(BF16)
        s_f = s_f * decf_scr[i][0:1, :] + updf_scr[i]
        s_b = s_b * decb_scr[j][0:1, :] + updb_scr[j]
        return s_f, s_b

    zero = jnp.zeros((LANES, LANES), F32)
    lax.fori_loop(0, n_chunks, phase2, (zero, zero))

    def phase3(i, carry):
        rows = rows_of(i)
        o = of_scr[rows, :] + ob_scr[rows, :]
        o += lax.dot_general(qhf_scr[rows, :], stf_scr[i], NT_DIMS, preferred_element_type=F32)
        o += lax.dot_general(qhb_scr[rows, :], stb_scr[i], NT_DIMS, preferred_element_type=F32)
        o = _rmsnorm_rows(o, gain_ref[0])
        g = g_ref[0, rows, :]
        o_ref[0, rows, :] = (o * (g * jax.nn.sigmoid(g))).astype(o_ref.dtype)
        return carry

    lax.fori_loop(0, n_chunks, phase3, 0, unroll=2)


def _level_masks(chunk):
    idx = jnp.arange(chunk)
    t, s = idx[:, None], idx[None, :]
    fwd, bwd = [], []
    h = 1
    while h < chunk:
        same = (t // (2 * h)) == (s // (2 * h))
        t_up = (t & h) != 0
        s_up = (s & h) != 0
        fwd.append(same & t_up & ~s_up)
        bwd.append(same & ~t_up & s_up)
        h *= 2
    return jnp.stack(fwd + bwd).astype(F32)


def hgrn2_mixer(p, lb_f, lb_b, out_gain, *, heads, chunk=HGRN_CHUNK):
    b, s, _ = p.shape
    n_chunks = s // chunk
    n_levels = int(math.log2(chunk))

    def lb_rows(lb):
        lb = lb.astype(F32).reshape(heads, 1, LANES)
        rows = jnp.concatenate([jnp.log(lb), jnp.log1p(-lb), 1.0 - lb], axis=1)
        return jnp.pad(rows, ((0, 0), (0, SUBLANES - 3), (0, 0)))

    def col(block0):
        return pl.BlockSpec((1, s, LANES), lambda bi, hi: (bi, 0, block0 + hi))

    per_head = pl.BlockSpec((1, SUBLANES, LANES), lambda bi, hi: (hi, 0, 0))
    masks = _level_masks(chunk)
    idx = jnp.arange(chunk)
    lower = idx[None, :] <= idx[:, None]
    tri = jnp.stack([lower, lower.T]).astype(BF16)
    kern = functools.partial(_hgrn_kernel, chunk=chunk, n_chunks=n_chunks, n_levels=n_levels)
    seq_f32 = pltpu.VMEM((s, LANES), F32)
    seq_bf16 = pltpu.VMEM((s, LANES), BF16)
    chunk_mat_f32 = pltpu.VMEM((n_chunks, LANES, LANES), F32)
    chunk_mat_bf16 = pltpu.VMEM((n_chunks, LANES, LANES), BF16)
    chunk_row = pltpu.VMEM((n_chunks, SUBLANES, LANES), F32)
    return pl.pallas_call(
        kern,
        grid=(b, heads),
        in_specs=[
            col(COL_RQ // LANES), col(COL_RI // LANES), col(COL_ZF // LANES), col(COL_ZB // LANES),
            col(COL_RG // LANES), per_head, per_head,
            pl.BlockSpec((1, 1, LANES), lambda bi, hi: (hi, 0, 0)),
            pl.BlockSpec(masks.shape, lambda bi, hi: (0, 0, 0)),
            pl.BlockSpec(tri.shape, lambda bi, hi: (0, 0, 0)),
        ],
        out_specs=pl.BlockSpec((1, s, LANES), lambda bi, hi: (bi, 0, hi)),
        out_shape=jax.ShapeDtypeStruct((b, s, heads * LANES), BF16),
        scratch_shapes=[
            seq_f32, seq_f32, seq_bf16, seq_bf16,
            chunk_mat_f32, chunk_mat_f32, chunk_row, chunk_row,
            chunk_mat_bf16, chunk_mat_bf16,
            seq_f32, seq_f32, seq_f32, seq_f32, chunk_row, chunk_row,
        ],
        compiler_params=_params("parallel", "parallel"),
        name="hgrn2",
    )(p, p, p, p, p, lb_rows(lb_f), lb_rows(lb_b), out_gain.astype(F32).reshape(heads, 1, LANES), masks, tri)


def _attn_kernel(sink_ref, q_ref, kp_ref, kc_ref, kn_ref, vp_ref, vc_ref, vn_ref, bias_ref, o_ref):
    qscale = LOG2E / math.sqrt(ATT_HD)
    qb = Q_BLOCK

    def head_cols(ref, head):
        return ref[0, :, head * ATT_HD:(head + 1) * ATT_HD]

    def band(p_ref, c_ref, n_ref, kvh):
        return jnp.concatenate([head_cols(p_ref, kvh), head_cols(c_ref, kvh), head_cols(n_ref, kvh)],
                               axis=0).astype(BF16)

    for kvh in range(ATT_KV_HEADS):
        k = band(kp_ref, kc_ref, kn_ref, kvh)
        v1 = jnp.concatenate([band(vp_ref, vc_ref, vn_ref, kvh), jnp.ones((3 * qb, ATT_HD), BF16)], axis=1)
        for g in range(ATT_GROUP):
            head = kvh * ATT_GROUP + g
            q = (head_cols(q_ref, head) * qscale).astype(BF16)
            s = lax.dot_general(q, k, NT_DIMS, preferred_element_type=F32) + bias_ref[0, head * qb:(head + 1) * qb]
            sink = sink_ref[head] * LOG2E
            m = jnp.broadcast_to(jnp.maximum(jnp.max(s, axis=-1, keepdims=True), sink), (qb, ATT_HD))
            e = jnp.exp2(s - jnp.concatenate([m, m, m], axis=1)).astype(BF16)
            ov = jnp.dot(e, v1, preferred_element_type=F32)
            o = ov[:, :ATT_HD] / (ov[:, ATT_HD:] + jnp.exp2(sink - m))
            o_ref[0, :, head * ATT_HD:(head + 1) * ATT_HD] = o.astype(o_ref.dtype)


def _attn_bias():
    qi = jnp.arange(Q_BLOCK)[:, None]
    kj = jnp.arange(3 * Q_BLOCK)[None, :]
    dist = jnp.abs(kj - Q_BLOCK - qi)
    slopes = 2.0 ** (-8.0 * jnp.arange(1, ATT_HEADS + 1, dtype=F32) / ATT_HEADS)
    alibi = -(slopes[:, None, None] * dist.astype(F32)[None]) * LOG2E
    cases = []
    for has_prev, has_next in ((False, True), (True, True), (True, False)):
        valid = (dist <= WINDOW) & ((kj >= Q_BLOCK) | has_prev) & ((kj < 2 * Q_BLOCK) | has_next)
        cases.append(jnp.where(valid[None], alibi, -jnp.inf).reshape(ATT_HEADS * Q_BLOCK, 3 * Q_BLOCK))
    return jnp.stack(cases)


def window_gqa(p, sink):
    b, s, _ = p.shape
    nb = s // Q_BLOCK
    assert nb >= 2
    kcol, vcol = COL_AK // KV_WIDTH, COL_AV // KV_WIDTH

    def kv(col, off):
        return pl.BlockSpec((1, Q_BLOCK, KV_WIDTH),
                            lambda bi, n, sk: (bi, jnp.clip(n + off, 0, nb - 1), col))

    def bias_case(bi, n, sk):
        return (jnp.where(n == 0, 0, jnp.where(n == nb - 1, 2, 1)), 0, 0)

    return pl.pallas_call(
        _attn_kernel,
        grid_spec=pltpu.PrefetchScalarGridSpec(
            num_scalar_prefetch=1,
            grid=(b, nb),
            in_specs=[
                pl.BlockSpec((1, Q_BLOCK, ATT_WIDTH), lambda bi, n, sk: (bi, n, COL_AQ // ATT_WIDTH)),
                kv(kcol, -1), kv(kcol, 0), kv(kcol, 1),
                kv(vcol, -1), kv(vcol, 0), kv(vcol, 1),
                pl.BlockSpec((1, ATT_HEADS * Q_BLOCK, 3 * Q_BLOCK), bias_case),
            ],
            out_specs=pl.BlockSpec((1, Q_BLOCK, ATT_WIDTH), lambda bi, n, sk: (bi, n, 0)),
        ),
        out_shape=jax.ShapeDtypeStruct((b, s, ATT_WIDTH), BF16),
        compiler_params=_params("parallel", "arbitrary"),
        name="window_gqa",
    )(sink.astype(F32), p, p, p, p, p, p, p, _attn_bias())


def _merge_kernel(oa_ref, ob_ref, wa_ref, wb_ref, ga_ref, gb_ref, m_ref):
    ya = jnp.dot(oa_ref[...], wa_ref[...].astype(BF16), preferred_element_type=F32)
    yb = jnp.dot(ob_ref[...], wb_ref[...].astype(BF16), preferred_element_type=F32)
    m_ref[...] = (jax.nn.sigmoid(ga_ref[...]) * ya + jax.nn.sigmoid(gb_ref[...]) * yb).astype(m_ref.dtype)


def merge_branches(oa, ob, wa, wb, p, l, *, tm=TM_MERGE, tn=TN_MERGE):
    t, ka = oa.shape
    kb = ob.shape[1]
    n = wa.shape[2]
    assert COL_GA % tn == 0 and COL_GB % tn == 0
    return pl.pallas_call(
        _merge_kernel,
        grid=(t // tm, n // tn),
        in_specs=[
            pl.BlockSpec((tm, ka), lambda i, j: (i, 0)),
            pl.BlockSpec((tm, kb), lambda i, j: (i, 0)),
            pl.BlockSpec((None, ka, tn), lambda i, j: (l, 0, j)),
            pl.BlockSpec((None, kb, tn), lambda i, j: (l, 0, j)),
            pl.BlockSpec((tm, tn), lambda i, j: (i, COL_GA // tn + j)),
            pl.BlockSpec((tm, tn), lambda i, j: (i, COL_GB // tn + j)),
        ],
        out_specs=pl.BlockSpec((tm, tn), lambda i, j: (i, j)),
        out_shape=jax.ShapeDtypeStruct((t, n), BF16),
        compiler_params=_params("parallel", "parallel"),
        name="merge_branches",
    )(oa, ob, wa, wb, p, p)


def _final_norm_kernel(x_ref, gain_ref, o_ref):
    o_ref[...] = _rmsnorm_rows(x_ref[...], gain_ref[...])


def final_rmsnorm(x, gain, *, tm=TM_NORM):
    t, d = x.shape
    return pl.pallas_call(
        _final_norm_kernel,
        grid=(t // tm,),
        in_specs=[pl.BlockSpec((tm, d), lambda i: (i, 0)), pl.BlockSpec((1, d), lambda i: (0, 0))],
        out_specs=pl.BlockSpec((tm, d), lambda i: (i, 0)),
        out_shape=jax.ShapeDtypeStruct((t, d), F32),
        compiler_params=_params("parallel"),
        name="final_norm",
    )(x, gain.reshape(1, d))


def _lower_bounds(lb_logits):
    lb = jnp.cumsum(jax.nn.softmax(lb_logits.astype(F32), axis=0), axis=0)
    return lb - lb[0:1]


def kernel(x, ffn1_norm, ffn1_w_gate, ffn1_w_up, ffn1_w_down, mix_norm, w_in, lb_fwd_logits, lb_bwd_logits, rg_out_norm, attn_sink, w_branch_a, w_branch_b, w_out, ffn2_norm, ffn2_w_gate, ffn2_w_up, ffn2_w_down, final_norm):
    b, s, d = x.shape
    t = b * s
    depth = w_in.shape[0]
    lb_f_all = _lower_bounds(lb_fwd_logits)
    lb_b_all = _lower_bounds(lb_bwd_logits)
    xf = x.reshape(t, d)

    def ffn(xf, gain, wg, wu, wd, l):
        a = ffn_up(xf, gain, wg, wu, l)
        return resid_mm(a, wd, xf, l, scale=0.5, tm=TM_DOWN, tn=TN_DOWN)

    for l in range(depth):
        xf = ffn(xf, ffn1_norm, ffn1_w_gate, ffn1_w_up, ffn1_w_down, l)
        p = in_proj(xf, mix_norm, w_in, l)
        p3 = p.reshape(b, s, IN_COLS)
        o_a = hgrn2_mixer(p3, lb_f_all[l], lb_b_all[l], rg_out_norm[l], heads=RG_HEADS)
        o_b = window_gqa(p3, attn_sink[l])
        merged = merge_branches(o_a.reshape(t, RG_WIDTH), o_b.reshape(t, ATT_WIDTH),
                                w_branch_a, w_branch_b, p, l)
        xf = resid_mm(merged, w_out, xf, l, scale=1.0, tm=TM, tn=TN_OUT)
        xf = ffn(xf, ffn2_norm, ffn2_w_gate, ffn2_w_up, ffn2_w_down, l)
    return final_rmsnorm(xf, final_norm).reshape(b, s, d)
```

```python
import functools
import math

import jax
import jax.numpy as jnp
from jax import lax
from jax.experimental import pallas as pl
from jax.experimental.pallas import tpu as pltpu

D_MODEL = 2048
D_FF = 5632
RG_HEADS = 8
RG_DK = 128
RG_DV = 128
RG_WIDTH = RG_HEADS * RG_DV
ATT_HEADS = 8
ATT_KV_HEADS = 2
ATT_GROUP = ATT_HEADS // ATT_KV_HEADS
ATT_HD = 128
ATT_WIDTH = ATT_HEADS * ATT_HD
KV_WIDTH = ATT_KV_HEADS * ATT_HD
WINDOW = 128
Q_BLOCK = 128
IN_COLS = 5 * RG_WIDTH + ATT_WIDTH + 2 * KV_WIDTH + 2 * D_MODEL
EPS = 1e-6

COL_RQ = 0
COL_RI = RG_WIDTH
COL_ZF = 2 * RG_WIDTH
COL_ZB = 3 * RG_WIDTH
COL_RG = 4 * RG_WIDTH
COL_AQ = 5 * RG_WIDTH
COL_AK = COL_AQ + ATT_WIDTH
COL_AV = COL_AK + KV_WIDTH
COL_GA = COL_AV + KV_WIDTH
COL_GB = COL_GA + D_MODEL

LANES = 128
SUBLANES = 8
HGRN_CHUNK = 128
VMEM_LIMIT_BYTES = 56 * 1024 * 1024
LOG2E = 1.4426950408889634
LN2 = 0.6931471805599453
SAFE_LOG2_RANGE = 240.0

TM = 2048
TM_DOWN = 1024
TM_MERGE = 1024
TM_OUT = 1024
TN_FFN_UP = 512
TN_IN_PROJ = 512
TN_MERGE = 512
TN_DOWN = 256
TN_OUT = 1024
TM_NORM = 512
MXU_COLS = 256

BF16 = jnp.bfloat16
F32 = jnp.float32
NT_DIMS = (((1,), (1,)), ((), ()))
TN_DIMS = (((0,), (0,)), ((), ()))


def _params(*sem):
    return pltpu.CompilerParams(dimension_semantics=sem, vmem_limit_bytes=VMEM_LIMIT_BYTES)


def _rmsnorm_rows(x, gain):
    ms = jnp.mean(x * x, axis=-1, keepdims=True)
    return x * lax.rsqrt(ms + EPS) * gain


def _once(block_shape, index_map):
    return pl.BlockSpec(block_shape, index_map, pipeline_mode=pl.Buffered(1))


def _ffn_up_kernel(x_ref, gain_ref, wg_ref, wu_ref, a_ref, h_scr):
    @pl.when(pl.program_id(1) == 0)
    def _():
        h_scr[...] = _rmsnorm_rows(x_ref[...], gain_ref[...]).astype(BF16)

    h = h_scr[...]
    for c0 in range(0, a_ref.shape[1], MXU_COLS):
        cols = slice(c0, c0 + MXU_COLS)
        g = jnp.dot(h, wg_ref[:, cols].astype(BF16), preferred_element_type=F32)
        u = jnp.dot(h, wu_ref[:, cols].astype(BF16), preferred_element_type=F32)
        a_ref[:, cols] = (g * jax.nn.sigmoid(g) * u).astype(BF16)


def ffn_up(x, gain, wg, wu, l, *, tm=TM, tn=TN_FFN_UP):
    t, d = x.shape
    f = wg.shape[2]
    return pl.pallas_call(
        _ffn_up_kernel,
        grid=(t // tm, f // tn),
        in_specs=[
            _once((tm, d), lambda i, j: (i, 0)),
            pl.BlockSpec((None, 1, d), lambda i, j: (l, 0, 0)),
            pl.BlockSpec((None, d, tn), lambda i, j: (l, 0, j)),
            pl.BlockSpec((None, d, tn), lambda i, j: (l, 0, j)),
        ],
        out_specs=pl.BlockSpec((tm, tn), lambda i, j: (i, j)),
        out_shape=jax.ShapeDtypeStruct((t, f), BF16),
        scratch_shapes=[pltpu.VMEM((tm, d), BF16)],
        compiler_params=_params("parallel", "arbitrary"),
        name="ffn_up",
    )(x, gain.reshape(gain.shape[0], 1, d), wg, wu)


def _resid_mm_kernel(a_ref, w_ref, x_ref, o_ref, *, scale):
    y = jnp.dot(a_ref[...], w_ref[...].astype(BF16), preferred_element_type=F32)
    o_ref[...] = x_ref[...] + (y if scale == 1.0 else scale * y)


def resid_mm(a, w, x, l, *, scale, tm, tn):
    t, kdim = a.shape
    n = w.shape[2]
    return pl.pallas_call(
        functools.partial(_resid_mm_kernel, scale=scale),
        grid=(t // tm, n // tn),
        in_specs=[
            pl.BlockSpec((tm, kdim), lambda i, j: (i, 0)),
            pl.BlockSpec((None, kdim, tn), lambda i, j: (l, 0, j)),
            pl.BlockSpec((tm, tn), lambda i, j: (i, j)),
        ],
        out_specs=pl.BlockSpec((tm, tn), lambda i, j: (i, j)),
        out_shape=jax.ShapeDtypeStruct((t, n), F32),
        compiler_params=_params("parallel", "arbitrary"),
        name="resid_mm",
    )(a, w, x)


def _resid_mm_resident_kernel(a_ref, w_ref, x_ref, o_ref, w_scr, *, tn):
    @pl.when((pl.program_id(0) == 0) & (pl.program_id(1) == 0))
    def _():
        w_scr[...] = w_ref[...].astype(BF16)

    cols = pl.ds(pl.multiple_of(pl.program_id(1) * tn, tn), tn)
    o_ref[...] = x_ref[...] + jnp.dot(a_ref[...], w_scr[:, cols], preferred_element_type=F32)


def resid_mm_resident(a, w, x, l, *, tm, tn):
    t, kdim = a.shape
    n = w.shape[2]
    return pl.pallas_call(
        functools.partial(_resid_mm_resident_kernel, tn=tn),
        grid=(t // tm, n // tn),
        in_specs=[
            pl.BlockSpec((tm, kdim), lambda i, j: (i, 0)),
            _once((None, kdim, n), lambda i, j: (l, 0, 0)),
            pl.BlockSpec((tm, tn), lambda i, j: (i, j)),
        ],
        out_specs=pl.BlockSpec((tm, tn), lambda i, j: (i, j)),
        out_shape=jax.ShapeDtypeStruct((t, n), F32),
        scratch_shapes=[pltpu.VMEM((kdim, n), BF16)],
        compiler_params=_params("arbitrary", "arbitrary"),
        name="resid_mm_resident",
    )(a, w, x)


def _in_proj_kernel(x_ref, gain_ref, w_ref, p_ref, h_scr):
    @pl.when(pl.program_id(1) == 0)
    def _():
        h_scr[...] = _rmsnorm_rows(x_ref[...], gain_ref[...]).astype(BF16)

    p_ref[...] = jnp.dot(h_scr[...], w_ref[...].astype(BF16), preferred_element_type=F32)


def in_proj(x, gain, w, l, *, tm=TM, tn=TN_IN_PROJ):
    t, d = x.shape
    n = w.shape[2]
    return pl.pallas_call(
        _in_proj_kernel,
        grid=(t // tm, n // tn),
        in_specs=[
            _once((tm, d), lambda i, j: (i, 0)),
            pl.BlockSpec((None, 1, d), lambda i, j: (l, 0, 0)),
            pl.BlockSpec((None, d, tn), lambda i, j: (l, 0, j)),
        ],
        out_specs=pl.BlockSpec((tm, tn), lambda i, j: (i, j)),
        out_shape=jax.ShapeDtypeStruct((t, n), F32),
        scratch_shapes=[pltpu.VMEM((tm, d), BF16)],
        compiler_params=_params("parallel", "arbitrary"),
        name="in_proj",
    )(x, gain.reshape(gain.shape[0], 1, d), w)


def _block_ref_rows(p, h, r):
    c, n = p.shape
    blk = 2 * h
    if blk >= SUBLANES:
        p3 = p.reshape(c // blk, blk, n)
        return jnp.broadcast_to(p3[:, r:r + 1, :], p3.shape).reshape(c, n)
    pos = lax.broadcasted_iota(jnp.int32, p.shape, 0) & (blk - 1)
    out = p
    for src in range(blk):
        if src == r:
            continue
        shifted = pltpu.roll(p, (src - r) % c, axis=0)
        out = jnp.where(pos == src, shifted, out)
    return out


def _chunk_scores(q, key, a, masks_ref, mask_base, reverse):
    c, n = q.shape
    row = lax.broadcasted_iota(jnp.int32, q.shape, 0)
    p = a
    scores = None
    h = 1
    level = 0
    while h < c:
        if h >= SUBLANES:
            nb = c // (2 * h)
            p4 = p.reshape(nb, 2, h, n)
            q4 = q.reshape(nb, 2, h, n)
            k4 = key.reshape(nb, 2, h, n)
            lo, hi = p4[:, 0], p4[:, 1]
            if reverse:
                t_row = hi[:, 0:1, :]
                qk = jnp.stack([q4[:, 0] * jnp.exp(lo), k4[:, 1] * jnp.exp(t_row - hi)], axis=1)
                p = jnp.stack([lo + t_row, hi], axis=1).reshape(c, n)
            else:
                t_row = lo[:, h - 1:h, :]
                qk = jnp.stack([k4[:, 0] * jnp.exp(t_row - lo), q4[:, 1] * jnp.exp(hi)], axis=1)
                p = jnp.stack([lo, hi + t_row], axis=1).reshape(c, n)
            qk = qk.reshape(c, n).astype(BF16)
        else:
            upper = (row & h) != 0
            qside = jnp.logical_not(upper) if reverse else upper
            t_rows = _block_ref_rows(p, h, h if reverse else h - 1)
            e = jnp.where(qside, p, t_rows - p)
            qk = (jnp.where(qside, q, key) * jnp.exp(e)).astype(BF16)
            p = p + jnp.where(qside, t_rows, 0.0)
        s = lax.dot_general(qk, qk, NT_DIMS, preferred_element_type=F32)
        s = s * masks_ref[mask_base + level]
        scores = s if scores is None else scores + s
        h *= 2
        level += 1
    return scores, p


def _log2_decay_and_key(z, lb, log1m_lb, one_m_lb):
    u = jnp.exp(-jnp.abs(z))
    one_u = 1.0 + u
    r = 1.0 / one_u
    pos = z >= 0.0
    sig = jnp.where(pos, 1.0, u) * r
    key = one_m_lb * (jnp.where(pos, u, 1.0) * r)
    y2 = (log1m_lb + jnp.minimum(z, 0.0)) * LOG2E - jnp.log2(one_u)
    return jnp.maximum(jnp.log2(lb + one_m_lb * sig), y2), key


def _cumsum_rows(a, tri_bf16):
    a1 = a.astype(BF16)
    r1 = a - a1.astype(F32)
    a2 = r1.astype(BF16)
    a3 = (r1 - a2.astype(F32)).astype(BF16)
    out = jnp.dot(tri_bf16, jnp.concatenate([a1, a2, a3], axis=1), preferred_element_type=F32)
    n = a.shape[1]
    return (out[:, 2 * n:] + out[:, n:2 * n]) + out[:, :n]


def _hgrn_kernel(q_ref, v_ref, zf_ref, zb_ref, g_ref, lbf_ref, lbb_ref, gain_ref, masks_ref, tri_ref, o_ref,
                 of_scr, ob_scr, qhf_scr, qhb_scr, updf_scr, updb_scr, decf_scr, decb_scr, stf_scr, stb_scr,
                 cumf_scr, cumb_scr, keyf_scr, keyb_scr, totf_scr, totb_scr,
                 *, chunk, n_chunks, n_levels):
    c = chunk

    def rows_of(i):
        return pl.ds(pl.multiple_of(i * c, c), c)

    dirs = (
        (False, zf_ref, lbf_ref, of_scr, qhf_scr, updf_scr, decf_scr, cumf_scr, keyf_scr, totf_scr),
        (True, zb_ref, lbb_ref, ob_scr, qhb_scr, updb_scr, decb_scr, cumb_scr, keyb_scr, totb_scr),
    )

    def gates(z_ref, lb_ref, rows):
        return _log2_decay_and_key(z_ref[0, rows, :], lb_ref[0, 0:1, :], lb_ref[0, 1:2, :], lb_ref[0, 2:3, :])

    def phase0(i, lowest):
        rows = rows_of(i)
        for reverse, z_ref, lb_ref, _, _, _, dec_scr, cum_scr, key_scr, tot_scr in dirs:
            a2, key = gates(z_ref, lb_ref, rows)
            cum = _cumsum_rows(a2, tri_ref[1 if reverse else 0])
            total = cum[0:1, :] if reverse else cum[c - 1:c, :]
            cum_scr[rows, :] = cum
            key_scr[rows, :] = key
            tot_scr[i] = jnp.broadcast_to(total, (SUBLANES, LANES))
            dec_scr[i] = jnp.broadcast_to(jnp.exp2(total), (SUBLANES, LANES))
            lowest = jnp.minimum(lowest, total)
        return lowest

    lowest = lax.fori_loop(0, n_chunks, phase0, jnp.zeros((1, LANES), F32), unroll=2)
    safe = jnp.min(lowest) > -SAFE_LOG2_RANGE

    @pl.when(safe)
    def _():
        ti = lax.broadcasted_iota(jnp.int32, (c, c), 0)
        si = lax.broadcasted_iota(jnp.int32, (c, c), 1)

        def body(i, carry):
            rows = rows_of(i)
            q = q_ref[0, rows, :]
            vb = v_ref[0, rows, :].astype(BF16)
            for reverse, _, _, o_scr, qh_scr, upd_scr, _, cum_scr, key_scr, tot_scr in dirs:
                cum = cum_scr[rows, :]
                half = tot_scr[i][0:1, :] * 0.5
                qt = q * jnp.exp2(cum - half)
                kt = key_scr[rows, :] * jnp.exp2(half - cum)
                s = lax.dot_general(qt.astype(BF16), kt.astype(BF16), NT_DIMS, preferred_element_type=F32)
                s = jnp.where((si >= ti) if reverse else (si <= ti), s, 0.0)
                o_scr[rows, :] = jnp.dot(s.astype(BF16), vb, preferred_element_type=F32)
                edge = jnp.exp2(half)
                qh_scr[rows, :] = (qt * edge).astype(BF16)
                upd_scr[i] = lax.dot_general(vb, (kt * edge).astype(BF16), TN_DIMS,
                                             preferred_element_type=F32)
            return carry

        lax.fori_loop(0, n_chunks, body, 0, unroll=2)

    @pl.when(jnp.logical_not(safe))
    def _():
        def body(i, carry):
            rows = rows_of(i)
            q = q_ref[0, rows, :]
            v = v_ref[0, rows, :]
            vb = v.astype(BF16)
            for reverse, z_ref, lb_ref, o_scr, qh_scr, upd_scr, _, _, _, _ in dirs:
                a2, key = gates(z_ref, lb_ref, rows)
                scores, cum = _chunk_scores(q, key, a2 * LN2, masks_ref, n_levels if reverse else 0, reverse)
                diag = jnp.sum(q * key, axis=-1, keepdims=True)
                o_scr[rows, :] = jnp.dot(scores.astype(BF16), vb, preferred_element_type=F32) + diag * v
                total = cum[0:1, :] if reverse else cum[c - 1:c, :]
                qh_scr[rows, :] = (q * jnp.exp(cum)).astype(BF16)
                kd = (key * jnp.exp(total - cum)).astype(BF16)
                upd_scr[i] = lax.dot_general(vb, kd, TN_DIMS, preferred_element_type=F32)
            return carry

        lax.fori_loop(0, n_chunks, body, 0)

    def phase2(i, carry):
        s_f, s_b = carry
        j = n_chunks - 1 - i
        stf_scr[i] = s_f.astype(BF16)
        stb_scr[j] = s_b.astype(BF16)
        s_f = s_f * decf_scr[i][0:1, :] + updf_scr[i]
        s_b = s_b * decb_scr[j][0:1, :] + updb_scr[j]
        return s_f, s_b

    zero = jnp.zeros((LANES, LANES), F32)
    lax.fori_loop(0, n_chunks, phase2, (zero, zero))

    def phase3(i, carry):
        rows = rows_of(i)
        o = of_scr[rows, :] + ob_scr[rows, :]
        o += lax.dot_general(qhf_scr[rows, :], stf_scr[i], NT_DIMS, preferred_element_type=F32)
        o += lax.dot_general(qhb_scr[rows, :], stb_scr[i], NT_DIMS, preferred_element_type=F32)
        o = _rmsnorm_rows(o, gain_ref[0])
        g = g_ref[0, rows, :]
        o_ref[0, rows, :] = (o * (g * jax.nn.sigmoid(g))).astype(o_ref.dtype)
        return carry

    lax.fori_loop(0, n_chunks, phase3, 0, unroll=4)


def _level_masks(chunk):
    idx = jnp.arange(chunk)
    t, s = idx[:, None], idx[None, :]
    fwd, bwd = [], []
    h = 1
    while h < chunk:
        same = (t // (2 * h)) == (s // (2 * h))
        t_up = (t & h) != 0
        s_up = (s & h) != 0
        fwd.append(same & t_up & ~s_up)
        bwd.append(same & ~t_up & s_up)
        h *= 2
    return jnp.stack(fwd + bwd).astype(F32)


def hgrn2_mixer(p, lb_f, lb_b, out_gain, *, heads, chunk=HGRN_CHUNK):
    b, s, _ = p.shape
    n_chunks = s // chunk
    n_levels = int(math.log2(chunk))

    def lb_rows(lb):
        lb = lb.astype(F32).reshape(heads, 1, LANES)
        rows = jnp.concatenate([lb, jnp.log1p(-lb), 1.0 - lb], axis=1)
        return jnp.pad(rows, ((0, 0), (0, SUBLANES - 3), (0, 0)))

    def col(block0):
        return pl.BlockSpec((1, s, LANES), lambda bi, hi: (bi, 0, block0 + hi))

    per_head = pl.BlockSpec((1, SUBLANES, LANES), lambda bi, hi: (hi, 0, 0))
    masks = _level_masks(chunk)
    idx = jnp.arange(chunk)
    lower = idx[None, :] <= idx[:, None]
    tri = jnp.stack([lower, lower.T]).astype(BF16)
    kern = functools.partial(_hgrn_kernel, chunk=chunk, n_chunks=n_chunks, n_levels=n_levels)
    seq_f32 = pltpu.VMEM((s, LANES), F32)
    seq_bf16 = pltpu.VMEM((s, LANES), BF16)
    chunk_mat_f32 = pltpu.VMEM((n_chunks, LANES, LANES), F32)
    chunk_mat_bf16 = pltpu.VMEM((n_chunks, LANES, LANES), BF16)
    chunk_row = pltpu.VMEM((n_chunks, SUBLANES, LANES), F32)
    return pl.pallas_call(
        kern,
        grid=(b, heads),
        in_specs=[
            col(COL_RQ // LANES), col(COL_RI // LANES), col(COL_ZF // LANES), col(COL_ZB // LANES),
            col(COL_RG // LANES), per_head, per_head,
            pl.BlockSpec((1, 1, LANES), lambda bi, hi: (hi, 0, 0)),
            pl.BlockSpec(masks.shape, lambda bi, hi: (0, 0, 0)),
            pl.BlockSpec(tri.shape, lambda bi, hi: (0, 0, 0)),
        ],
        out_specs=pl.BlockSpec((1, s, LANES), lambda bi, hi: (bi, 0, hi)),
        out_shape=jax.ShapeDtypeStruct((b, s, heads * LANES), BF16),
        scratch_shapes=[
            seq_f32, seq_f32, seq_bf16, seq_bf16,
            chunk_mat_f32, chunk_mat_f32, chunk_row, chunk_row,
            chunk_mat_bf16, chunk_mat_bf16,
            seq_f32, seq_f32, seq_f32, seq_f32, chunk_row, chunk_row,
        ],
        compiler_params=_params("parallel", "parallel"),
        name="hgrn2",
    )(p, p, p, p, p, lb_rows(lb_f), lb_rows(lb_b), out_gain.astype(F32).reshape(heads, 1, LANES), masks, tri)


def _attn_kernel(sink_ref, q_ref, kp_ref, kc_ref, kn_ref, vp_ref, vc_ref, vn_ref, bias_ref, o_ref):
    qscale = LOG2E / math.sqrt(ATT_HD)
    qb = Q_BLOCK

    def head_cols(ref, head):
        return ref[0, :, head * ATT_HD:(head + 1) * ATT_HD]

    def band(p_ref, c_ref, n_ref, kvh):
        return jnp.concatenate([head_cols(p_ref, kvh), head_cols(c_ref, kvh), head_cols(n_ref, kvh)],
                               axis=0).astype(BF16)

    for kvh in range(ATT_KV_HEADS):
        k = band(kp_ref, kc_ref, kn_ref, kvh)
        v1 = jnp.concatenate([band(vp_ref, vc_ref, vn_ref, kvh), jnp.ones((3 * qb, ATT_HD), BF16)], axis=1)
        for g in range(ATT_GROUP):
            head = kvh * ATT_GROUP + g
            q = (head_cols(q_ref, head) * qscale).astype(BF16)
            s = lax.dot_general(q, k, NT_DIMS, preferred_element_type=F32) + bias_ref[0, head * qb:(head + 1) * qb]
            sink = sink_ref[head] * LOG2E
            m = jnp.broadcast_to(jnp.maximum(jnp.max(s, axis=-1, keepdims=True), sink), (qb, ATT_HD))
            e = jnp.exp2(s - jnp.concatenate([m, m, m], axis=1)).astype(BF16)
            ov = jnp.dot(e, v1, preferred_element_type=F32)
            o = ov[:, :ATT_HD] / (ov[:, ATT_HD:] + jnp.exp2(sink - m))
            o_ref[0, :, head * ATT_HD:(head + 1) * ATT_HD] = o.astype(o_ref.dtype)


def _attn_bias():
    qi = jnp.arange(Q_BLOCK)[:, None]
    kj = jnp.arange(3 * Q_BLOCK)[None, :]
    dist = jnp.abs(kj - Q_BLOCK - qi)
    slopes = 2.0 ** (-8.0 * jnp.arange(1, ATT_HEADS + 1, dtype=F32) / ATT_HEADS)
    alibi = -(slopes[:, None, None] * dist.astype(F32)[None]) * LOG2E
    cases = []
    for has_prev, has_next in ((False, True), (True, True), (True, False)):
        valid = (dist <= WINDOW) & ((kj >= Q_BLOCK) | has_prev) & ((kj < 2 * Q_BLOCK) | has_next)
        cases.append(jnp.where(valid[None], alibi, -jnp.inf).reshape(ATT_HEADS * Q_BLOCK, 3 * Q_BLOCK))
    return jnp.stack(cases)


def window_gqa(p, sink):
    b, s, _ = p.shape
    nb = s // Q_BLOCK
    assert nb >= 2
    kcol, vcol = COL_AK // KV_WIDTH, COL_AV // KV_WIDTH

    def kv(col, off):
        return pl.BlockSpec((1, Q_BLOCK, KV_WIDTH),
                            lambda bi, n, sk: (bi, jnp.clip(n + off, 0, nb - 1), col))

    def bias_case(bi, n, sk):
        return (jnp.where(n == 0, 0, jnp.where(n == nb - 1, 2, 1)), 0, 0)

    return pl.pallas_call(
        _attn_kernel,
        grid_spec=pltpu.PrefetchScalarGridSpec(
            num_scalar_prefetch=1,
            grid=(b, nb),
            in_specs=[
                pl.BlockSpec((1, Q_BLOCK, ATT_WIDTH), lambda bi, n, sk: (bi, n, COL_AQ // ATT_WIDTH)),
                kv(kcol, -1), kv(kcol, 0), kv(kcol, 1),
                kv(vcol, -1), kv(vcol, 0), kv(vcol, 1),
                pl.BlockSpec((1, ATT_HEADS * Q_BLOCK, 3 * Q_BLOCK), bias_case),
            ],
            out_specs=pl.BlockSpec((1, Q_BLOCK, ATT_WIDTH), lambda bi, n, sk: (bi, n, 0)),
        ),
        out_shape=jax.ShapeDtypeStruct((b, s, ATT_WIDTH), BF16),
        compiler_params=_params("parallel", "arbitrary"),
        name="window_gqa",
    )(sink.astype(F32), p, p, p, p, p, p, p, _attn_bias())


def _merge_kernel(oa_ref, ob_ref, wa_ref, wb_ref, ga_ref, gb_ref, m_ref, wa_scr, wb_scr, *, tn):
    @pl.when((pl.program_id(0) == 0) & (pl.program_id(1) == 0))
    def _():
        wa_scr[...] = wa_ref[...].astype(BF16)
        wb_scr[...] = wb_ref[...].astype(BF16)

    cols = pl.ds(pl.multiple_of(pl.program_id(1) * tn, tn), tn)
    ya = jnp.dot(oa_ref[...], wa_scr[:, cols], preferred_element_type=F32)
    yb = jnp.dot(ob_ref[...], wb_scr[:, cols], preferred_element_type=F32)
    m_ref[...] = (jax.nn.sigmoid(ga_ref[...]) * ya + jax.nn.sigmoid(gb_ref[...]) * yb).astype(m_ref.dtype)


def merge_branches(oa, ob, wa, wb, p, l, *, tm=TM_MERGE, tn=TN_MERGE):
    t, ka = oa.shape
    kb = ob.shape[1]
    n = wa.shape[2]
    assert COL_GA % tn == 0 and COL_GB % tn == 0
    return pl.pallas_call(
        functools.partial(_merge_kernel, tn=tn),
        grid=(t // tm, n // tn),
        in_specs=[
            pl.BlockSpec((tm, ka), lambda i, j: (i, 0)),
            pl.BlockSpec((tm, kb), lambda i, j: (i, 0)),
            _once((None, ka, n), lambda i, j: (l, 0, 0)),
            _once((None, kb, n), lambda i, j: (l, 0, 0)),
            pl.BlockSpec((tm, tn), lambda i, j: (i, COL_GA // tn + j)),
            pl.BlockSpec((tm, tn), lambda i, j: (i, COL_GB // tn + j)),
        ],
        out_specs=pl.BlockSpec((tm, tn), lambda i, j: (i, j)),
        out_shape=jax.ShapeDtypeStruct((t, n), BF16),
        scratch_shapes=[pltpu.VMEM((ka, n), BF16), pltpu.VMEM((kb, n), BF16)],
        compiler_params=_params("arbitrary", "arbitrary"),
        name="merge_branches",
    )(oa, ob, wa, wb, p, p)


def _final_norm_kernel(x_ref, gain_ref, o_ref):
    o_ref[...] = _rmsnorm_rows(x_ref[...], gain_ref[...])


def final_rmsnorm(x, gain, *, tm=TM_NORM):
    t, d = x.shape
    return pl.pallas_call(
        _final_norm_kernel,
        grid=(t // tm,),
        in_specs=[pl.BlockSpec((tm, d), lambda i: (i, 0)), pl.BlockSpec((1, d), lambda i: (0, 0))],
        out_specs=pl.BlockSpec((tm, d), lambda i: (i, 0)),
        out_shape=jax.ShapeDtypeStruct((t, d), F32),
        compiler_params=_params("parallel"),
        name="final_norm",
    )(x, gain.reshape(1, d))


def _lower_bounds(lb_logits):
    lb = jnp.cumsum(jax.nn.softmax(lb_logits.astype(F32), axis=0), axis=0)
    return lb - lb[0:1]


def kernel(x, ffn1_norm, ffn1_w_gate, ffn1_w_up, ffn1_w_down, mix_norm, w_in, lb_fwd_logits, lb_bwd_logits, rg_out_norm, attn_sink, w_branch_a, w_branch_b, w_out, ffn2_norm, ffn2_w_gate, ffn2_w_up, ffn2_w_down, final_norm):
    b, s, d = x.shape
    t = b * s
    depth = w_in.shape[0]
    lb_f_all = _lower_bounds(lb_fwd_logits)
    lb_b_all = _lower_bounds(lb_bwd_logits)
    xf = x.reshape(t, d)

    def ffn(xf, gain, wg, wu, wd, l):
        a = ffn_up(xf, gain, wg, wu, l)
        return resid_mm(a, wd, xf, l, scale=0.5, tm=TM_DOWN, tn=TN_DOWN)

    for l in range(depth):
        xf = ffn(xf, ffn1_norm, ffn1_w_gate, ffn1_w_up, ffn1_w_down, l)
        p = in_proj(xf, mix_norm, w_in, l)
        p3 = p.reshape(b, s, IN_COLS)
        o_a = hgrn2_mixer(p3, lb_f_all[l], lb_b_all[l], rg_out_norm[l], heads=RG_HEADS)
        o_b = window_gqa(p3, attn_sink[l])
        merged = merge_branches(o_a.reshape(t, RG_WIDTH), o_b.reshape(t, ATT_WIDTH),
                                w_branch_a, w_branch_b, p, l)
        xf = resid_mm_resident(merged, w_out, xf, l, tm=TM_OUT, tn=TN_OUT)
        xf = ffn(xf, ffn2_norm, ffn2_w_gate, ffn2_w_up, ffn2_w_down, l)
    return final_rmsnorm(xf, final_norm).reshape(b, s, d)
```

```python
import functools
import math

import jax
import jax.numpy as jnp
from jax import lax
from jax.experimental import pallas as pl
from jax.experimental.pallas import tpu as pltpu

D_MODEL = 2048
D_FF = 5632
RG_HEADS = 8
RG_DK = 128
RG_DV = 128
RG_WIDTH = RG_HEADS * RG_DV
ATT_HEADS = 8
ATT_KV_HEADS = 2
ATT_GROUP = ATT_HEADS // ATT_KV_HEADS
ATT_HD = 128
ATT_WIDTH = ATT_HEADS * ATT_HD
KV_WIDTH = ATT_KV_HEADS * ATT_HD
WINDOW = 128
Q_BLOCK = 128
IN_COLS = 5 * RG_WIDTH + ATT_WIDTH + 2 * KV_WIDTH + 2 * D_MODEL
EPS = 1e-6

COL_RQ = 0
COL_RI = RG_WIDTH
COL_ZF = 2 * RG_WIDTH
COL_ZB = 3 * RG_WIDTH
COL_RG = 4 * RG_WIDTH
COL_AQ = 5 * RG_WIDTH
COL_AK = COL_AQ + ATT_WIDTH
COL_AV = COL_AK + KV_WIDTH
COL_GA = COL_AV + KV_WIDTH
COL_GB = COL_GA + D_MODEL

LANES = 128
SUBLANES = 8
HGRN_CHUNK = 128
UNROLL_GATES = 8
VMEM_LIMIT_BYTES = 56 * 1024 * 1024
LOG2E = 1.4426950408889634
LN2 = 0.6931471805599453
SAFE_LOG2_RANGE = 240.0

TM = 2048
TM_DOWN = 1024
TM_MERGE = 1024
TM_OUT = 1024
TN_FFN_UP = 512
TN_IN_PROJ = 512
TN_MERGE = 512
TN_DOWN = 256
TN_OUT = 512
TM_NORM = 512
NORM_STRIP_ROWS = 256
MXU_COLS = 256

BF16 = jnp.bfloat16
F32 = jnp.float32
NT_DIMS = (((1,), (1,)), ((), ()))
TN_DIMS = (((0,), (0,)), ((), ()))


def _params(*sem):
    return pltpu.CompilerParams(dimension_semantics=sem, vmem_limit_bytes=VMEM_LIMIT_BYTES)


def _rmsnorm_rows(x, gain):
    ms = jnp.mean(x * x, axis=-1, keepdims=True)
    return x * lax.rsqrt(ms + EPS) * gain


def _rmsnorm_tile_to_bf16(x_ref, gain_ref, h_ref):
    def strip(r, carry):
        rows = pl.ds(pl.multiple_of(r * NORM_STRIP_ROWS, NORM_STRIP_ROWS), NORM_STRIP_ROWS)
        h_ref[rows, :] = _rmsnorm_rows(x_ref[rows, :].astype(F32), gain_ref[...]).astype(BF16)
        return carry

    lax.fori_loop(0, x_ref.shape[0] // NORM_STRIP_ROWS, strip, 0)


def _once(block_shape, index_map):
    return pl.BlockSpec(block_shape, index_map, pipeline_mode=pl.Buffered(1))


def _token_rows_spec(x, tm, room_for_two):
    spec = pl.BlockSpec if room_for_two else _once
    return spec((tm, x.shape[1]), lambda i, j: (i, 0))


def _ffn_up_kernel(x_ref, gain_ref, wg_ref, wu_ref, a_ref, h_scr):
    @pl.when(pl.program_id(1) == 0)
    def _():
        _rmsnorm_tile_to_bf16(x_ref, gain_ref, h_scr)

    h = h_scr[...]
    for c0 in range(0, a_ref.shape[1], MXU_COLS):
        cols = slice(c0, c0 + MXU_COLS)
        g = jnp.dot(h, wg_ref[:, cols].astype(BF16), preferred_element_type=F32)
        u = jnp.dot(h, wu_ref[:, cols].astype(BF16), preferred_element_type=F32)
        a_ref[:, cols] = (g * jax.nn.sigmoid(g) * u).astype(BF16)


def ffn_up(x, gain, wg, wu, l, *, tm=TM, tn=TN_FFN_UP):
    t, d = x.shape
    f = wg.shape[2]
    return pl.pallas_call(
        _ffn_up_kernel,
        grid=(t // tm, f // tn),
        in_specs=[
            _token_rows_spec(x, tm, room_for_two=False),
            pl.BlockSpec((None, 1, d), lambda i, j: (l, 0, 0)),
            pl.BlockSpec((None, d, tn), lambda i, j: (l, 0, j)),
            pl.BlockSpec((None, d, tn), lambda i, j: (l, 0, j)),
        ],
        out_specs=pl.BlockSpec((tm, tn), lambda i, j: (i, j)),
        out_shape=jax.ShapeDtypeStruct((t, f), BF16),
        scratch_shapes=[pltpu.VMEM((tm, d), BF16)],
        compiler_params=_params("parallel", "arbitrary"),
        name="ffn_up",
    )(x, gain.reshape(gain.shape[0], 1, d), wg, wu)


def _store_residual(out, o_ref, copy_refs):
    o_ref[...] = out
    for ref in copy_refs:
        ref[...] = out.astype(ref.dtype)


def _residual_outputs(t, n, tm, tn, bf16_copy):
    spec = pl.BlockSpec((tm, tn), lambda i, j: (i, j))
    shape = jax.ShapeDtypeStruct((t, n), F32)
    if not bf16_copy:
        return spec, shape
    return [spec, spec], [shape, jax.ShapeDtypeStruct((t, n), BF16)]


def _resid_mm_kernel(a_ref, w_ref, x_ref, o_ref, *copy_refs, scale):
    y = jnp.dot(a_ref[...], w_ref[...].astype(BF16), preferred_element_type=F32)
    _store_residual(x_ref[...] + scale * y, o_ref, copy_refs)


def resid_mm(a, w, x, l, *, scale, tm, tn, bf16_copy):
    t, kdim = a.shape
    n = w.shape[2]
    out_specs, out_shape = _residual_outputs(t, n, tm, tn, bf16_copy)
    return pl.pallas_call(
        functools.partial(_resid_mm_kernel, scale=scale),
        grid=(t // tm, n // tn),
        in_specs=[
            pl.BlockSpec((tm, kdim), lambda i, j: (i, 0)),
            pl.BlockSpec((None, kdim, tn), lambda i, j: (l, 0, j)),
            pl.BlockSpec((tm, tn), lambda i, j: (i, j)),
        ],
        out_specs=out_specs,
        out_shape=out_shape,
        compiler_params=_params("parallel", "arbitrary"),
        name="resid_mm",
    )(a, w, x)


def _resid_mm_resident_kernel(a_ref, w_ref, x_ref, o_ref, *rest, tn):
    *copy_refs, w_scr = rest

    @pl.when((pl.program_id(0) == 0) & (pl.program_id(1) == 0))
    def _():
        w_scr[...] = w_ref[...].astype(BF16)

    cols = pl.ds(pl.multiple_of(pl.program_id(1) * tn, tn), tn)
    y = jnp.dot(a_ref[...], w_scr[:, cols], preferred_element_type=F32)
    _store_residual(x_ref[...] + y, o_ref, copy_refs)


def resid_mm_resident(a, w, x, l, *, tm, tn, bf16_copy):
    t, kdim = a.shape
    n = w.shape[2]
    out_specs, out_shape = _residual_outputs(t, n, tm, tn, bf16_copy)
    return pl.pallas_call(
        functools.partial(_resid_mm_resident_kernel, tn=tn),
        grid=(t // tm, n // tn),
        in_specs=[
            pl.BlockSpec((tm, kdim), lambda i, j: (i, 0)),
            _once((None, kdim, n), lambda i, j: (l, 0, 0)),
            pl.BlockSpec((tm, tn), lambda i, j: (i, j)),
        ],
        out_specs=out_specs,
        out_shape=out_shape,
        scratch_shapes=[pltpu.VMEM((kdim, n), BF16)],
        compiler_params=_params("arbitrary", "arbitrary"),
        name="resid_mm_resident",
    )(a, w, x)


def _in_proj_kernel(x_ref, gain_ref, w_ref, p_ref, h_scr):
    @pl.when(pl.program_id(1) == 0)
    def _():
        _rmsnorm_tile_to_bf16(x_ref, gain_ref, h_scr)

    p_ref[...] = jnp.dot(h_scr[...], w_ref[...].astype(BF16), preferred_element_type=F32)


def in_proj(x, gain, w, l, *, tm=TM, tn=TN_IN_PROJ):
    t, d = x.shape
    n = w.shape[2]
    return pl.pallas_call(
        _in_proj_kernel,
        grid=(t // tm, n // tn),
        in_specs=[
            _token_rows_spec(x, tm, room_for_two=x.dtype == BF16),
            pl.BlockSpec((None, 1, d), lambda i, j: (l, 0, 0)),
            pl.BlockSpec((None, d, tn), lambda i, j: (l, 0, j)),
        ],
        out_specs=pl.BlockSpec((tm, tn), lambda i, j: (i, j)),
        out_shape=jax.ShapeDtypeStruct((t, n), F32),
        scratch_shapes=[pltpu.VMEM((tm, d), BF16)],
        compiler_params=_params("parallel", "arbitrary"),
        name="in_proj",
    )(x, gain.reshape(gain.shape[0], 1, d), w)


def _block_ref_rows(p, h, r):
    c, n = p.shape
    blk = 2 * h
    if blk >= SUBLANES:
        p3 = p.reshape(c // blk, blk, n)
        return jnp.broadcast_to(p3[:, r:r + 1, :], p3.shape).reshape(c, n)
    pos = lax.broadcasted_iota(jnp.int32, p.shape, 0) & (blk - 1)
    out = p
    for src in range(blk):
        if src == r:
            continue
        shifted = pltpu.roll(p, (src - r) % c, axis=0)
        out = jnp.where(pos == src, shifted, out)
    return out


def _chunk_scores(q, key, a, masks_ref, mask_base, reverse):
    c, n = q.shape
    row = lax.broadcasted_iota(jnp.int32, q.shape, 0)
    p = a
    scores = None
    h = 1
    level = 0
    while h < c:
        if h >= SUBLANES:
            nb = c // (2 * h)
            p4 = p.reshape(nb, 2, h, n)
            q4 = q.reshape(nb, 2, h, n)
            k4 = key.reshape(nb, 2, h, n)
            lo, hi = p4[:, 0], p4[:, 1]
            if reverse:
                t_row = hi[:, 0:1, :]
                qk = jnp.stack([q4[:, 0] * jnp.exp(lo), k4[:, 1] * jnp.exp(t_row - hi)], axis=1)
                p = jnp.stack([lo + t_row, hi], axis=1).reshape(c, n)
            else:
                t_row = lo[:, h - 1:h, :]
                qk = jnp.stack([k4[:, 0] * jnp.exp(t_row - lo), q4[:, 1] * jnp.exp(hi)], axis=1)
                p = jnp.stack([lo, hi + t_row], axis=1).reshape(c, n)
            qk = qk.reshape(c, n).astype(BF16)
        else:
            upper = (row & h) != 0
            qside = jnp.logical_not(upper) if reverse else upper
            t_rows = _block_ref_rows(p, h, h if reverse else h - 1)
            e = jnp.where(qside, p, t_rows - p)
            qk = (jnp.where(qside, q, key) * jnp.exp(e)).astype(BF16)
            p = p + jnp.where(qside, t_rows, 0.0)
        s = lax.dot_general(qk, qk, NT_DIMS, preferred_element_type=F32)
        s = s * masks_ref[mask_base + level]
        scores = s if scores is None else scores + s
        h *= 2
        level += 1
    return scores, p


def _log2_decay_and_key(z, lb, log1m_lb, one_m_lb):
    u = jnp.exp(-jnp.abs(z))
    one_u = 1.0 + u
    r = 1.0 / one_u
    pos = z >= 0.0
    sig = jnp.where(pos, 1.0, u) * r
    key = one_m_lb * (jnp.where(pos, u, 1.0) * r)
    y2 = (log1m_lb + jnp.minimum(z, 0.0)) * LOG2E - jnp.log2(one_u)
    return jnp.maximum(jnp.log2(lb + one_m_lb * sig), y2), key


def _cumsum_rows(a, tri_bf16):
    a1 = a.astype(BF16)
    r1 = a - a1.astype(F32)
    a2 = r1.astype(BF16)
    a3 = (r1 - a2.astype(F32)).astype(BF16)
    out = jnp.dot(tri_bf16, jnp.concatenate([a1, a2, a3], axis=1), preferred_element_type=F32)
    n = a.shape[1]
    return (out[:, 2 * n:] + out[:, n:2 * n]) + out[:, :n]


def _hgrn_kernel(q_ref, v_ref, zf_ref, zb_ref, g_ref, lbf_ref, lbb_ref, gain_ref, masks_ref, tri_ref, o_ref,
                 of_scr, ob_scr, qhf_scr, qhb_scr, updf_scr, updb_scr, decf_scr, decb_scr, stf_scr, stb_scr,
                 cumf_scr, cumb_scr, keyf_scr, keyb_scr, totf_scr, totb_scr,
                 *, chunk, n_chunks, n_levels):
    c = chunk

    def rows_of(i):
        return pl.ds(pl.multiple_of(i * c, c), c)

    dirs = (
        (False, zf_ref, lbf_ref, of_scr, qhf_scr, updf_scr, decf_scr, cumf_scr, keyf_scr, totf_scr),
        (True, zb_ref, lbb_ref, ob_scr, qhb_scr, updb_scr, decb_scr, cumb_scr, keyb_scr, totb_scr),
    )

    def gates(z_ref, lb_ref, rows):
        return _log2_decay_and_key(z_ref[0, rows, :], lb_ref[0, 0:1, :], lb_ref[0, 1:2, :], lb_ref[0, 2:3, :])

    def phase0(i, lowest):
        rows = rows_of(i)
        for reverse, z_ref, lb_ref, _, _, _, dec_scr, cum_scr, key_scr, tot_scr in dirs:
            a2, key = gates(z_ref, lb_ref, rows)
            cum = _cumsum_rows(a2, tri_ref[1 if reverse else 0])
            total = cum[0:1, :] if reverse else cum[c - 1:c, :]
            cum_scr[rows, :] = cum
            key_scr[rows, :] = key
            tot_scr[i] = jnp.broadcast_to(total, (SUBLANES, LANES))
            dec_scr[i] = jnp.broadcast_to(jnp.exp2(total), (SUBLANES, LANES))
            lowest = jnp.minimum(lowest, total)
        return lowest

    lowest = lax.fori_loop(0, n_chunks, phase0, jnp.zeros((1, LANES), F32), unroll=UNROLL_GATES)
    safe = jnp.min(lowest) > -SAFE_LOG2_RANGE

    @pl.when(safe)
    def _():
        ti = lax.broadcasted_iota(jnp.int32, (c, c), 0)
        si = lax.broadcasted_iota(jnp.int32, (c, c), 1)

        def body(i, carry):
            rows = rows_of(i)
            q = q_ref[0, rows, :]
            vb = v_ref[0, rows, :].astype(BF16)
            for reverse, _, _, o_scr, qh_scr, upd_scr, _, cum_scr, key_scr, tot_scr in dirs:
                cum = cum_scr[rows, :]
                half = tot_scr[i][0:1, :] * 0.5
                qt = q * jnp.exp2(cum - half)
                kt = key_scr[rows, :] * jnp.exp2(half - cum)
                s = lax.dot_general(qt.astype(BF16), kt.astype(BF16), NT_DIMS, preferred_element_type=F32)
                s = jnp.where((si >= ti) if reverse else (si <= ti), s, 0.0)
                o_scr[rows, :] = jnp.dot(s.astype(BF16), vb, preferred_element_type=F32)
                edge = jnp.exp2(half)
                qh_scr[rows, :] = (qt * edge).astype(BF16)
                upd_scr[i] = lax.dot_general(vb, (kt * edge).astype(BF16), TN_DIMS,
                                             preferred_element_type=F32)
            return carry

        lax.fori_loop(0, n_chunks, body, 0, unroll=True)

    @pl.when(jnp.logical_not(safe))
    def _():
        def body(i, carry):
            rows = rows_of(i)
            q = q_ref[0, rows, :]
            v = v_ref[0, rows, :]
            vb = v.astype(BF16)
            for reverse, z_ref, lb_ref, o_scr, qh_scr, upd_scr, _, _, _, _ in dirs:
                a2, key = gates(z_ref, lb_ref, rows)
                scores, cum = _chunk_scores(q, key, a2 * LN2, masks_ref, n_levels if reverse else 0, reverse)
                diag = jnp.sum(q * key, axis=-1, keepdims=True)
                o_scr[rows, :] = jnp.dot(scores.astype(BF16), vb, preferred_element_type=F32) + diag * v
                total = cum[0:1, :] if reverse else cum[c - 1:c, :]
                qh_scr[rows, :] = (q * jnp.exp(cum)).astype(BF16)
                kd = (key * jnp.exp(total - cum)).astype(BF16)
                upd_scr[i] = lax.dot_general(vb, kd, TN_DIMS, preferred_element_type=F32)
            return carry

        lax.fori_loop(0, n_chunks, body, 0)

    def phase2(i, carry):
        s_f, s_b = carry
        j = n_chunks - 1 - i
        stf_scr[i] = s_f.astype(BF16)
        stb_scr[j] = s_b.astype(BF16)
        s_f = s_f * decf_scr[i][0:1, :] + updf_scr[i]
        s_b = s_b * decb_scr[j][0:1, :] + updb_scr[j]
        return s_f, s_b

    zero = jnp.zeros((LANES, LANES), F32)
    lax.fori_loop(0, n_chunks, phase2, (zero, zero))

    def phase3(i, carry):
        rows = rows_of(i)
        o = of_scr[rows, :] + ob_scr[rows, :]
        o += lax.dot_general(qhf_scr[rows, :], stf_scr[i], NT_DIMS, preferred_element_type=F32)
        o += lax.dot_general(qhb_scr[rows, :], stb_scr[i], NT_DIMS, preferred_element_type=F32)
        o = _rmsnorm_rows(o, gain_ref[0])
        g = g_ref[0, rows, :]
        o_ref[0, rows, :] = (o * (g * jax.nn.sigmoid(g))).astype(o_ref.dtype)
        return carry

    lax.fori_loop(0, n_chunks, phase3, 0, unroll=True)


def _level_masks(chunk):
    idx = jnp.arange(chunk)
    t, s = idx[:, None], idx[None, :]
    fwd, bwd = [], []
    h = 1
    while h < chunk:
        same = (t // (2 * h)) == (s // (2 * h))
        t_up = (t & h) != 0
        s_up = (s & h) != 0
        fwd.append(same & t_up & ~s_up)
        bwd.append(same & ~t_up & s_up)
        h *= 2
    return jnp.stack(fwd + bwd).astype(F32)


def hgrn2_mixer(p, lb_f, lb_b, out_gain, *, heads, chunk=HGRN_CHUNK):
    b, s, _ = p.shape
    n_chunks = s // chunk
    n_levels = int(math.log2(chunk))

    def lb_rows(lb):
        lb = lb.astype(F32).reshape(heads, 1, LANES)
        rows = jnp.concatenate([lb, jnp.log1p(-lb), 1.0 - lb], axis=1)
        return jnp.pad(rows, ((0, 0), (0, SUBLANES - 3), (0, 0)))

    def col(block0):
        return pl.BlockSpec((1, s, LANES), lambda bi, hi: (bi, 0, block0 + hi))

    per_head = pl.BlockSpec((1, SUBLANES, LANES), lambda bi, hi: (hi, 0, 0))
    masks = _level_masks(chunk)
    idx = jnp.arange(chunk)
    lower = idx[None, :] <= idx[:, None]
    tri = jnp.stack([lower, lower.T]).astype(BF16)
    kern = functools.partial(_hgrn_kernel, chunk=chunk, n_chunks=n_chunks, n_levels=n_levels)
    seq_f32 = pltpu.VMEM((s, LANES), F32)
    seq_bf16 = pltpu.VMEM((s, LANES), BF16)
    chunk_mat_f32 = pltpu.VMEM((n_chunks, LANES, LANES), F32)
    chunk_mat_bf16 = pltpu.VMEM((n_chunks, LANES, LANES), BF16)
    chunk_row = pltpu.VMEM((n_chunks, SUBLANES, LANES), F32)
    return pl.pallas_call(
        kern,
        grid=(b, heads),
        in_specs=[
            col(COL_RQ // LANES), col(COL_RI // LANES), col(COL_ZF // LANES), col(COL_ZB // LANES),
            col(COL_RG // LANES), per_head, per_head,
            pl.BlockSpec((1, 1, LANES), lambda bi, hi: (hi, 0, 0)),
            pl.BlockSpec(masks.shape, lambda bi, hi: (0, 0, 0)),
            pl.BlockSpec(tri.shape, lambda bi, hi: (0, 0, 0)),
        ],
        out_specs=pl.BlockSpec((1, s, LANES), lambda bi, hi: (bi, 0, hi)),
        out_shape=jax.ShapeDtypeStruct((b, s, heads * LANES), BF16),
        scratch_shapes=[
            seq_f32, seq_f32, seq_bf16, seq_bf16,
            chunk_mat_f32, chunk_mat_f32, chunk_row, chunk_row,
            chunk_mat_bf16, chunk_mat_bf16,
            seq_f32, seq_f32, seq_f32, seq_f32, chunk_row, chunk_row,
        ],
        compiler_params=_params("parallel", "parallel"),
        name="hgrn2",
    )(p, p, p, p, p, lb_rows(lb_f), lb_rows(lb_b), out_gain.astype(F32).reshape(heads, 1, LANES), masks, tri)


def _attn_kernel(sink_ref, q_ref, kp_ref, kc_ref, kn_ref, vp_ref, vc_ref, vn_ref, bias_ref, o_ref):
    qscale = LOG2E / math.sqrt(ATT_HD)
    qb = Q_BLOCK

    def head_cols(ref, head):
        return ref[0, :, head * ATT_HD:(head + 1) * ATT_HD]

    def band(p_ref, c_ref, n_ref, kvh):
        return jnp.concatenate([head_cols(p_ref, kvh), head_cols(c_ref, kvh), head_cols(n_ref, kvh)],
                               axis=0).astype(BF16)

    for kvh in range(ATT_KV_HEADS):
        k = band(kp_ref, kc_ref, kn_ref, kvh)
        v1 = jnp.concatenate([band(vp_ref, vc_ref, vn_ref, kvh), jnp.ones((3 * qb, ATT_HD), BF16)], axis=1)
        for g in range(ATT_GROUP):
            head = kvh * ATT_GROUP + g
            q = (head_cols(q_ref, head) * qscale).astype(BF16)
            s = lax.dot_general(q, k, NT_DIMS, preferred_element_type=F32) + bias_ref[0, head * qb:(head + 1) * qb]
            sink = sink_ref[head] * LOG2E
            m = jnp.broadcast_to(jnp.maximum(jnp.max(s, axis=-1, keepdims=True), sink), (qb, ATT_HD))
            e = jnp.exp2(s - jnp.concatenate([m, m, m], axis=1)).astype(BF16)
            ov = jnp.dot(e, v1, preferred_element_type=F32)
            o = ov[:, :ATT_HD] / (ov[:, ATT_HD:] + jnp.exp2(sink - m))
            o_ref[0, :, head * ATT_HD:(head + 1) * ATT_HD] = o.astype(o_ref.dtype)


def _attn_bias():
    qi = jnp.arange(Q_BLOCK)[:, None]
    kj = jnp.arange(3 * Q_BLOCK)[None, :]
    dist = jnp.abs(kj - Q_BLOCK - qi)
    slopes = 2.0 ** (-8.0 * jnp.arange(1, ATT_HEADS + 1, dtype=F32) / ATT_HEADS)
    alibi = -(slopes[:, None, None] * dist.astype(F32)[None]) * LOG2E
    cases = []
    for has_prev, has_next in ((False, True), (True, True), (True, False)):
        valid = (dist <= WINDOW) & ((kj >= Q_BLOCK) | has_prev) & ((kj < 2 * Q_BLOCK) | has_next)
        cases.append(jnp.where(valid[None], alibi, -jnp.inf).reshape(ATT_HEADS * Q_BLOCK, 3 * Q_BLOCK))
    return jnp.stack(cases)


def window_gqa(p, sink):
    b, s, _ = p.shape
    nb = s // Q_BLOCK
    assert nb >= 2
    kcol, vcol = COL_AK // KV_WIDTH, COL_AV // KV_WIDTH

    def kv(col, off):
        return pl.BlockSpec((1, Q_BLOCK, KV_WIDTH),
                            lambda bi, n, sk: (bi, jnp.clip(n + off, 0, nb - 1), col))

    def bias_case(bi, n, sk):
        return (jnp.where(n == 0, 0, jnp.where(n == nb - 1, 2, 1)), 0, 0)

    return pl.pallas_call(
        _attn_kernel,
        grid_spec=pltpu.PrefetchScalarGridSpec(
            num_scalar_prefetch=1,
            grid=(b, nb),
            in_specs=[
                pl.BlockSpec((1, Q_BLOCK, ATT_WIDTH), lambda bi, n, sk: (bi, n, COL_AQ // ATT_WIDTH)),
                kv(kcol, -1), kv(kcol, 0), kv(kcol, 1),
                kv(vcol, -1), kv(vcol, 0), kv(vcol, 1),
                pl.BlockSpec((1, ATT_HEADS * Q_BLOCK, 3 * Q_BLOCK), bias_case),
            ],
            out_specs=pl.BlockSpec((1, Q_BLOCK, ATT_WIDTH), lambda bi, n, sk: (bi, n, 0)),
        ),
        out_shape=jax.ShapeDtypeStruct((b, s, ATT_WIDTH), BF16),
        compiler_params=_params("parallel", "arbitrary"),
        name="window_gqa",
    )(sink.astype(F32), p, p, p, p, p, p, p, _attn_bias())


def _merge_kernel(oa_ref, ob_ref, wa_ref, wb_ref, ga_ref, gb_ref, m_ref, wa_scr, wb_scr, *, tn):
    @pl.when((pl.program_id(0) == 0) & (pl.program_id(1) == 0))
    def _():
        wa_scr[...] = wa_ref[...].astype(BF16)
        wb_scr[...] = wb_ref[...].astype(BF16)

    cols = pl.ds(pl.multiple_of(pl.program_id(1) * tn, tn), tn)
    ya = jnp.dot(oa_ref[...], wa_scr[:, cols], preferred_element_type=F32)
    yb = jnp.dot(ob_ref[...], wb_scr[:, cols], preferred_element_type=F32)
    m_ref[...] = (jax.nn.sigmoid(ga_ref[...]) * ya + jax.nn.sigmoid(gb_ref[...]) * yb).astype(m_ref.dtype)


def merge_branches(oa, ob, wa, wb, p, l, *, tm=TM_MERGE, tn=TN_MERGE):
    t, ka = oa.shape
    kb = ob.shape[1]
    n = wa.shape[2]
    assert COL_GA % tn == 0 and COL_GB % tn == 0
    return pl.pallas_call(
        functools.partial(_merge_kernel, tn=tn),
        grid=(t // tm, n // tn),
        in_specs=[
            pl.BlockSpec((tm, ka), lambda i, j: (i, 0)),
            pl.BlockSpec((tm, kb), lambda i, j: (i, 0)),
            _once((None, ka, n), lambda i, j: (l, 0, 0)),
            _once((None, kb, n), lambda i, j: (l, 0, 0)),
            pl.BlockSpec((tm, tn), lambda i, j: (i, COL_GA // tn + j)),
            pl.BlockSpec((tm, tn), lambda i, j: (i, COL_GB // tn + j)),
        ],
        out_specs=pl.BlockSpec((tm, tn), lambda i, j: (i, j)),
        out_shape=jax.ShapeDtypeStruct((t, n), BF16),
        scratch_shapes=[pltpu.VMEM((ka, n), BF16), pltpu.VMEM((kb, n), BF16)],
        compiler_params=_params("arbitrary", "arbitrary"),
        name="merge_branches",
    )(oa, ob, wa, wb, p, p)


def _final_norm_kernel(x_ref, gain_ref, o_ref):
    o_ref[...] = _rmsnorm_rows(x_ref[...], gain_ref[...])


def final_rmsnorm(x, gain, *, tm=TM_NORM):
    t, d = x.shape
    return pl.pallas_call(
        _final_norm_kernel,
        grid=(t // tm,),
        in_specs=[pl.BlockSpec((tm, d), lambda i: (i, 0)), pl.BlockSpec((1, d), lambda i: (0, 0))],
        out_specs=pl.BlockSpec((tm, d), lambda i: (i, 0)),
        out_shape=jax.ShapeDtypeStruct((t, d), F32),
        compiler_params=_params("parallel"),
        name="final_norm",
    )(x, gain.reshape(1, d))


def _lower_bounds(lb_logits):
    lb = jnp.cumsum(jax.nn.softmax(lb_logits.astype(F32), axis=0), axis=0)
    return lb - lb[0:1]


def kernel(x, ffn1_norm, ffn1_w_gate, ffn1_w_up, ffn1_w_down, mix_norm, w_in, lb_fwd_logits, lb_bwd_logits, rg_out_norm, attn_sink, w_branch_a, w_branch_b, w_out, ffn2_norm, ffn2_w_gate, ffn2_w_up, ffn2_w_down, final_norm):
    b, s, d = x.shape
    t = b * s
    depth = w_in.shape[0]
    lb_f_all = _lower_bounds(lb_fwd_logits)
    lb_b_all = _lower_bounds(lb_bwd_logits)
    xf = x.reshape(t, d)
    x_norm_in = xf

    def ffn(xf, x_norm_in, gain, wg, wu, wd, l, bf16_copy):
        a = ffn_up(x_norm_in, gain, wg, wu, l)
        return resid_mm(a, wd, xf, l, scale=0.5, tm=TM_DOWN, tn=TN_DOWN, bf16_copy=bf16_copy)

    for l in range(depth):
        xf, x_norm_in = ffn(xf, x_norm_in, ffn1_norm, ffn1_w_gate, ffn1_w_up, ffn1_w_down, l, True)
        p = in_proj(x_norm_in, mix_norm, w_in, l)
        p3 = p.reshape(b, s, IN_COLS)
        o_a = hgrn2_mixer(p3, lb_f_all[l], lb_b_all[l], rg_out_norm[l], heads=RG_HEADS)
        o_b = window_gqa(p3, attn_sink[l])
        merged = merge_branches(o_a.reshape(t, RG_WIDTH), o_b.reshape(t, ATT_WIDTH),
                                w_branch_a, w_branch_b, p, l)
        xf, x_norm_in = resid_mm_resident(merged, w_out, xf, l, tm=TM_OUT, tn=TN_OUT, bf16_copy=True)
        if l + 1 < depth:
            xf, x_norm_in = ffn(xf, x_norm_in, ffn2_norm, ffn2_w_gate, ffn2_w_up, ffn2_w_down, l, True)
        else:
            xf = ffn(xf, x_norm_in, ffn2_norm, ffn2_w_gate, ffn2_w_up, ffn2_w_down, l, False)
    return final_rmsnorm(xf, final_norm).reshape(b, s, d)
```

```python
import functools
import math

import jax
import jax.numpy as jnp
from jax import lax
from jax.experimental import pallas as pl
from jax.experimental.pallas import tpu as pltpu

D_MODEL = 2048
D_FF = 5632
RG_HEADS = 8
RG_DK = 128
RG_DV = 128
RG_WIDTH = RG_HEADS * RG_DV
ATT_HEADS = 8
ATT_KV_HEADS = 2
ATT_GROUP = ATT_HEADS // ATT_KV_HEADS
ATT_HD = 128
ATT_WIDTH = ATT_HEADS * ATT_HD
KV_WIDTH = ATT_KV_HEADS * ATT_HD
WINDOW = 128
Q_BLOCK = 128
IN_COLS = 5 * RG_WIDTH + ATT_WIDTH + 2 * KV_WIDTH + 2 * D_MODEL
EPS = 1e-6

COL_RQ = 0
COL_RI = RG_WIDTH
COL_ZF = 2 * RG_WIDTH
COL_ZB = 3 * RG_WIDTH
COL_RG = 4 * RG_WIDTH
COL_AQ = 5 * RG_WIDTH
COL_AK = COL_AQ + ATT_WIDTH
COL_AV = COL_AK + KV_WIDTH
COL_GA = COL_AV + KV_WIDTH
COL_GB = COL_GA + D_MODEL

LANES = 128
SUBLANES = 8
HGRN_CHUNK = 128
UNROLL_GATES = 8
VMEM_LIMIT_BYTES = 56 * 1024 * 1024
LOG2E = 1.4426950408889634
LN2 = 0.6931471805599453
SAFE_LOG2_RANGE = 240.0

TM = 2048
TM_DOWN = 1024
TM_MERGE = 1024
TM_OUT = 1024
TN_FFN_UP = 512
TN_IN_PROJ = 512
TN_MERGE = 512
TN_DOWN = 256
TN_OUT = 1024
TM_NORM = 512
NORM_STRIP_ROWS = 256
MXU_COLS = 256

BF16 = jnp.bfloat16
F32 = jnp.float32
NT_DIMS = (((1,), (1,)), ((), ()))
TN_DIMS = (((0,), (0,)), ((), ()))


def _params(*sem):
    return pltpu.CompilerParams(dimension_semantics=sem, vmem_limit_bytes=VMEM_LIMIT_BYTES)


def _rmsnorm_rows(x, gain):
    ms = jnp.mean(x * x, axis=-1, keepdims=True)
    return x * lax.rsqrt(ms + EPS) * gain


def _rmsnorm_tile_to_bf16(x_ref, gain_ref, h_ref):
    def strip(r, carry):
        rows = pl.ds(pl.multiple_of(r * NORM_STRIP_ROWS, NORM_STRIP_ROWS), NORM_STRIP_ROWS)
        h_ref[rows, :] = _rmsnorm_rows(x_ref[rows, :].astype(F32), gain_ref[...]).astype(BF16)
        return carry

    lax.fori_loop(0, x_ref.shape[0] // NORM_STRIP_ROWS, strip, 0)


def _once(block_shape, index_map):
    return pl.BlockSpec(block_shape, index_map, pipeline_mode=pl.Buffered(1))


def _token_rows_spec(x, tm, room_for_two):
    spec = pl.BlockSpec if room_for_two else _once
    return spec((tm, x.shape[1]), lambda i, j: (i, 0))


def _ffn_up_kernel(x_ref, gain_ref, wg_ref, wu_ref, a_ref, h_scr):
    @pl.when(pl.program_id(1) == 0)
    def _():
        _rmsnorm_tile_to_bf16(x_ref, gain_ref, h_scr)

    h = h_scr[...]
    for c0 in range(0, a_ref.shape[1], MXU_COLS):
        cols = slice(c0, c0 + MXU_COLS)
        g = jnp.dot(h, wg_ref[:, cols].astype(BF16), preferred_element_type=F32)
        u = jnp.dot(h, wu_ref[:, cols].astype(BF16), preferred_element_type=F32)
        a_ref[:, cols] = (g * jax.nn.sigmoid(g) * u).astype(BF16)


def ffn_up(x, gain, wg, wu, l, *, tm=TM, tn=TN_FFN_UP):
    t, d = x.shape
    f = wg.shape[2]
    return pl.pallas_call(
        _ffn_up_kernel,
        grid=(t // tm, f // tn),
        in_specs=[
            _token_rows_spec(x, tm, room_for_two=False),
            pl.BlockSpec((None, 1, d), lambda i, j: (l, 0, 0)),
            pl.BlockSpec((None, d, tn), lambda i, j: (l, 0, j)),
            pl.BlockSpec((None, d, tn), lambda i, j: (l, 0, j)),
        ],
        out_specs=pl.BlockSpec((tm, tn), lambda i, j: (i, j)),
        out_shape=jax.ShapeDtypeStruct((t, f), BF16),
        scratch_shapes=[pltpu.VMEM((tm, d), BF16)],
        compiler_params=_params("parallel", "arbitrary"),
        name="ffn_up",
    )(x, gain.reshape(gain.shape[0], 1, d), wg, wu)


def _store_residual(out, o_ref, copy_refs):
    o_ref[...] = out
    for ref in copy_refs:
        ref[...] = out.astype(ref.dtype)


def _residual_outputs(t, n, tm, tn, bf16_copy):
    spec = pl.BlockSpec((tm, tn), lambda i, j: (i, j))
    shape = jax.ShapeDtypeStruct((t, n), F32)
    if not bf16_copy:
        return spec, shape
    return [spec, spec], [shape, jax.ShapeDtypeStruct((t, n), BF16)]


def _resid_mm_kernel(a_ref, w_ref, x_ref, o_ref, *copy_refs, scale):
    y = jnp.dot(a_ref[...], w_ref[...].astype(BF16), preferred_element_type=F32)
    _store_residual(x_ref[...] + scale * y, o_ref, copy_refs)


def resid_mm(a, w, x, l, *, scale, tm, tn, bf16_copy):
    t, kdim = a.shape
    n = w.shape[2]
    out_specs, out_shape = _residual_outputs(t, n, tm, tn, bf16_copy)
    return pl.pallas_call(
        functools.partial(_resid_mm_kernel, scale=scale),
        grid=(t // tm, n // tn),
        in_specs=[
            pl.BlockSpec((tm, kdim), lambda i, j: (i, 0)),
            pl.BlockSpec((None, kdim, tn), lambda i, j: (l, 0, j)),
            pl.BlockSpec((tm, tn), lambda i, j: (i, j)),
        ],
        out_specs=out_specs,
        out_shape=out_shape,
        compiler_params=_params("parallel", "arbitrary"),
        name="resid_mm",
    )(a, w, x)


def _resid_mm_resident_kernel(a_ref, w_ref, x_ref, o_ref, w_scr, *, tn):
    @pl.when((pl.program_id(0) == 0) & (pl.program_id(1) == 0))
    def _():
        w_scr[...] = w_ref[...].astype(BF16)

    cols = pl.ds(pl.multiple_of(pl.program_id(1) * tn, tn), tn)
    o_ref[...] = x_ref[...] + jnp.dot(a_ref[...], w_scr[:, cols], preferred_element_type=F32)


def resid_mm_resident(a, w, x, l, *, tm, tn):
    t, kdim = a.shape
    n = w.shape[2]
    out_specs, out_shape = _residual_outputs(t, n, tm, tn, False)
    return pl.pallas_call(
        functools.partial(_resid_mm_resident_kernel, tn=tn),
        grid=(t // tm, n // tn),
        in_specs=[
            pl.BlockSpec((tm, kdim), lambda i, j: (i, 0)),
            _once((None, kdim, n), lambda i, j: (l, 0, 0)),
            pl.BlockSpec((tm, tn), lambda i, j: (i, j)),
        ],
        out_specs=out_specs,
        out_shape=out_shape,
        scratch_shapes=[pltpu.VMEM((kdim, n), BF16)],
        compiler_params=_params("arbitrary", "arbitrary"),
        name="resid_mm_resident",
    )(a, w, x)


def _in_proj_kernel(x_ref, gain_ref, w_ref, p_ref, h_scr):
    @pl.when(pl.program_id(1) == 0)
    def _():
        _rmsnorm_tile_to_bf16(x_ref, gain_ref, h_scr)

    p_ref[...] = jnp.dot(h_scr[...], w_ref[...].astype(BF16), preferred_element_type=F32)


def in_proj(x, gain, w, l, *, tm=TM, tn=TN_IN_PROJ):
    t, d = x.shape
    n = w.shape[2]
    return pl.pallas_call(
        _in_proj_kernel,
        grid=(t // tm, n // tn),
        in_specs=[
            _token_rows_spec(x, tm, room_for_two=x.dtype == BF16),
            pl.BlockSpec((None, 1, d), lambda i, j: (l, 0, 0)),
            pl.BlockSpec((None, d, tn), lambda i, j: (l, 0, j)),
        ],
        out_specs=pl.BlockSpec((tm, tn), lambda i, j: (i, j)),
        out_shape=jax.ShapeDtypeStruct((t, n), F32),
        scratch_shapes=[pltpu.VMEM((tm, d), BF16)],
        compiler_params=_params("parallel", "arbitrary"),
        name="in_proj",
    )(x, gain.reshape(gain.shape[0], 1, d), w)


def _block_ref_rows(p, h, r):
    c, n = p.shape
    blk = 2 * h
    if blk >= SUBLANES:
        p3 = p.reshape(c // blk, blk, n)
        return jnp.broadcast_to(p3[:, r:r + 1, :], p3.shape).reshape(c, n)
    pos = lax.broadcasted_iota(jnp.int32, p.shape, 0) & (blk - 1)
    out = p
    for src in range(blk):
        if src == r:
            continue
        shifted = pltpu.roll(p, (src - r) % c, axis=0)
        out = jnp.where(pos == src, shifted, out)
    return out


def _chunk_scores(q, key, a, masks_ref, mask_base, reverse):
    c, n = q.shape
    row = lax.broadcasted_iota(jnp.int32, q.shape, 0)
    p = a
    scores = None
    h = 1
    level = 0
    while h < c:
        if h >= SUBLANES:
            nb = c // (2 * h)
            p4 = p.reshape(nb, 2, h, n)
            q4 = q.reshape(nb, 2, h, n)
            k4 = key.reshape(nb, 2, h, n)
            lo, hi = p4[:, 0], p4[:, 1]
            if reverse:
                t_row = hi[:, 0:1, :]
                qk = jnp.stack([q4[:, 0] * jnp.exp(lo), k4[:, 1] * jnp.exp(t_row - hi)], axis=1)
                p = jnp.stack([lo + t_row, hi], axis=1).reshape(c, n)
            else:
                t_row = lo[:, h - 1:h, :]
                qk = jnp.stack([k4[:, 0] * jnp.exp(t_row - lo), q4[:, 1] * jnp.exp(hi)], axis=1)
                p = jnp.stack([lo, hi + t_row], axis=1).reshape(c, n)
            qk = qk.reshape(c, n).astype(BF16)
        else:
            upper = (row & h) != 0
            qside = jnp.logical_not(upper) if reverse else upper
            t_rows = _block_ref_rows(p, h, h if reverse else h - 1)
            e = jnp.where(qside, p, t_rows - p)
            qk = (jnp.where(qside, q, key) * jnp.exp(e)).astype(BF16)
            p = p + jnp.where(qside, t_rows, 0.0)
        s = lax.dot_general(qk, qk, NT_DIMS, preferred_element_type=F32)
        s = s * masks_ref[mask_base + level]
        scores = s if scores is None else scores + s
        h *= 2
        level += 1
    return scores, p


def _log2_decay_and_key(z, lb, log1m_lb, one_m_lb):
    u = jnp.exp(-jnp.abs(z))
    one_u = 1.0 + u
    r = 1.0 / one_u
    pos = z >= 0.0
    sig = jnp.where(pos, 1.0, u) * r
    key = one_m_lb * (jnp.where(pos, u, 1.0) * r)
    y2 = (log1m_lb + jnp.minimum(z, 0.0)) * LOG2E - jnp.log2(one_u)
    return jnp.maximum(jnp.log2(lb + one_m_lb * sig), y2), key


def _cumsum_rows(a, tri_bf16):
    a1 = a.astype(BF16)
    r1 = a - a1.astype(F32)
    a2 = r1.astype(BF16)
    a3 = (r1 - a2.astype(F32)).astype(BF16)
    out = jnp.dot(tri_bf16, jnp.concatenate([a1, a2, a3], axis=1), preferred_element_type=F32)
    n = a.shape[1]
    return (out[:, 2 * n:] + out[:, n:2 * n]) + out[:, :n]


def _hgrn_kernel(q_ref, v_ref, zf_ref, zb_ref, g_ref, lbf_ref, lbb_ref, gain_ref, masks_ref, tri_ref, o_ref,
                 of_scr, ob_scr, qhf_scr, qhb_scr, updf_scr, updb_scr, decf_scr, decb_scr, stf_scr, stb_scr,
                 cumf_scr, cumb_scr, keyf_scr, keyb_scr, totf_scr, totb_scr,
                 *, chunk, n_chunks, n_levels):
    c = chunk

    def rows_of(i):
        return pl.ds(pl.multiple_of(i * c, c), c)

    dirs = (
        (False, zf_ref, lbf_ref, of_scr, qhf_scr, updf_scr, decf_scr, cumf_scr, keyf_scr, totf_scr),
        (True, zb_ref, lbb_ref, ob_scr, qhb_scr, updb_scr, decb_scr, cumb_scr, keyb_scr, totb_scr),
    )

    def gates(z_ref, lb_ref, rows):
        return _log2_decay_and_key(z_ref[0, rows, :], lb_ref[0, 0:1, :], lb_ref[0, 1:2, :], lb_ref[0, 2:3, :])

    def phase0(i, lowest):
        rows = rows_of(i)
        for reverse, z_ref, lb_ref, _, _, _, dec_scr, cum_scr, key_scr, tot_scr in dirs:
            a2, key = gates(z_ref, lb_ref, rows)
            cum = _cumsum_rows(a2, tri_ref[1 if reverse else 0])
            total = cum[0:1, :] if reverse else cum[c - 1:c, :]
            cum_scr[rows, :] = cum
            key_scr[rows, :] = key
            tot_scr[i] = jnp.broadcast_to(total, (SUBLANES, LANES))
            dec_scr[i] = jnp.broadcast_to(jnp.exp2(total), (SUBLANES, LANES))
            lowest = jnp.minimum(lowest, total)
        return lowest

    lowest = lax.fori_loop(0, n_chunks, phase0, jnp.zeros((1, LANES), F32), unroll=UNROLL_GATES)
    safe = jnp.min(lowest) > -SAFE_LOG2_RANGE

    @pl.when(safe)
    def _():
        ti = lax.broadcasted_iota(jnp.int32, (c, c), 0)
        si = lax.broadcasted_iota(jnp.int32, (c, c), 1)

        def body(i, carry):
            rows = rows_of(i)
            q = q_ref[0, rows, :]
            vb = v_ref[0, rows, :].astype(BF16)
            for reverse, _, _, o_scr, qh_scr, upd_scr, _, cum_scr, key_scr, tot_scr in dirs:
                cum = cum_scr[rows, :]
                half = tot_scr[i][0:1, :] * 0.5
                qt = q * jnp.exp2(cum - half)
                kt = key_scr[rows, :] * jnp.exp2(half - cum)
                s = lax.dot_general(qt.astype(BF16), kt.astype(BF16), NT_DIMS, preferred_element_type=F32)
                s = jnp.where((si >= ti) if reverse else (si <= ti), s, 0.0)
                o_scr[rows, :] = jnp.dot(s.astype(BF16), vb, preferred_element_type=F32)
                edge = jnp.exp2(half)
                qh_scr[rows, :] = (qt * edge).astype(BF16)
                upd_scr[i] = lax.dot_general(vb, (kt * edge).astype(BF16), TN_DIMS,
                                             preferred_element_type=F32)
            return carry

        lax.fori_loop(0, n_chunks, body, 0, unroll=True)

    @pl.when(jnp.logical_not(safe))
    def _():
        def body(i, carry):
            rows = rows_of(i)
            q = q_ref[0, rows, :]
            v = v_ref[0, rows, :]
            vb = v.astype(BF16)
            for reverse, z_ref, lb_ref, o_scr, qh_scr, upd_scr, _, _, _, _ in dirs:
                a2, key = gates(z_ref, lb_ref, rows)
                scores, cum = _chunk_scores(q, key, a2 * LN2, masks_ref, n_levels if reverse else 0, reverse)
                diag = jnp.sum(q * key, axis=-1, keepdims=True)
                o_scr[rows, :] = jnp.dot(scores.astype(BF16), vb, preferred_element_type=F32) + diag * v
                total = cum[0:1, :] if reverse else cum[c - 1:c, :]
                qh_scr[rows, :] = (q * jnp.exp(cum)).astype(BF16)
                kd = (key * jnp.exp(total - cum)).astype(BF16)
                upd_scr[i] = lax.dot_general(vb, kd, TN_DIMS, preferred_element_type=F32)
            return carry

        lax.fori_loop(0, n_chunks, body, 0)

    def phase2(i, carry):
        s_f, s_b = carry
        j = n_chunks - 1 - i
        stf_scr[i] = s_f.astype(BF16)
        stb_scr[j] = s_b.astype(BF16)
        s_f = s_f * decf_scr[i][0:1, :] + updf_scr[i]
        s_b = s_b * decb_scr[j][0:1, :] + updb_scr[j]
        return s_f, s_b

    zero = jnp.zeros((LANES, LANES), F32)
    lax.fori_loop(0, n_chunks, phase2, (zero, zero))

    def phase3(i, carry):
        rows = rows_of(i)
        o = of_scr[rows, :] + ob_scr[rows, :]
        o += lax.dot_general(qhf_scr[rows, :], stf_scr[i], NT_DIMS, preferred_element_type=F32)
        o += lax.dot_general(qhb_scr[rows, :], stb_scr[i], NT_DIMS, preferred_element_type=F32)
        o = _rmsnorm_rows(o, gain_ref[0])
        g = g_ref[0, rows, :]
        o_ref[0, rows, :] = (o * (g * jax.nn.sigmoid(g))).astype(o_ref.dtype)
        return carry

    lax.fori_loop(0, n_chunks, phase3, 0, unroll=True)


def _level_masks(chunk):
    idx = jnp.arange(chunk)
    t, s = idx[:, None], idx[None, :]
    fwd, bwd = [], []
    h = 1
    while h < chunk:
        same = (t // (2 * h)) == (s // (2 * h))
        t_up = (t & h) != 0
        s_up = (s & h) != 0
        fwd.append(same & t_up & ~s_up)
        bwd.append(same & ~t_up & s_up)
        h *= 2
    return jnp.stack(fwd + bwd).astype(F32)


def hgrn2_mixer(p, lb_f, lb_b, out_gain, *, heads, chunk=HGRN_CHUNK):
    b, s, _ = p.shape
    n_chunks = s // chunk
    n_levels = int(math.log2(chunk))

    def lb_rows(lb):
        lb = lb.astype(F32).reshape(heads, 1, LANES)
        rows = jnp.concatenate([lb, jnp.log1p(-lb), 1.0 - lb], axis=1)
        return jnp.pad(rows, ((0, 0), (0, SUBLANES - 3), (0, 0)))

    def col(block0):
        return pl.BlockSpec((1, s, LANES), lambda bi, hi: (bi, 0, block0 + hi))

    per_head = pl.BlockSpec((1, SUBLANES, LANES), lambda bi, hi: (hi, 0, 0))
    masks = _level_masks(chunk)
    idx = jnp.arange(chunk)
    lower = idx[None, :] <= idx[:, None]
    tri = jnp.stack([lower, lower.T]).astype(BF16)
    kern = functools.partial(_hgrn_kernel, chunk=chunk, n_chunks=n_chunks, n_levels=n_levels)
    seq_f32 = pltpu.VMEM((s, LANES), F32)
    seq_bf16 = pltpu.VMEM((s, LANES), BF16)
    chunk_mat_f32 = pltpu.VMEM((n_chunks, LANES, LANES), F32)
    chunk_mat_bf16 = pltpu.VMEM((n_chunks, LANES, LANES), BF16)
    chunk_row = pltpu.VMEM((n_chunks, SUBLANES, LANES), F32)
    return pl.pallas_call(
        kern,
        grid=(b, heads),
        in_specs=[
            col(COL_RQ // LANES), col(COL_RI // LANES), col(COL_ZF // LANES), col(COL_ZB // LANES),
            col(COL_RG // LANES), per_head, per_head,
            pl.BlockSpec((1, 1, LANES), lambda bi, hi: (hi, 0, 0)),
            pl.BlockSpec(masks.shape, lambda bi, hi: (0, 0, 0)),
            pl.BlockSpec(tri.shape, lambda bi, hi: (0, 0, 0)),
        ],
        out_specs=pl.BlockSpec((1, s, LANES), lambda bi, hi: (bi, 0, hi)),
        out_shape=jax.ShapeDtypeStruct((b, s, heads * LANES), BF16),
        scratch_shapes=[
            seq_f32, seq_f32, seq_bf16, seq_bf16,
            chunk_mat_f32, chunk_mat_f32, chunk_row, chunk_row,
            chunk_mat_bf16, chunk_mat_bf16,
            seq_f32, seq_f32, seq_f32, seq_f32, chunk_row, chunk_row,
        ],
        compiler_params=_params("parallel", "parallel"),
        name="hgrn2",
    )(p, p, p, p, p, lb_rows(lb_f), lb_rows(lb_b), out_gain.astype(F32).reshape(heads, 1, LANES), masks, tri)


ATT_SUB = 2


def _attn_kernel(sink_ref, q_ref, k0_ref, k1_ref, k2_ref, k3_ref, v0_ref, v1_ref, v2_ref, v3_ref,
                 bias0_ref, bias1_ref, o_ref):
    qscale = LOG2E / math.sqrt(ATT_HD)
    qb = Q_BLOCK
    k_refs = (k0_ref, k1_ref, k2_ref, k3_ref)
    v_refs = (v0_ref, v1_ref, v2_ref, v3_ref)

    def head_cols(ref, head, rows=slice(None)):
        return ref[0, rows, head * ATT_HD:(head + 1) * ATT_HD]

    def band(refs, first, kvh):
        return jnp.concatenate([head_cols(r, kvh) for r in refs[first:first + 3]], axis=0).astype(BF16)

    for sub, bias_ref in enumerate((bias0_ref, bias1_ref)):
        rows = slice(sub * qb, (sub + 1) * qb)
        for kvh in range(ATT_KV_HEADS):
            k = band(k_refs, sub, kvh)
            v1 = jnp.concatenate([band(v_refs, sub, kvh), jnp.ones((3 * qb, ATT_HD), BF16)], axis=1)
            for g in range(ATT_GROUP):
                head = kvh * ATT_GROUP + g
                q = (head_cols(q_ref, head, rows) * qscale).astype(BF16)
                s = lax.dot_general(q, k, NT_DIMS, preferred_element_type=F32)
                s = s + bias_ref[0, head * qb:(head + 1) * qb]
                sink = sink_ref[head] * LOG2E
                m = jnp.broadcast_to(jnp.maximum(jnp.max(s, axis=-1, keepdims=True), sink), (qb, ATT_HD))
                e = jnp.exp2(s - jnp.concatenate([m, m, m], axis=1)).astype(BF16)
                ov = jnp.dot(e, v1, preferred_element_type=F32)
                o = ov[:, :ATT_HD] / (ov[:, ATT_HD:] + jnp.exp2(sink - m))
                o_ref[0, rows, head * ATT_HD:(head + 1) * ATT_HD] = o.astype(o_ref.dtype)


def _attn_bias():
    qi = jnp.arange(Q_BLOCK)[:, None]
    kj = jnp.arange(3 * Q_BLOCK)[None, :]
    dist = jnp.abs(kj - Q_BLOCK - qi)
    slopes = 2.0 ** (-8.0 * jnp.arange(1, ATT_HEADS + 1, dtype=F32) / ATT_HEADS)
    alibi = -(slopes[:, None, None] * dist.astype(F32)[None]) * LOG2E
    cases = []
    for has_prev, has_next in ((False, True), (True, True), (True, False)):
        valid = (dist <= WINDOW) & ((kj >= Q_BLOCK) | has_prev) & ((kj < 2 * Q_BLOCK) | has_next)
        cases.append(jnp.where(valid[None], alibi, -jnp.inf).reshape(ATT_HEADS * Q_BLOCK, 3 * Q_BLOCK))
    return jnp.stack(cases)


def window_gqa(p, sink):
    b, s, _ = p.shape
    nb = s // Q_BLOCK
    assert nb >= 2 and nb % ATT_SUB == 0
    kcol, vcol = COL_AK // KV_WIDTH, COL_AV // KV_WIDTH

    def kv(col, off):
        return pl.BlockSpec((1, Q_BLOCK, KV_WIDTH),
                            lambda bi, n, sk: (bi, jnp.clip(ATT_SUB * n - 1 + off, 0, nb - 1), col))

    def bias(sub):
        def case(bi, n, sk):
            blk = ATT_SUB * n + sub
            return (jnp.where(blk == 0, 0, jnp.where(blk == nb - 1, 2, 1)), 0, 0)
        return pl.BlockSpec((1, ATT_HEADS * Q_BLOCK, 3 * Q_BLOCK), case)

    rows = ATT_SUB * Q_BLOCK
    n_band = ATT_SUB + 2
    return pl.pallas_call(
        _attn_kernel,
        grid_spec=pltpu.PrefetchScalarGridSpec(
            num_scalar_prefetch=1,
            grid=(b, nb // ATT_SUB),
            in_specs=[pl.BlockSpec((1, rows, ATT_WIDTH), lambda bi, n, sk: (bi, n, COL_AQ // ATT_WIDTH))]
            + [kv(kcol, off) for off in range(n_band)]
            + [kv(vcol, off) for off in range(n_band)]
            + [bias(sub) for sub in range(ATT_SUB)],
            out_specs=pl.BlockSpec((1, rows, ATT_WIDTH), lambda bi, n, sk: (bi, n, 0)),
        ),
        out_shape=jax.ShapeDtypeStruct((b, s, ATT_WIDTH), BF16),
        compiler_params=_params("parallel", "arbitrary"),
        name="window_gqa",
    )(sink.astype(F32), *([p] * (1 + 2 * n_band)), *([_attn_bias()] * ATT_SUB))


def _merge_kernel(oa_ref, ob_ref, wa_ref, wb_ref, ga_ref, gb_ref, m_ref, wa_scr, wb_scr, *, tn):
    @pl.when((pl.program_id(0) == 0) & (pl.program_id(1) == 0))
    def _():
        wa_scr[...] = wa_ref[...].astype(BF16)
        wb_scr[...] = wb_ref[...].astype(BF16)

    cols = pl.ds(pl.multiple_of(pl.program_id(1) * tn, tn), tn)
    ya = jnp.dot(oa_ref[...], wa_scr[:, cols], preferred_element_type=F32)
    yb = jnp.dot(ob_ref[...], wb_scr[:, cols], preferred_element_type=F32)
    m_ref[...] = (jax.nn.sigmoid(ga_ref[...]) * ya + jax.nn.sigmoid(gb_ref[...]) * yb).astype(m_ref.dtype)


def merge_branches(oa, ob, wa, wb, p, l, *, tm=TM_MERGE, tn=TN_MERGE):
    t, ka = oa.shape
    kb = ob.shape[1]
    n = wa.shape[2]
    assert COL_GA % tn == 0 and COL_GB % tn == 0
    return pl.pallas_call(
        functools.partial(_merge_kernel, tn=tn),
        grid=(t // tm, n // tn),
        in_specs=[
            pl.BlockSpec((tm, ka), lambda i, j: (i, 0)),
            pl.BlockSpec((tm, kb), lambda i, j: (i, 0)),
            _once((None, ka, n), lambda i, j: (l, 0, 0)),
            _once((None, kb, n), lambda i, j: (l, 0, 0)),
            pl.BlockSpec((tm, tn), lambda i, j: (i, COL_GA // tn + j)),
            pl.BlockSpec((tm, tn), lambda i, j: (i, COL_GB // tn + j)),
        ],
        out_specs=pl.BlockSpec((tm, tn), lambda i, j: (i, j)),
        out_shape=jax.ShapeDtypeStruct((t, n), BF16),
        scratch_shapes=[pltpu.VMEM((ka, n), BF16), pltpu.VMEM((kb, n), BF16)],
        compiler_params=_params("arbitrary", "arbitrary"),
        name="merge_branches",
    )(oa, ob, wa, wb, p, p)


def _final_norm_kernel(x_ref, gain_ref, o_ref):
    o_ref[...] = _rmsnorm_rows(x_ref[...], gain_ref[...])


def final_rmsnorm(x, gain, *, tm=TM_NORM):
    t, d = x.shape
    return pl.pallas_call(
        _final_norm_kernel,
        grid=(t // tm,),
        in_specs=[pl.BlockSpec((tm, d), lambda i: (i, 0)), pl.BlockSpec((1, d), lambda i: (0, 0))],
        out_specs=pl.BlockSpec((tm, d), lambda i: (i, 0)),
        out_shape=jax.ShapeDtypeStruct((t, d), F32),
        compiler_params=_params("parallel"),
        name="final_norm",
    )(x, gain.reshape(1, d))


def _lower_bounds(lb_logits):
    lb = jnp.cumsum(jax.nn.softmax(lb_logits.astype(F32), axis=0), axis=0)
    return lb - lb[0:1]


def kernel(x, ffn1_norm, ffn1_w_gate, ffn1_w_up, ffn1_w_down, mix_norm, w_in, lb_fwd_logits, lb_bwd_logits, rg_out_norm, attn_sink, w_branch_a, w_branch_b, w_out, ffn2_norm, ffn2_w_gate, ffn2_w_up, ffn2_w_down, final_norm):
    b, s, d = x.shape
    t = b * s
    depth = w_in.shape[0]
    lb_f_all = _lower_bounds(lb_fwd_logits)
    lb_b_all = _lower_bounds(lb_bwd_logits)
    xf = x.reshape(t, d)
    x_norm_in = xf

    def ffn(xf, x_norm_in, gain, wg, wu, wd, l, bf16_copy):
        a = ffn_up(x_norm_in, gain, wg, wu, l)
        return resid_mm(a, wd, xf, l, scale=0.5, tm=TM_DOWN, tn=TN_DOWN, bf16_copy=bf16_copy)

    for l in range(depth):
        xf, x_norm_in = ffn(xf, x_norm_in, ffn1_norm, ffn1_w_gate, ffn1_w_up, ffn1_w_down, l, True)
        p = in_proj(x_norm_in, mix_norm, w_in, l)
        p3 = p.reshape(b, s, IN_COLS)
        o_a = hgrn2_mixer(p3, lb_f_all[l], lb_b_all[l], rg_out_norm[l], heads=RG_HEADS)
        o_b = window_gqa(p3, attn_sink[l])
        merged = merge_branches(o_a.reshape(t, RG_WIDTH), o_b.reshape(t, ATT_WIDTH),
                                w_branch_a, w_branch_b, p, l)
        xf = x_norm_in = resid_mm_resident(merged, w_out, xf, l, tm=TM_OUT, tn=TN_OUT)
        if l + 1 < depth:
            xf, x_norm_in = ffn(xf, x_norm_in, ffn2_norm, ffn2_w_gate, ffn2_w_up, ffn2_w_down, l, True)
        else:
            xf = ffn(xf, x_norm_in, ffn2_norm, ffn2_w_gate, ffn2_w_up, ffn2_w_down, l, False)
    return final_rmsnorm(xf, final_norm).reshape(b, s, d)
```

```python
import functools
import math

import jax
import jax.numpy as jnp
from jax import lax
from jax.experimental import pallas as pl
from jax.experimental.pallas import tpu as pltpu

D_MODEL = 2048
D_FF = 5632
RG_HEADS = 8
RG_DK = 128
RG_DV = 128
RG_WIDTH = RG_HEADS * RG_DV
ATT_HEADS = 8
ATT_KV_HEADS = 2
ATT_GROUP = ATT_HEADS // ATT_KV_HEADS
ATT_HD = 128
ATT_WIDTH = ATT_HEADS * ATT_HD
KV_WIDTH = ATT_KV_HEADS * ATT_HD
WINDOW = 128
Q_BLOCK = 128
IN_COLS = 5 * RG_WIDTH + ATT_WIDTH + 2 * KV_WIDTH + 2 * D_MODEL
EPS = 1e-6

COL_RQ = 0
COL_RI = RG_WIDTH
COL_ZF = 2 * RG_WIDTH
COL_ZB = 3 * RG_WIDTH
COL_RG = 4 * RG_WIDTH
COL_AQ = 5 * RG_WIDTH
COL_AK = COL_AQ + ATT_WIDTH
COL_AV = COL_AK + KV_WIDTH
COL_GA = COL_AV + KV_WIDTH
COL_GB = COL_GA + D_MODEL

LANES = 128
SUBLANES = 8
HGRN_CHUNK = 128
UNROLL_GATES = 8
VMEM_LIMIT_BYTES = 56 * 1024 * 1024
LOG2E = 1.4426950408889634
LN2 = 0.6931471805599453
SAFE_LOG2_RANGE = 240.0

TM = 2048
TM_DOWN = 1024
TM_MERGE = 1024
TM_OUT = 1024
TN_FFN_UP = 512
TN_IN_PROJ = 512
TN_MERGE = 512
TN_DOWN = 512
TN_OUT = 1024
TM_NORM = 512
CAST_STRIP_ROWS = 512
NORM_STRIP_ROWS = 256
MXU_COLS = 256

BF16 = jnp.bfloat16
F32 = jnp.float32
NT_DIMS = (((1,), (1,)), ((), ()))
TN_DIMS = (((0,), (0,)), ((), ()))


def _params(*sem):
    return pltpu.CompilerParams(dimension_semantics=sem, vmem_limit_bytes=VMEM_LIMIT_BYTES)


def _rmsnorm_rows(x, gain):
    ms = jnp.mean(x * x, axis=-1, keepdims=True)
    return x * lax.rsqrt(ms + EPS) * gain


def _rmsnorm_tile_to_bf16(x_ref, gain_ref, h_ref):
    def strip(r, carry):
        rows = pl.ds(pl.multiple_of(r * NORM_STRIP_ROWS, NORM_STRIP_ROWS), NORM_STRIP_ROWS)
        h_ref[rows, :] = _rmsnorm_rows(x_ref[rows, :].astype(F32), gain_ref[...]).astype(BF16)
        return carry

    lax.fori_loop(0, x_ref.shape[0] // NORM_STRIP_ROWS, strip, 0)


def _once(block_shape, index_map):
    return pl.BlockSpec(block_shape, index_map, pipeline_mode=pl.Buffered(1))


def _token_rows_spec(x, tm, room_for_two):
    spec = pl.BlockSpec if room_for_two else _once
    return spec((tm, x.shape[1]), lambda i, j: (i, 0))


def _ffn_up_kernel(x_ref, gain_ref, wg_ref, wu_ref, a_ref, h_scr):
    @pl.when(pl.program_id(1) == 0)
    def _():
        _rmsnorm_tile_to_bf16(x_ref, gain_ref, h_scr)

    h = h_scr[...]
    for c0 in range(0, a_ref.shape[1], MXU_COLS):
        cols = slice(c0, c0 + MXU_COLS)
        g = jnp.dot(h, wg_ref[:, cols].astype(BF16), preferred_element_type=F32)
        u = jnp.dot(h, wu_ref[:, cols].astype(BF16), preferred_element_type=F32)
        a_ref[:, cols] = (g * jax.nn.sigmoid(g) * u).astype(BF16)


def ffn_up(x, gain, wg, wu, l, *, tm=TM, tn=TN_FFN_UP):
    t, d = x.shape
    f = wg.shape[2]
    return pl.pallas_call(
        _ffn_up_kernel,
        grid=(t // tm, f // tn),
        in_specs=[
            _token_rows_spec(x, tm, room_for_two=False),
            pl.BlockSpec((None, 1, d), lambda i, j: (l, 0, 0)),
            pl.BlockSpec((None, d, tn), lambda i, j: (l, 0, j)),
            pl.BlockSpec((None, d, tn), lambda i, j: (l, 0, j)),
        ],
        out_specs=pl.BlockSpec((tm, tn), lambda i, j: (i, j)),
        out_shape=jax.ShapeDtypeStruct((t, f), BF16),
        scratch_shapes=[pltpu.VMEM((tm, d), BF16)],
        compiler_params=_params("parallel", "arbitrary"),
        name="ffn_up",
    )(x, gain.reshape(gain.shape[0], 1, d), wg, wu)


def _store_residual(out, o_ref, copy_refs):
    o_ref[...] = out
    for ref in copy_refs:
        ref[...] = out.astype(ref.dtype)


def _resid_mm_kernel(a_ref, w_ref, x_ref, o_ref, *rest, scale):
    *copy_refs, w_scr = rest

    @pl.when(pl.program_id(1) == 0)
    def _():
        def strip(r, carry):
            rows = pl.ds(pl.multiple_of(r * CAST_STRIP_ROWS, CAST_STRIP_ROWS), CAST_STRIP_ROWS)
            w_scr[rows, :] = w_ref[rows, :].astype(BF16)
            return carry

        lax.fori_loop(0, w_ref.shape[0] // CAST_STRIP_ROWS, strip, 0)

    y = jnp.dot(a_ref[...], w_scr[...], preferred_element_type=F32)
    _store_residual(x_ref[...] + (y if scale == 1.0 else scale * y), o_ref, copy_refs)


def resid_mm(a, w, x, l, *, scale, tm, tn, bf16_copy):
    t, kdim = a.shape
    n = w.shape[2]
    assert kdim % CAST_STRIP_ROWS == 0
    tile = pl.BlockSpec((tm, tn), lambda j, i: (i, j))
    shape = jax.ShapeDtypeStruct((t, n), F32)
    out_specs, out_shape = tile, shape
    if bf16_copy:
        out_specs, out_shape = [tile, tile], [shape, jax.ShapeDtypeStruct((t, n), BF16)]
    return pl.pallas_call(
        functools.partial(_resid_mm_kernel, scale=scale),
        grid=(n // tn, t // tm),
        in_specs=[
            pl.BlockSpec((tm, kdim), lambda j, i: (i, 0)),
            _once((None, kdim, tn), lambda j, i: (l, 0, j)),
            tile,
        ],
        out_specs=out_specs,
        out_shape=out_shape,
        scratch_shapes=[pltpu.VMEM((kdim, tn), BF16)],
        compiler_params=_params("arbitrary", "arbitrary"),
        name="resid_mm",
    )(a, w, x)


def _in_proj_kernel(x_ref, gain_ref, w_ref, p_ref, h_scr):
    @pl.when(pl.program_id(1) == 0)
    def _():
        _rmsnorm_tile_to_bf16(x_ref, gain_ref, h_scr)

    p_ref[...] = jnp.dot(h_scr[...], w_ref[...].astype(BF16), preferred_element_type=F32)


def in_proj(x, gain, w, l, *, tm=TM, tn=TN_IN_PROJ):
    t, d = x.shape
    n = w.shape[2]
    return pl.pallas_call(
        _in_proj_kernel,
        grid=(t // tm, n // tn),
        in_specs=[
            _token_rows_spec(x, tm, room_for_two=x.dtype == BF16),
            pl.BlockSpec((None, 1, d), lambda i, j: (l, 0, 0)),
            pl.BlockSpec((None, d, tn), lambda i, j: (l, 0, j)),
        ],
        out_specs=pl.BlockSpec((tm, tn), lambda i, j: (i, j)),
        out_shape=jax.ShapeDtypeStruct((t, n), F32),
        scratch_shapes=[pltpu.VMEM((tm, d), BF16)],
        compiler_params=_params("parallel", "arbitrary"),
        name="in_proj",
    )(x, gain.reshape(gain.shape[0], 1, d), w)


def _block_ref_rows(p, h, r):
    c, n = p.shape
    blk = 2 * h
    if blk >= SUBLANES:
        p3 = p.reshape(c // blk, blk, n)
        return jnp.broadcast_to(p3[:, r:r + 1, :], p3.shape).reshape(c, n)
    pos = lax.broadcasted_iota(jnp.int32, p.shape, 0) & (blk - 1)
    out = p
    for src in range(blk):
        if src == r:
            continue
        shifted = pltpu.roll(p, (src - r) % c, axis=0)
        out = jnp.where(pos == src, shifted, out)
    return out


def _chunk_scores(q, key, a, masks_ref, mask_base, reverse):
    c, n = q.shape
    row = lax.broadcasted_iota(jnp.int32, q.shape, 0)
    p = a
    scores = None
    h = 1
    level = 0
    while h < c:
        if h >= SUBLANES:
            nb = c // (2 * h)
            p4 = p.reshape(nb, 2, h, n)
            q4 = q.reshape(nb, 2, h, n)
            k4 = key.reshape(nb, 2, h, n)
            lo, hi = p4[:, 0], p4[:, 1]
            if reverse:
                t_row = hi[:, 0:1, :]
                qk = jnp.stack([q4[:, 0] * jnp.exp(lo), k4[:, 1] * jnp.exp(t_row - hi)], axis=1)
                p = jnp.stack([lo + t_row, hi], axis=1).reshape(c, n)
            else:
                t_row = lo[:, h - 1:h, :]
                qk = jnp.stack([k4[:, 0] * jnp.exp(t_row - lo), q4[:, 1] * jnp.exp(hi)], axis=1)
                p = jnp.stack([lo, hi + t_row], axis=1).reshape(c, n)
            qk = qk.reshape(c, n).astype(BF16)
        else:
            upper = (row & h) != 0
            qside = jnp.logical_not(upper) if reverse else upper
            t_rows = _block_ref_rows(p, h, h if reverse else h - 1)
            e = jnp.where(qside, p, t_rows - p)
            qk = (jnp.where(qside, q, key) * jnp.exp(e)).astype(BF16)
            p = p + jnp.where(qside, t_rows, 0.0)
        s = lax.dot_general(qk, qk, NT_DIMS, preferred_element_type=F32)
        s = s * masks_ref[mask_base + level]
        scores = s if scores is None else scores + s
        h *= 2
        level += 1
    return scores, p


def _log2_decay_and_key(z, lb, log1m_lb, one_m_lb):
    u = jnp.exp(-jnp.abs(z))
    one_u = 1.0 + u
    r = 1.0 / one_u
    pos = z >= 0.0
    sig = jnp.where(pos, 1.0, u) * r
    key = one_m_lb * (jnp.where(pos, u, 1.0) * r)
    y2 = (log1m_lb + jnp.minimum(z, 0.0)) * LOG2E - jnp.log2(one_u)
    return jnp.maximum(jnp.log2(lb + one_m_lb * sig), y2), key


def _cumsum_rows(a, tri_bf16):
    a1 = a.astype(BF16)
    r1 = a - a1.astype(F32)
    a2 = r1.astype(BF16)
    a3 = (r1 - a2.astype(F32)).astype(BF16)
    out = jnp.dot(tri_bf16, jnp.concatenate([a1, a2, a3], axis=1), preferred_element_type=F32)
    n = a.shape[1]
    return (out[:, 2 * n:] + out[:, n:2 * n]) + out[:, :n]


def _hgrn_kernel(q_ref, v_ref, zf_ref, zb_ref, g_ref, lbf_ref, lbb_ref, gain_ref, masks_ref, tri_ref, o_ref,
                 of_scr, ob_scr, qhf_scr, qhb_scr, updf_scr, updb_scr, decf_scr, decb_scr, stf_scr, stb_scr,
                 cumf_scr, cumb_scr, keyf_scr, keyb_scr, totf_scr, totb_scr,
                 *, chunk, n_chunks, n_levels):
    c = chunk

    def rows_of(i):
        return pl.ds(pl.multiple_of(i * c, c), c)

    dirs = (
        (False, zf_ref, lbf_ref, of_scr, qhf_scr, updf_scr, decf_scr, cumf_scr, keyf_scr, totf_scr),
        (True, zb_ref, lbb_ref, ob_scr, qhb_scr, updb_scr, decb_scr, cumb_scr, keyb_scr, totb_scr),
    )

    def gates(z_ref, lb_ref, rows):
        return _log2_decay_and_key(z_ref[0, rows, :], lb_ref[0, 0:1, :], lb_ref[0, 1:2, :], lb_ref[0, 2:3, :])

    def phase0(i, lowest):
        rows = rows_of(i)
        for reverse, z_ref, lb_ref, _, _, _, dec_scr, cum_scr, key_scr, tot_scr in dirs:
            a2, key = gates(z_ref, lb_ref, rows)
            cum = _cumsum_rows(a2, tri_ref[1 if reverse else 0])
            total = cum[0:1, :] if reverse else cum[c - 1:c, :]
            cum_scr[rows, :] = cum
            key_scr[rows, :] = key
            tot_scr[i] = jnp.broadcast_to(total, (SUBLANES, LANES))
            dec_scr[i] = jnp.broadcast_to(jnp.exp2(total), (SUBLANES, LANES))
            lowest = jnp.minimum(lowest, total)
        return lowest

    lowest = lax.fori_loop(0, n_chunks, phase0, jnp.zeros((1, LANES), F32), unroll=UNROLL_GATES)
    safe = jnp.min(lowest) > -SAFE_LOG2_RANGE

    @pl.when(safe)
    def _():
        ti = lax.broadcasted_iota(jnp.int32, (c, c), 0)
        si = lax.broadcasted_iota(jnp.int32, (c, c), 1)

        def body(i, carry):
            rows = rows_of(i)
            q = q_ref[0, rows, :]
            vb = v_ref[0, rows, :].astype(BF16)
            for reverse, _, _, o_scr, qh_scr, upd_scr, _, cum_scr, key_scr, tot_scr in dirs:
                cum = cum_scr[rows, :]
                half = tot_scr[i][0:1, :] * 0.5
                qt = q * jnp.exp2(cum - half)
                kt = key_scr[rows, :] * jnp.exp2(half - cum)
                s = lax.dot_general(qt.astype(BF16), kt.astype(BF16), NT_DIMS, preferred_element_type=F32)
                s = jnp.where((si >= ti) if reverse else (si <= ti), s, 0.0)
                o_scr[rows, :] = jnp.dot(s.astype(BF16), vb, preferred_element_type=F32)
                edge = jnp.exp2(half)
                qh_scr[rows, :] = (qt * edge).astype(BF16)
                upd_scr[i] = lax.dot_general(vb, (kt * edge).astype(BF16), TN_DIMS,
                                             preferred_element_type=F32)
            return carry

        lax.fori_loop(0, n_chunks, body, 0, unroll=True)

    @pl.when(jnp.logical_not(safe))
    def _():
        def body(i, carry):
            rows = rows_of(i)
            q = q_ref[0, rows, :]
            v = v_ref[0, rows, :]
            vb = v.astype(BF16)
            for reverse, z_ref, lb_ref, o_scr, qh_scr, upd_scr, _, _, _, _ in dirs:
                a2, key = gates(z_ref, lb_ref, rows)
                scores, cum = _chunk_scores(q, key, a2 * LN2, masks_ref, n_levels if reverse else 0, reverse)
                diag = jnp.sum(q * key, axis=-1, keepdims=True)
                o_scr[rows, :] = jnp.dot(scores.astype(BF16), vb, preferred_element_type=F32) + diag * v
                total = cum[0:1, :] if reverse else cum[c - 1:c, :]
                qh_scr[rows, :] = (q * jnp.exp(cum)).astype(BF16)
                kd = (key * jnp.exp(total - cum)).astype(BF16)
                upd_scr[i] = lax.dot_general(vb, kd, TN_DIMS, preferred_element_type=F32)
            return carry

        lax.fori_loop(0, n_chunks, body, 0)

    def phase2(i, carry):
        s_f, s_b = carry
        j = n_chunks - 1 - i
        stf_scr[i] = s_f.astype(BF16)
        stb_scr[j] = s_b.astype(BF16)
        s_f = s_f * decf_scr[i][0:1, :] + updf_scr[i]
        s_b = s_b * decb_scr[j][0:1, :] + updb_scr[j]
        return s_f, s_b

    zero = jnp.zeros((LANES, LANES), F32)
    lax.fori_loop(0, n_chunks, phase2, (zero, zero))

    def phase3(i, carry):
        rows = rows_of(i)
        o = of_scr[rows, :] + ob_scr[rows, :]
        o += lax.dot_general(qhf_scr[rows, :], stf_scr[i], NT_DIMS, preferred_element_type=F32)
        o += lax.dot_general(qhb_scr[rows, :], stb_scr[i], NT_DIMS, preferred_element_type=F32)
        o = _rmsnorm_rows(o, gain_ref[0])
        g = g_ref[0, rows, :]
        o_ref[0, rows, :] = (o * (g * jax.nn.sigmoid(g))).astype(o_ref.dtype)
        return carry

    lax.fori_loop(0, n_chunks, phase3, 0, unroll=True)


def _level_masks(chunk):
    idx = jnp.arange(chunk)
    t, s = idx[:, None], idx[None, :]
    fwd, bwd = [], []
    h = 1
    while h < chunk:
        same = (t // (2 * h)) == (s // (2 * h))
        t_up = (t & h) != 0
        s_up = (s & h) != 0
        fwd.append(same & t_up & ~s_up)
        bwd.append(same & ~t_up & s_up)
        h *= 2
    return jnp.stack(fwd + bwd).astype(F32)


def hgrn2_mixer(p, lb_f, lb_b, out_gain, *, heads, chunk=HGRN_CHUNK):
    b, s, _ = p.shape
    n_chunks = s // chunk
    n_levels = int(math.log2(chunk))

    def lb_rows(lb):
        lb = lb.astype(F32).reshape(heads, 1, LANES)
        rows = jnp.concatenate([lb, jnp.log1p(-lb), 1.0 - lb], axis=1)
        return jnp.pad(rows, ((0, 0), (0, SUBLANES - 3), (0, 0)))

    def col(block0):
        return pl.BlockSpec((1, s, LANES), lambda bi, hi: (bi, 0, block0 + hi))

    per_head = pl.BlockSpec((1, SUBLANES, LANES), lambda bi, hi: (hi, 0, 0))
    masks = _level_masks(chunk)
    idx = jnp.arange(chunk)
    lower = idx[None, :] <= idx[:, None]
    tri = jnp.stack([lower, lower.T]).astype(BF16)
    kern = functools.partial(_hgrn_kernel, chunk=chunk, n_chunks=n_chunks, n_levels=n_levels)
    seq_f32 = pltpu.VMEM((s, LANES), F32)
    seq_bf16 = pltpu.VMEM((s, LANES), BF16)
    chunk_mat_f32 = pltpu.VMEM((n_chunks, LANES, LANES), F32)
    chunk_mat_bf16 = pltpu.VMEM((n_chunks, LANES, LANES), BF16)
    chunk_row = pltpu.VMEM((n_chunks, SUBLANES, LANES), F32)
    return pl.pallas_call(
        kern,
        grid=(b, heads),
        in_specs=[
            col(COL_RQ // LANES), col(COL_RI // LANES), col(COL_ZF // LANES), col(COL_ZB // LANES),
            col(COL_RG // LANES), per_head, per_head,
            pl.BlockSpec((1, 1, LANES), lambda bi, hi: (hi, 0, 0)),
            pl.BlockSpec(masks.shape, lambda bi, hi: (0, 0, 0)),
            pl.BlockSpec(tri.shape, lambda bi, hi: (0, 0, 0)),
        ],
        out_specs=pl.BlockSpec((1, s, LANES), lambda bi, hi: (bi, 0, hi)),
        out_shape=jax.ShapeDtypeStruct((b, s, heads * LANES), BF16),
        scratch_shapes=[
            seq_f32, seq_f32, seq_bf16, seq_bf16,
            chunk_mat_f32, chunk_mat_f32, chunk_row, chunk_row,
            chunk_mat_bf16, chunk_mat_bf16,
            seq_f32, seq_f32, seq_f32, seq_f32, chunk_row, chunk_row,
        ],
        compiler_params=_params("parallel", "parallel"),
        name="hgrn2",
    )(p, p, p, p, p, lb_rows(lb_f), lb_rows(lb_b), out_gain.astype(F32).reshape(heads, 1, LANES), masks, tri)


ATT_SUB = 2


def _attn_kernel(sink_ref, q_ref, k0_ref, k1_ref, k2_ref, k3_ref, v0_ref, v1_ref, v2_ref, v3_ref,
                 bias0_ref, bias1_ref, o_ref):
    qscale = LOG2E / math.sqrt(ATT_HD)
    qb = Q_BLOCK
    k_refs = (k0_ref, k1_ref, k2_ref, k3_ref)
    v_refs = (v0_ref, v1_ref, v2_ref, v3_ref)

    def head_cols(ref, head, rows=slice(None)):
        return ref[0, rows, head * ATT_HD:(head + 1) * ATT_HD]

    def band(refs, first, kvh):
        return jnp.concatenate([head_cols(r, kvh) for r in refs[first:first + 3]], axis=0).astype(BF16)

    for sub, bias_ref in enumerate((bias0_ref, bias1_ref)):
        rows = slice(sub * qb, (sub + 1) * qb)
        for kvh in range(ATT_KV_HEADS):
            k = band(k_refs, sub, kvh)
            v1 = jnp.concatenate([band(v_refs, sub, kvh), jnp.ones((3 * qb, ATT_HD), BF16)], axis=1)
            for g in range(ATT_GROUP):
                head = kvh * ATT_GROUP + g
                q = (head_cols(q_ref, head, rows) * qscale).astype(BF16)
                s = lax.dot_general(q, k, NT_DIMS, preferred_element_type=F32)
                s = s + bias_ref[0, head * qb:(head + 1) * qb]
                sink = sink_ref[head] * LOG2E
                m = jnp.broadcast_to(jnp.maximum(jnp.max(s, axis=-1, keepdims=True), sink), (qb, ATT_HD))
                e = jnp.exp2(s - jnp.concatenate([m, m, m], axis=1)).astype(BF16)
                ov = jnp.dot(e, v1, preferred_element_type=F32)
                o = ov[:, :ATT_HD] / (ov[:, ATT_HD:] + jnp.exp2(sink - m))
                o_ref[0, rows, head * ATT_HD:(head + 1) * ATT_HD] = o.astype(o_ref.dtype)


def _attn_bias():
    qi = jnp.arange(Q_BLOCK)[:, None]
    kj = jnp.arange(3 * Q_BLOCK)[None, :]
    dist = jnp.abs(kj - Q_BLOCK - qi)
    slopes = 2.0 ** (-8.0 * jnp.arange(1, ATT_HEADS + 1, dtype=F32) / ATT_HEADS)
    alibi = -(slopes[:, None, None] * dist.astype(F32)[None]) * LOG2E
    cases = []
    for has_prev, has_next in ((False, True), (True, True), (True, False)):
        valid = (dist <= WINDOW) & ((kj >= Q_BLOCK) | has_prev) & ((kj < 2 * Q_BLOCK) | has_next)
        cases.append(jnp.where(valid[None], alibi, -jnp.inf).reshape(ATT_HEADS * Q_BLOCK, 3 * Q_BLOCK))
    return jnp.stack(cases)


def window_gqa(p, sink):
    b, s, _ = p.shape
    nb = s // Q_BLOCK
    assert nb >= 2 and nb % ATT_SUB == 0
    kcol, vcol = COL_AK // KV_WIDTH, COL_AV // KV_WIDTH

    def kv(col, off):
        return pl.BlockSpec((1, Q_BLOCK, KV_WIDTH),
                            lambda bi, n, sk: (bi, jnp.clip(ATT_SUB * n - 1 + off, 0, nb - 1), col))

    def bias(sub):
        def case(bi, n, sk):
            blk = ATT_SUB * n + sub
            return (jnp.where(blk == 0, 0, jnp.where(blk == nb - 1, 2, 1)), 0, 0)
        return pl.BlockSpec((1, ATT_HEADS * Q_BLOCK, 3 * Q_BLOCK), case)

    rows = ATT_SUB * Q_BLOCK
    n_band = ATT_SUB + 2
    return pl.pallas_call(
        _attn_kernel,
        grid_spec=pltpu.PrefetchScalarGridSpec(
            num_scalar_prefetch=1,
            grid=(b, nb // ATT_SUB),
            in_specs=[pl.BlockSpec((1, rows, ATT_WIDTH), lambda bi, n, sk: (bi, n, COL_AQ // ATT_WIDTH))]
            + [kv(kcol, off) for off in range(n_band)]
            + [kv(vcol, off) for off in range(n_band)]
            + [bias(sub) for sub in range(ATT_SUB)],
            out_specs=pl.BlockSpec((1, rows, ATT_WIDTH), lambda bi, n, sk: (bi, n, 0)),
        ),
        out_shape=jax.ShapeDtypeStruct((b, s, ATT_WIDTH), BF16),
        compiler_params=_params("parallel", "arbitrary"),
        name="window_gqa",
    )(sink.astype(F32), *([p] * (1 + 2 * n_band)), *([_attn_bias()] * ATT_SUB))


def _merge_kernel(oa_ref, ob_ref, wa_ref, wb_ref, ga_ref, gb_ref, m_ref, wa_scr, wb_scr, *, tn):
    @pl.when((pl.program_id(0) == 0) & (pl.program_id(1) == 0))
    def _():
        wa_scr[...] = wa_ref[...].astype(BF16)
        wb_scr[...] = wb_ref[...].astype(BF16)

    cols = pl.ds(pl.multiple_of(pl.program_id(1) * tn, tn), tn)
    ya = jnp.dot(oa_ref[...], wa_scr[:, cols], preferred_element_type=F32)
    yb = jnp.dot(ob_ref[...], wb_scr[:, cols], preferred_element_type=F32)
    m_ref[...] = (jax.nn.sigmoid(ga_ref[...]) * ya + jax.nn.sigmoid(gb_ref[...]) * yb).astype(m_ref.dtype)


def merge_branches(oa, ob, wa, wb, p, l, *, tm=TM_MERGE, tn=TN_MERGE):
    t, ka = oa.shape
    kb = ob.shape[1]
    n = wa.shape[2]
    assert COL_GA % tn == 0 and COL_GB % tn == 0
    return pl.pallas_call(
        functools.partial(_merge_kernel, tn=tn),
        grid=(t // tm, n // tn),
        in_specs=[
            pl.BlockSpec((tm, ka), lambda i, j: (i, 0)),
            pl.BlockSpec((tm, kb), lambda i, j: (i, 0)),
            _once((None, ka, n), lambda i, j: (l, 0, 0)),
            _once((None, kb, n), lambda i, j: (l, 0, 0)),
            pl.BlockSpec((tm, tn), lambda i, j: (i, COL_GA // tn + j)),
            pl.BlockSpec((tm, tn), lambda i, j: (i, COL_GB // tn + j)),
        ],
        out_specs=pl.BlockSpec((tm, tn), lambda i, j: (i, j)),
        out_shape=jax.ShapeDtypeStruct((t, n), BF16),
        scratch_shapes=[pltpu.VMEM((ka, n), BF16), pltpu.VMEM((kb, n), BF16)],
        compiler_params=_params("arbitrary", "arbitrary"),
        name="merge_branches",
    )(oa, ob, wa, wb, p, p)


def _final_norm_kernel(x_ref, gain_ref, o_ref):
    o_ref[...] = _rmsnorm_rows(x_ref[...], gain_ref[...])


def final_rmsnorm(x, gain, *, tm=TM_NORM):
    t, d = x.shape
    return pl.pallas_call(
        _final_norm_kernel,
        grid=(t // tm,),
        in_specs=[pl.BlockSpec((tm, d), lambda i: (i, 0)), pl.BlockSpec((1, d), lambda i: (0, 0))],
        out_specs=pl.BlockSpec((tm, d), lambda i: (i, 0)),
        out_shape=jax.ShapeDtypeStruct((t, d), F32),
        compiler_params=_params("parallel"),
        name="final_norm",
    )(x, gain.reshape(1, d))


def _lower_bounds(lb_logits):
    lb = jnp.cumsum(jax.nn.softmax(lb_logits.astype(F32), axis=0), axis=0)
    return lb - lb[0:1]


def kernel(x, ffn1_norm, ffn1_w_gate, ffn1_w_up, ffn1_w_down, mix_norm, w_in, lb_fwd_logits, lb_bwd_logits, rg_out_norm, attn_sink, w_branch_a, w_branch_b, w_out, ffn2_norm, ffn2_w_gate, ffn2_w_up, ffn2_w_down, final_norm):
    b, s, d = x.shape
    t = b * s
    depth = w_in.shape[0]
    lb_f_all = _lower_bounds(lb_fwd_logits)
    lb_b_all = _lower_bounds(lb_bwd_logits)
    xf = x.reshape(t, d)
    x_norm_in = xf

    def ffn(xf, x_norm_in, gain, wg, wu, wd, l, bf16_copy):
        a = ffn_up(x_norm_in, gain, wg, wu, l)
        return resid_mm(a, wd, xf, l, scale=0.5, tm=TM_DOWN, tn=TN_DOWN, bf16_copy=bf16_copy)

    for l in range(depth):
        xf, x_norm_in = ffn(xf, x_norm_in, ffn1_norm, ffn1_w_gate, ffn1_w_up, ffn1_w_down, l, True)
        p = in_proj(x_norm_in, mix_norm, w_in, l)
        p3 = p.reshape(b, s, IN_COLS)
        o_a = hgrn2_mixer(p3, lb_f_all[l], lb_b_all[l], rg_out_norm[l], heads=RG_HEADS)
        o_b = window_gqa(p3, attn_sink[l])
        merged = merge_branches(o_a.reshape(t, RG_WIDTH), o_b.reshape(t, ATT_WIDTH),
                                w_branch_a, w_branch_b, p, l)
        xf = x_norm_in = resid_mm(merged, w_out, xf, l, scale=1.0, tm=TM_OUT, tn=TN_OUT, bf16_copy=False)
        if l + 1 < depth:
            xf, x_norm_in = ffn(xf, x_norm_in, ffn2_norm, ffn2_w_gate, ffn2_w_up, ffn2_w_down, l, True)
        else:
            xf = ffn(xf, x_norm_in, ffn2_norm, ffn2_w_gate, ffn2_w_up, ffn2_w_down, l, False)
    return final_rmsnorm(xf, final_norm).reshape(b, s, d)
```

```python
import functools
import math

import jax
import jax.numpy as jnp
from jax import lax
from jax.experimental import pallas as pl
from jax.experimental.pallas import tpu as pltpu

D_MODEL = 2048
D_FF = 5632
RG_HEADS = 8
RG_DK = 128
RG_DV = 128
RG_WIDTH = RG_HEADS * RG_DV
ATT_HEADS = 8
ATT_KV_HEADS = 2
ATT_GROUP = ATT_HEADS // ATT_KV_HEADS
ATT_HD = 128
ATT_WIDTH = ATT_HEADS * ATT_HD
KV_WIDTH = ATT_KV_HEADS * ATT_HD
WINDOW = 128
Q_BLOCK = 128
IN_COLS = 5 * RG_WIDTH + ATT_WIDTH + 2 * KV_WIDTH + 2 * D_MODEL
EPS = 1e-6

COL_RQ = 0
COL_RI = RG_WIDTH
COL_ZF = 2 * RG_WIDTH
COL_ZB = 3 * RG_WIDTH
COL_RG = 4 * RG_WIDTH
COL_AQ = 5 * RG_WIDTH
COL_AK = COL_AQ + ATT_WIDTH
COL_AV = COL_AK + KV_WIDTH
COL_GA = COL_AV + KV_WIDTH
COL_GB = COL_GA + D_MODEL

LANES = 128
SUBLANES = 8
HGRN_CHUNK = 128
UNROLL_GATES = 16
VMEM_LIMIT_BYTES = 56 * 1024 * 1024
LOG2E = 1.4426950408889634
LN2 = 0.6931471805599453
SAFE_LOG2_RANGE = 240.0

TM = 2048
TM_DOWN = 1024
TM_MERGE = 1024
TM_OUT = 512
TN_FFN_UP = 512
TN_IN_PROJ = 512
TN_MERGE = 512
TN_DOWN = 512
TN_OUT = 2048
TM_NORM = 512
CAST_STRIP_ROWS = 512
NORM_STRIP_ROWS = 256
MXU_COLS = 256

BF16 = jnp.bfloat16
F32 = jnp.float32
NT_DIMS = (((1,), (1,)), ((), ()))
TN_DIMS = (((0,), (0,)), ((), ()))


def _params(*sem):
    return pltpu.CompilerParams(dimension_semantics=sem, vmem_limit_bytes=VMEM_LIMIT_BYTES)


def _rmsnorm_rows(x, gain):
    ms = jnp.mean(x * x, axis=-1, keepdims=True)
    return x * lax.rsqrt(ms + EPS) * gain


def _rmsnorm_tile_to_bf16(x_ref, gain_ref, h_ref):
    def strip(r, carry):
        rows = pl.ds(pl.multiple_of(r * NORM_STRIP_ROWS, NORM_STRIP_ROWS), NORM_STRIP_ROWS)
        h_ref[rows, :] = _rmsnorm_rows(x_ref[rows, :].astype(F32), gain_ref[...]).astype(BF16)
        return carry

    lax.fori_loop(0, x_ref.shape[0] // NORM_STRIP_ROWS, strip, 0)


def _once(block_shape, index_map):
    return pl.BlockSpec(block_shape, index_map, pipeline_mode=pl.Buffered(1))


def _token_rows_spec(x, tm, room_for_two):
    spec = pl.BlockSpec if room_for_two else _once
    return spec((tm, x.shape[1]), lambda i, j: (i, 0))


def _ffn_up_kernel(x_ref, gain_ref, wg_ref, wu_ref, a_ref, *maybe_h_scr):
    if maybe_h_scr:
        h_scr, = maybe_h_scr

        @pl.when(pl.program_id(1) == 0)
        def _():
            _rmsnorm_tile_to_bf16(x_ref, gain_ref, h_scr)

        h = h_scr[...]
    else:
        h = x_ref[...]
    for c0 in range(0, a_ref.shape[1], MXU_COLS):
        cols = slice(c0, c0 + MXU_COLS)
        g = jnp.dot(h, wg_ref[:, cols].astype(BF16), preferred_element_type=F32)
        u = jnp.dot(h, wu_ref[:, cols].astype(BF16), preferred_element_type=F32)
        a_ref[:, cols] = (g * jax.nn.sigmoid(g) * u).astype(BF16)


def ffn_up(x, gain, wg, wu, l, *, normalized, tm=TM, tn=TN_FFN_UP):
    t, d = x.shape
    f = wg.shape[2]
    return pl.pallas_call(
        _ffn_up_kernel,
        grid=(t // tm, f // tn),
        in_specs=[
            _token_rows_spec(x, tm, room_for_two=normalized),
            pl.BlockSpec((None, 1, d), lambda i, j: (l, 0, 0)),
            pl.BlockSpec((None, d, tn), lambda i, j: (l, 0, j)),
            pl.BlockSpec((None, d, tn), lambda i, j: (l, 0, j)),
        ],
        out_specs=pl.BlockSpec((tm, tn), lambda i, j: (i, j)),
        out_shape=jax.ShapeDtypeStruct((t, f), BF16),
        scratch_shapes=[] if normalized else [pltpu.VMEM((tm, d), BF16)],
        compiler_params=_params("parallel", "arbitrary"),
        name="ffn_up",
    )(x, gain.reshape(gain.shape[0], 1, d), wg, wu)


def _resid_mm_kernel(a_ref, w_ref, x_ref, *rest, scale, second):
    gain_ref, rest = (rest[0], rest[1:]) if second == "norm" else (None, rest)
    o_ref, *second_refs, w_scr = rest

    @pl.when(pl.program_id(1) == 0)
    def _():
        def strip(r, carry):
            rows = pl.ds(pl.multiple_of(r * CAST_STRIP_ROWS, CAST_STRIP_ROWS), CAST_STRIP_ROWS)
            w_scr[rows, :] = w_ref[rows, :].astype(BF16)
            return carry

        lax.fori_loop(0, w_ref.shape[0] // CAST_STRIP_ROWS, strip, 0)

    y = jnp.dot(a_ref[...], w_scr[...], preferred_element_type=F32)
    out = x_ref[...] + (y if scale == 1.0 else scale * y)
    o_ref[...] = out
    if second == "copy":
        second_refs[0][...] = out.astype(BF16)
    elif second == "norm":
        second_refs[0][...] = _rmsnorm_rows(out, gain_ref[...]).astype(BF16)


def resid_mm(a, w, x, l, *, scale, tm, tn, second=None, norm_gain=None):
    t, kdim = a.shape
    n = w.shape[2]
    assert kdim % CAST_STRIP_ROWS == 0 and (second != "norm" or tn == n)
    tile = pl.BlockSpec((tm, tn), lambda j, i: (i, j))
    shape = jax.ShapeDtypeStruct((t, n), F32)
    in_specs = [
        pl.BlockSpec((tm, kdim), lambda j, i: (i, 0)),
        _once((None, kdim, tn), lambda j, i: (l, 0, j)),
        tile,
    ]
    operands = [a, w, x]
    out_specs, out_shape = tile, shape
    if second is not None:
        out_specs, out_shape = [tile, tile], [shape, jax.ShapeDtypeStruct((t, n), BF16)]
    if second == "norm":
        in_specs.append(pl.BlockSpec((None, 1, n), lambda j, i: (l, 0, 0)))
        operands.append(norm_gain.reshape(norm_gain.shape[0], 1, n))
    return pl.pallas_call(
        functools.partial(_resid_mm_kernel, scale=scale, second=second),
        grid=(n // tn, t // tm),
        in_specs=in_specs,
        out_specs=out_specs,
        out_shape=out_shape,
        scratch_shapes=[pltpu.VMEM((kdim, tn), BF16)],
        compiler_params=_params("arbitrary", "arbitrary"),
        name="resid_mm",
    )(*operands)


def _in_proj_kernel(x_ref, gain_ref, w_ref, p_ref, h_scr):
    @pl.when(pl.program_id(1) == 0)
    def _():
        _rmsnorm_tile_to_bf16(x_ref, gain_ref, h_scr)

    p_ref[...] = jnp.dot(h_scr[...], w_ref[...].astype(BF16), preferred_element_type=F32)


def in_proj(x, gain, w, l, *, tm=TM, tn=TN_IN_PROJ):
    t, d = x.shape
    n = w.shape[2]
    return pl.pallas_call(
        _in_proj_kernel,
        grid=(t // tm, n // tn),
        in_specs=[
            _token_rows_spec(x, tm, room_for_two=x.dtype == BF16),
            pl.BlockSpec((None, 1, d), lambda i, j: (l, 0, 0)),
            pl.BlockSpec((None, d, tn), lambda i, j: (l, 0, j)),
        ],
        out_specs=pl.BlockSpec((tm, tn), lambda i, j: (i, j)),
        out_shape=jax.ShapeDtypeStruct((t, n), F32),
        scratch_shapes=[pltpu.VMEM((tm, d), BF16)],
        compiler_params=_params("parallel", "arbitrary"),
        name="in_proj",
    )(x, gain.reshape(gain.shape[0], 1, d), w)


def _block_ref_rows(p, h, r):
    c, n = p.shape
    blk = 2 * h
    if blk >= SUBLANES:
        p3 = p.reshape(c // blk, blk, n)
        return jnp.broadcast_to(p3[:, r:r + 1, :], p3.shape).reshape(c, n)
    pos = lax.broadcasted_iota(jnp.int32, p.shape, 0) & (blk - 1)
    out = p
    for src in range(blk):
        if src == r:
            continue
        shifted = pltpu.roll(p, (src - r) % c, axis=0)
        out = jnp.where(pos == src, shifted, out)
    return out


def _chunk_scores(q, key, a, masks_ref, mask_base, reverse):
    c, n = q.shape
    row = lax.broadcasted_iota(jnp.int32, q.shape, 0)
    p = a
    scores = None
    h = 1
    level = 0
    while h < c:
        if h >= SUBLANES:
            nb = c // (2 * h)
            p4 = p.reshape(nb, 2, h, n)
            q4 = q.reshape(nb, 2, h, n)
            k4 = key.reshape(nb, 2, h, n)
            lo, hi = p4[:, 0], p4[:, 1]
            if reverse:
                t_row = hi[:, 0:1, :]
                qk = jnp.stack([q4[:, 0] * jnp.exp(lo), k4[:, 1] * jnp.exp(t_row - hi)], axis=1)
                p = jnp.stack([lo + t_row, hi], axis=1).reshape(c, n)
            else:
                t_row = lo[:, h - 1:h, :]
                qk = jnp.stack([k4[:, 0] * jnp.exp(t_row - lo), q4[:, 1] * jnp.exp(hi)], axis=1)
                p = jnp.stack([lo, hi + t_row], axis=1).reshape(c, n)
            qk = qk.reshape(c, n).astype(BF16)
        else:
            upper = (row & h) != 0
            qside = jnp.logical_not(upper) if reverse else upper
            t_rows = _block_ref_rows(p, h, h if reverse else h - 1)
            e = jnp.where(qside, p, t_rows - p)
            qk = (jnp.where(qside, q, key) * jnp.exp(e)).astype(BF16)
            p = p + jnp.where(qside, t_rows, 0.0)
        s = lax.dot_general(qk, qk, NT_DIMS, preferred_element_type=F32)
        s = s * masks_ref[mask_base + level]
        scores = s if scores is None else scores + s
        h *= 2
        level += 1
    return scores, p


def _log2_decay_and_key(z, lb, log1m_lb, one_m_lb):
    u = jnp.exp(-jnp.abs(z))
    one_u = 1.0 + u
    r = 1.0 / one_u
    pos = z >= 0.0
    sig = jnp.where(pos, 1.0, u) * r
    key = one_m_lb * (jnp.where(pos, u, 1.0) * r)
    y2 = (log1m_lb + jnp.minimum(z, 0.0)) * LOG2E - jnp.log2(one_u)
    return jnp.maximum(jnp.log2(lb + one_m_lb * sig), y2), key


def _cumsum_rows(a, tri_bf16):
    a1 = a.astype(BF16)
    r1 = a - a1.astype(F32)
    a2 = r1.astype(BF16)
    a3 = (r1 - a2.astype(F32)).astype(BF16)
    out = jnp.dot(tri_bf16, jnp.concatenate([a1, a2, a3], axis=1), preferred_element_type=F32)
    n = a.shape[1]
    return (out[:, 2 * n:] + out[:, n:2 * n]) + out[:, :n]


def _hgrn_kernel(q_ref, v_ref, zf_ref, zb_ref, g_ref, lbf_ref, lbb_ref, gain_ref, masks_ref, tri_ref, o_ref,
                 of_scr, ob_scr, qhf_scr, qhb_scr, updf_scr, updb_scr, decf_scr, decb_scr, stf_scr, stb_scr,
                 cumf_scr, cumb_scr, keyf_scr, keyb_scr, totf_scr, totb_scr,
                 *, chunk, n_chunks, n_levels):
    c = chunk

    def rows_of(i):
        return pl.ds(pl.multiple_of(i * c, c), c)

    dirs = (
        (False, zf_ref, lbf_ref, of_scr, qhf_scr, updf_scr, decf_scr, cumf_scr, keyf_scr, totf_scr),
        (True, zb_ref, lbb_ref, ob_scr, qhb_scr, updb_scr, decb_scr, cumb_scr, keyb_scr, totb_scr),
    )

    def gates(z_ref, lb_ref, rows):
        return _log2_decay_and_key(z_ref[0, rows, :], lb_ref[0, 0:1, :], lb_ref[0, 1:2, :], lb_ref[0, 2:3, :])

    def phase0(i, lowest):
        rows = rows_of(i)
        for reverse, z_ref, lb_ref, _, _, _, dec_scr, cum_scr, key_scr, tot_scr in dirs:
            a2, key = gates(z_ref, lb_ref, rows)
            cum = _cumsum_rows(a2, tri_ref[1 if reverse else 0])
            total = cum[0:1, :] if reverse else cum[c - 1:c, :]
            cum_scr[rows, :] = cum
            key_scr[rows, :] = key
            tot_scr[i] = jnp.broadcast_to(total, (SUBLANES, LANES))
            dec_scr[i] = jnp.broadcast_to(jnp.exp2(total), (SUBLANES, LANES))
            lowest = jnp.minimum(lowest, total)
        return lowest

    lowest = lax.fori_loop(0, n_chunks, phase0, jnp.zeros((1, LANES), F32), unroll=UNROLL_GATES)
    safe = jnp.min(lowest) > -SAFE_LOG2_RANGE

    @pl.when(safe)
    def _():
        ti = lax.broadcasted_iota(jnp.int32, (c, c), 0)
        si = lax.broadcasted_iota(jnp.int32, (c, c), 1)

        def body(i, carry):
            rows = rows_of(i)
            q = q_ref[0, rows, :]
            vb = v_ref[0, rows, :].astype(BF16)
            for reverse, _, _, o_scr, qh_scr, upd_scr, _, cum_scr, key_scr, tot_scr in dirs:
                cum = cum_scr[rows, :]
                half = tot_scr[i][0:1, :] * 0.5
                qt = q * jnp.exp2(cum - half)
                kt = key_scr[rows, :] * jnp.exp2(half - cum)
                s = lax.dot_general(qt.astype(BF16), kt.astype(BF16), NT_DIMS, preferred_element_type=F32)
                s = jnp.where((si >= ti) if reverse else (si <= ti), s, 0.0)
                o_scr[rows, :] = jnp.dot(s.astype(BF16), vb, preferred_element_type=F32)
                edge = jnp.exp2(half)
                qh_scr[rows, :] = (qt * edge).astype(BF16)
                upd_scr[i] = lax.dot_general(vb, (kt * edge).astype(BF16), TN_DIMS,
                                             preferred_element_type=F32)
            return carry

        lax.fori_loop(0, n_chunks, body, 0, unroll=True)

    @pl.when(jnp.logical_not(safe))
    def _():
        def body(i, carry):
            rows = rows_of(i)
            q = q_ref[0, rows, :]
            v = v_ref[0, rows, :]
            vb = v.astype(BF16)
            for reverse, z_ref, lb_ref, o_scr, qh_scr, upd_scr, _, _, _, _ in dirs:
                a2, key = gates(z_ref, lb_ref, rows)
                scores, cum = _chunk_scores(q, key, a2 * LN2, masks_ref, n_levels if reverse else 0, reverse)
                diag = jnp.sum(q * key, axis=-1, keepdims=True)
                o_scr[rows, :] = jnp.dot(scores.astype(BF16), vb, preferred_element_type=F32) + diag * v
                total = cum[0:1, :] if reverse else cum[c - 1:c, :]
                qh_scr[rows, :] = (q * jnp.exp(cum)).astype(BF16)
                kd = (key * jnp.exp(total - cum)).astype(BF16)
                upd_scr[i] = lax.dot_general(vb, kd, TN_DIMS, preferred_element_type=F32)
            return carry

        lax.fori_loop(0, n_chunks, body, 0)

    def phase2(i, carry):
        s_f, s_b = carry
        j = n_chunks - 1 - i
        stf_scr[i] = s_f.astype(BF16)
        stb_scr[j] = s_b.astype(BF16)
        s_f = s_f * decf_scr[i][0:1, :] + updf_scr[i]
        s_b = s_b * decb_scr[j][0:1, :] + updb_scr[j]
        return s_f, s_b

    zero = jnp.zeros((LANES, LANES), F32)
    lax.fori_loop(0, n_chunks, phase2, (zero, zero))

    def phase3(i, carry):
        rows = rows_of(i)
        o = of_scr[rows, :] + ob_scr[rows, :]
        o += lax.dot_general(jnp.concatenate([qhf_scr[rows, :], qhb_scr[rows, :]], axis=1),
                             jnp.concatenate([stf_scr[i], stb_scr[i]], axis=1), NT_DIMS,
                             preferred_element_type=F32)
        o = _rmsnorm_rows(o, gain_ref[0])
        g = g_ref[0, rows, :]
        o_ref[0, rows, :] = (o * (g * jax.nn.sigmoid(g))).astype(o_ref.dtype)
        return carry

    lax.fori_loop(0, n_chunks, phase3, 0, unroll=True)


def _level_masks(chunk):
    idx = jnp.arange(chunk)
    t, s = idx[:, None], idx[None, :]
    fwd, bwd = [], []
    h = 1
    while h < chunk:
        same = (t // (2 * h)) == (s // (2 * h))
        t_up = (t & h) != 0
        s_up = (s & h) != 0
        fwd.append(same & t_up & ~s_up)
        bwd.append(same & ~t_up & s_up)
        h *= 2
    return jnp.stack(fwd + bwd).astype(F32)


def hgrn2_mixer(p, lb_f, lb_b, out_gain, *, heads, chunk=HGRN_CHUNK):
    b, s, _ = p.shape
    n_chunks = s // chunk
    n_levels = int(math.log2(chunk))

    def lb_rows(lb):
        lb = lb.astype(F32).reshape(heads, 1, LANES)
        rows = jnp.concatenate([lb, jnp.log1p(-lb), 1.0 - lb], axis=1)
        return jnp.pad(rows, ((0, 0), (0, SUBLANES - 3), (0, 0)))

    def col(block0):
        return pl.BlockSpec((1, s, LANES), lambda bi, hi: (bi, 0, block0 + hi))

    per_head = pl.BlockSpec((1, SUBLANES, LANES), lambda bi, hi: (hi, 0, 0))
    masks = _level_masks(chunk)
    idx = jnp.arange(chunk)
    lower = idx[None, :] <= idx[:, None]
    tri = jnp.stack([lower, lower.T]).astype(BF16)
    kern = functools.partial(_hgrn_kernel, chunk=chunk, n_chunks=n_chunks, n_levels=n_levels)
    seq_f32 = pltpu.VMEM((s, LANES), F32)
    seq_bf16 = pltpu.VMEM((s, LANES), BF16)
    chunk_mat_f32 = pltpu.VMEM((n_chunks, LANES, LANES), F32)
    chunk_mat_bf16 = pltpu.VMEM((n_chunks, LANES, LANES), BF16)
    chunk_row = pltpu.VMEM((n_chunks, SUBLANES, LANES), F32)
    return pl.pallas_call(
        kern,
        grid=(b, heads),
        in_specs=[
            col(COL_RQ // LANES), col(COL_RI // LANES), col(COL_ZF // LANES), col(COL_ZB // LANES),
            col(COL_RG // LANES), per_head, per_head,
            pl.BlockSpec((1, 1, LANES), lambda bi, hi: (hi, 0, 0)),
            pl.BlockSpec(masks.shape, lambda bi, hi: (0, 0, 0)),
            pl.BlockSpec(tri.shape, lambda bi, hi: (0, 0, 0)),
        ],
        out_specs=pl.BlockSpec((1, s, LANES), lambda bi, hi: (bi, 0, hi)),
        out_shape=jax.ShapeDtypeStruct((b, s, heads * LANES), BF16),
        scratch_shapes=[
            seq_f32, seq_f32, seq_bf16, seq_bf16,
            chunk_mat_f32, chunk_mat_f32, chunk_row, chunk_row,
            chunk_mat_bf16, chunk_mat_bf16,
            seq_f32, seq_f32, seq_f32, seq_f32, chunk_row, chunk_row,
        ],
        compiler_params=_params("parallel", "parallel"),
        name="hgrn2",
    )(p, p, p, p, p, lb_rows(lb_f), lb_rows(lb_b), out_gain.astype(F32).reshape(heads, 1, LANES), masks, tri)


ATT_SUB = 2


def _attn_kernel(sink_ref, q_ref, k0_ref, k1_ref, k2_ref, k3_ref, v0_ref, v1_ref, v2_ref, v3_ref,
                 bias0_ref, bias1_ref, o_ref):
    qscale = LOG2E / math.sqrt(ATT_HD)
    qb = Q_BLOCK
    k_refs = (k0_ref, k1_ref, k2_ref, k3_ref)
    v_refs = (v0_ref, v1_ref, v2_ref, v3_ref)

    def head_cols(ref, head, rows=slice(None)):
        return ref[0, rows, head * ATT_HD:(head + 1) * ATT_HD]

    def band(refs, first, kvh):
        return jnp.concatenate([head_cols(r, kvh) for r in refs[first:first + 3]], axis=0).astype(BF16)

    for sub, bias_ref in enumerate((bias0_ref, bias1_ref)):
        rows = slice(sub * qb, (sub + 1) * qb)
        for kvh in range(ATT_KV_HEADS):
            k = band(k_refs, sub, kvh)
            v1 = jnp.concatenate([band(v_refs, sub, kvh), jnp.ones((3 * qb, ATT_HD), BF16)], axis=1)
            for g in range(ATT_GROUP):
                head = kvh * ATT_GROUP + g
                q = (head_cols(q_ref, head, rows) * qscale).astype(BF16)
                s = lax.dot_general(q, k, NT_DIMS, preferred_element_type=F32)
                s = s + bias_ref[0, head * qb:(head + 1) * qb]
                sink = sink_ref[head] * LOG2E
                m = jnp.broadcast_to(jnp.maximum(jnp.max(s, axis=-1, keepdims=True), sink), (qb, ATT_HD))
                e = jnp.exp2(s - jnp.concatenate([m, m, m], axis=1)).astype(BF16)
                ov = jnp.dot(e, v1, preferred_element_type=F32)
                o = ov[:, :ATT_HD] / (ov[:, ATT_HD:] + jnp.exp2(sink - m))
                o_ref[0, rows, head * ATT_HD:(head + 1) * ATT_HD] = o.astype(o_ref.dtype)


def _attn_bias():
    qi = jnp.arange(Q_BLOCK)[:, None]
    kj = jnp.arange(3 * Q_BLOCK)[None, :]
    dist = jnp.abs(kj - Q_BLOCK - qi)
    slopes = 2.0 ** (-8.0 * jnp.arange(1, ATT_HEADS + 1, dtype=F32) / ATT_HEADS)
    alibi = -(slopes[:, None, None] * dist.astype(F32)[None]) * LOG2E
    cases = []
    for has_prev, has_next in ((False, True), (True, True), (True, False)):
        valid = (dist <= WINDOW) & ((kj >= Q_BLOCK) | has_prev) & ((kj < 2 * Q_BLOCK) | has_next)
        cases.append(jnp.where(valid[None], alibi, -jnp.inf).reshape(ATT_HEADS * Q_BLOCK, 3 * Q_BLOCK))
    return jnp.stack(cases)


def window_gqa(p, sink, bias):
    b, s, _ = p.shape
    nb = s // Q_BLOCK
    assert nb >= 2 and nb % ATT_SUB == 0
    kcol, vcol = COL_AK // KV_WIDTH, COL_AV // KV_WIDTH

    def kv(col, off):
        return pl.BlockSpec((1, Q_BLOCK, KV_WIDTH),
                            lambda bi, n, sk: (bi, jnp.clip(ATT_SUB * n - 1 + off, 0, nb - 1), col))

    def bias_spec(sub):
        def case(bi, n, sk):
            blk = ATT_SUB * n + sub
            return (jnp.where(blk == 0, 0, jnp.where(blk == nb - 1, 2, 1)), 0, 0)
        return pl.BlockSpec((1, ATT_HEADS * Q_BLOCK, 3 * Q_BLOCK), case)

    rows = ATT_SUB * Q_BLOCK
    n_band = ATT_SUB + 2
    return pl.pallas_call(
        _attn_kernel,
        grid_spec=pltpu.PrefetchScalarGridSpec(
            num_scalar_prefetch=1,
            grid=(b, nb // ATT_SUB),
            in_specs=[pl.BlockSpec((1, rows, ATT_WIDTH), lambda bi, n, sk: (bi, n, COL_AQ // ATT_WIDTH))]
            + [kv(kcol, off) for off in range(n_band)]
            + [kv(vcol, off) for off in range(n_band)]
            + [bias_spec(sub) for sub in range(ATT_SUB)],
            out_specs=pl.BlockSpec((1, rows, ATT_WIDTH), lambda bi, n, sk: (bi, n, 0)),
        ),
        out_shape=jax.ShapeDtypeStruct((b, s, ATT_WIDTH), BF16),
        compiler_params=_params("parallel", "arbitrary"),
        name="window_gqa",
    )(sink.astype(F32), *([p] * (1 + 2 * n_band)), *([bias] * ATT_SUB))


def _merge_kernel(oa_ref, ob_ref, wa_ref, wb_ref, ga_ref, gb_ref, m_ref, wa_scr, wb_scr, *, tn):
    @pl.when((pl.program_id(0) == 0) & (pl.program_id(1) == 0))
    def _():
        wa_scr[...] = wa_ref[...].astype(BF16)
        wb_scr[...] = wb_ref[...].astype(BF16)

    cols = pl.ds(pl.multiple_of(pl.program_id(1) * tn, tn), tn)
    ya = jnp.dot(oa_ref[...], wa_scr[:, cols], preferred_element_type=F32)
    yb = jnp.dot(ob_ref[...], wb_scr[:, cols], preferred_element_type=F32)
    m_ref[...] = (jax.nn.sigmoid(ga_ref[...]) * ya + jax.nn.sigmoid(gb_ref[...]) * yb).astype(m_ref.dtype)


def merge_branches(oa, ob, wa, wb, p, l, *, tm=TM_MERGE, tn=TN_MERGE):
    t, ka = oa.shape
    kb = ob.shape[1]
    n = wa.shape[2]
    assert COL_GA % tn == 0 and COL_GB % tn == 0
    return pl.pallas_call(
        functools.partial(_merge_kernel, tn=tn),
        grid=(t // tm, n // tn),
        in_specs=[
            pl.BlockSpec((tm, ka), lambda i, j: (i, 0)),
            pl.BlockSpec((tm, kb), lambda i, j: (i, 0)),
            _once((None, ka, n), lambda i, j: (l, 0, 0)),
            _once((None, kb, n), lambda i, j: (l, 0, 0)),
            pl.BlockSpec((tm, tn), lambda i, j: (i, COL_GA // tn + j)),
            pl.BlockSpec((tm, tn), lambda i, j: (i, COL_GB // tn + j)),
        ],
        out_specs=pl.BlockSpec((tm, tn), lambda i, j: (i, j)),
        out_shape=jax.ShapeDtypeStruct((t, n), BF16),
        scratch_shapes=[pltpu.VMEM((ka, n), BF16), pltpu.VMEM((kb, n), BF16)],
        compiler_params=_params("arbitrary", "arbitrary"),
        name="merge_branches",
    )(oa, ob, wa, wb, p, p)


def _final_norm_kernel(x_ref, gain_ref, o_ref):
    o_ref[...] = _rmsnorm_rows(x_ref[...], gain_ref[...])


def final_rmsnorm(x, gain, *, tm=TM_NORM):
    t, d = x.shape
    return pl.pallas_call(
        _final_norm_kernel,
        grid=(t // tm,),
        in_specs=[pl.BlockSpec((tm, d), lambda i: (i, 0)), pl.BlockSpec((1, d), lambda i: (0, 0))],
        out_specs=pl.BlockSpec((tm, d), lambda i: (i, 0)),
        out_shape=jax.ShapeDtypeStruct((t, d), F32),
        compiler_params=_params("parallel"),
        name="final_norm",
    )(x, gain.reshape(1, d))


def _lower_bounds(lb_logits):
    lb = jnp.cumsum(jax.nn.softmax(lb_logits.astype(F32), axis=0), axis=0)
    return lb - lb[0:1]


def kernel(x, ffn1_norm, ffn1_w_gate, ffn1_w_up, ffn1_w_down, mix_norm, w_in, lb_fwd_logits, lb_bwd_logits, rg_out_norm, attn_sink, w_branch_a, w_branch_b, w_out, ffn2_norm, ffn2_w_gate, ffn2_w_up, ffn2_w_down, final_norm):
    b, s, d = x.shape
    t = b * s
    depth = w_in.shape[0]
    lb_f_all = _lower_bounds(lb_fwd_logits)
    lb_b_all = _lower_bounds(lb_bwd_logits)
    xf = x.reshape(t, d)
    x_norm_in = xf
    attn_bias = _attn_bias()

    def ffn_down(a, xf, wd, l, second):
        return resid_mm(a, wd, xf, l, scale=0.5, tm=TM_DOWN, tn=TN_DOWN, second=second)

    for l in range(depth):
        a = ffn_up(x_norm_in, ffn1_norm, ffn1_w_gate, ffn1_w_up, l, normalized=False)
        xf, x_norm_in = ffn_down(a, xf, ffn1_w_down, l, "copy")
        p = in_proj(x_norm_in, mix_norm, w_in, l)
        p3 = p.reshape(b, s, IN_COLS)
        o_a = hgrn2_mixer(p3, lb_f_all[l], lb_b_all[l], rg_out_norm[l], heads=RG_HEADS)
        o_b = window_gqa(p3, attn_sink[l], attn_bias)
        merged = merge_branches(o_a.reshape(t, RG_WIDTH), o_b.reshape(t, ATT_WIDTH),
                                w_branch_a, w_branch_b, p, l)
        xf, h = resid_mm(merged, w_out, xf, l, scale=1.0, tm=TM_OUT, tn=TN_OUT, second="norm",
                         norm_gain=ffn2_norm)
        a = ffn_up(h, ffn2_norm, ffn2_w_gate, ffn2_w_up, l, normalized=True)
        if l + 1 < depth:
            xf, x_norm_in = ffn_down(a, xf, ffn2_w_down, l, "copy")
        else:
            xf = ffn_down(a, xf, ffn2_w_down, l, None)
    return final_rmsnorm(xf, final_norm).reshape(b, s, d)
```

```python
import functools
import math

import jax
import jax.numpy as jnp
from jax import lax
from jax.experimental import pallas as pl
from jax.experimental.pallas import tpu as pltpu

D_MODEL = 2048
D_FF = 5632
RG_HEADS = 8
RG_DK = 128
RG_DV = 128
RG_WIDTH = RG_HEADS * RG_DV
ATT_HEADS = 8
ATT_KV_HEADS = 2
ATT_GROUP = ATT_HEADS // ATT_KV_HEADS
ATT_HD = 128
ATT_WIDTH = ATT_HEADS * ATT_HD
KV_WIDTH = ATT_KV_HEADS * ATT_HD
WINDOW = 128
Q_BLOCK = 128
IN_COLS = 5 * RG_WIDTH + ATT_WIDTH + 2 * KV_WIDTH + 2 * D_MODEL
EPS = 1e-6

COL_RQ = 0
COL_RI = RG_WIDTH
COL_ZF = 2 * RG_WIDTH
COL_ZB = 3 * RG_WIDTH
COL_RG = 4 * RG_WIDTH
COL_AQ = 5 * RG_WIDTH
COL_AK = COL_AQ + ATT_WIDTH
COL_AV = COL_AK + KV_WIDTH
COL_GA = COL_AV + KV_WIDTH
COL_GB = COL_GA + D_MODEL

LANES = 128
SUBLANES = 8
HGRN_CHUNK = 128
UNROLL_GATES = 16
VMEM_LIMIT_BYTES = 56 * 1024 * 1024
LOG2E = 1.4426950408889634
LN2 = 0.6931471805599453
SAFE_LOG2_RANGE = 240.0

TM = 2048
TM_DOWN = 1024
TM_MERGE = 512
TM_OUT = 512
TN_FFN_UP = 512
TN_IN_PROJ = 512
MERGE_COLS = 512
TN_DOWN = 512
TN_OUT = 2048
TM_NORM = 512
CAST_STRIP_ROWS = 512
NORM_STRIP_ROWS = 256
MXU_COLS = 256

BF16 = jnp.bfloat16
F32 = jnp.float32
NT_DIMS = (((1,), (1,)), ((), ()))
TN_DIMS = (((0,), (0,)), ((), ()))


def _params(*sem):
    return pltpu.CompilerParams(dimension_semantics=sem, vmem_limit_bytes=VMEM_LIMIT_BYTES)


def _rmsnorm_rows(x, gain):
    ms = jnp.mean(x * x, axis=-1, keepdims=True)
    return x * lax.rsqrt(ms + EPS) * gain


def _rmsnorm_tile_to_bf16(x_ref, gain_ref, h_ref):
    def strip(r, carry):
        rows = pl.ds(pl.multiple_of(r * NORM_STRIP_ROWS, NORM_STRIP_ROWS), NORM_STRIP_ROWS)
        h_ref[rows, :] = _rmsnorm_rows(x_ref[rows, :].astype(F32), gain_ref[...]).astype(BF16)
        return carry

    lax.fori_loop(0, x_ref.shape[0] // NORM_STRIP_ROWS, strip, 0)


def _once(block_shape, index_map):
    return pl.BlockSpec(block_shape, index_map, pipeline_mode=pl.Buffered(1))


def _token_rows_spec(x, tm, room_for_two):
    spec = pl.BlockSpec if room_for_two else _once
    return spec((tm, x.shape[1]), lambda i, j: (i, 0))


def _ffn_up_kernel(x_ref, gain_ref, wg_ref, wu_ref, a_ref, *maybe_h_scr):
    if maybe_h_scr:
        h_scr, = maybe_h_scr

        @pl.when(pl.program_id(1) == 0)
        def _():
            _rmsnorm_tile_to_bf16(x_ref, gain_ref, h_scr)

        h = h_scr[...]
    else:
        h = x_ref[...]
    for c0 in range(0, a_ref.shape[1], MXU_COLS):
        cols = slice(c0, c0 + MXU_COLS)
        g = jnp.dot(h, wg_ref[:, cols].astype(BF16), preferred_element_type=F32)
        u = jnp.dot(h, wu_ref[:, cols].astype(BF16), preferred_element_type=F32)
        a_ref[:, cols] = (g * jax.nn.sigmoid(g) * u).astype(BF16)


def ffn_up(x, gain, wg, wu, l, *, normalized, tm=TM, tn=TN_FFN_UP):
    t, d = x.shape
    f = wg.shape[2]
    return pl.pallas_call(
        _ffn_up_kernel,
        grid=(t // tm, f // tn),
        in_specs=[
            _token_rows_spec(x, tm, room_for_two=normalized),
            pl.BlockSpec((None, 1, d), lambda i, j: (l, 0, 0)),
            pl.BlockSpec((None, d, tn), lambda i, j: (l, 0, j)),
            pl.BlockSpec((None, d, tn), lambda i, j: (l, 0, j)),
        ],
        out_specs=pl.BlockSpec((tm, tn), lambda i, j: (i, j)),
        out_shape=jax.ShapeDtypeStruct((t, f), BF16),
        scratch_shapes=[] if normalized else [pltpu.VMEM((tm, d), BF16)],
        compiler_params=_params("parallel", "arbitrary"),
        name="ffn_up",
    )(x, gain.reshape(gain.shape[0], 1, d), wg, wu)


def _resid_mm_kernel(a_ref, w_ref, x_ref, *rest, scale, second):
    gain_ref, rest = (rest[0], rest[1:]) if second == "norm" else (None, rest)
    o_ref, *second_refs, w_scr = rest

    @pl.when(pl.program_id(1) == 0)
    def _():
        def strip(r, carry):
            rows = pl.ds(pl.multiple_of(r * CAST_STRIP_ROWS, CAST_STRIP_ROWS), CAST_STRIP_ROWS)
            w_scr[rows, :] = w_ref[rows, :].astype(BF16)
            return carry

        lax.fori_loop(0, w_ref.shape[0] // CAST_STRIP_ROWS, strip, 0)

    y = jnp.dot(a_ref[...], w_scr[...], preferred_element_type=F32)
    out = x_ref[...] + (y if scale == 1.0 else scale * y)
    o_ref[...] = out
    if second == "copy":
        second_refs[0][...] = out.astype(BF16)
    elif second == "norm":
        second_refs[0][...] = _rmsnorm_rows(out, gain_ref[...]).astype(BF16)


def resid_mm(a, w, x, l, *, scale, tm, tn, second=None, norm_gain=None):
    t, kdim = a.shape
    n = w.shape[2]
    assert kdim % CAST_STRIP_ROWS == 0 and (second != "norm" or tn == n)
    tile = pl.BlockSpec((tm, tn), lambda j, i: (i, j))
    shape = jax.ShapeDtypeStruct((t, n), F32)
    in_specs = [
        pl.BlockSpec((tm, kdim), lambda j, i: (i, 0)),
        _once((None, kdim, tn), lambda j, i: (l, 0, j)),
        tile,
    ]
    operands = [a, w, x]
    out_specs, out_shape = tile, shape
    if second is not None:
        out_specs, out_shape = [tile, tile], [shape, jax.ShapeDtypeStruct((t, n), BF16)]
    if second == "norm":
        in_specs.append(pl.BlockSpec((None, 1, n), lambda j, i: (l, 0, 0)))
        operands.append(norm_gain.reshape(norm_gain.shape[0], 1, n))
    return pl.pallas_call(
        functools.partial(_resid_mm_kernel, scale=scale, second=second),
        grid=(n // tn, t // tm),
        in_specs=in_specs,
        out_specs=out_specs,
        out_shape=out_shape,
        scratch_shapes=[pltpu.VMEM((kdim, tn), BF16)],
        compiler_params=_params("arbitrary", "arbitrary"),
        name="resid_mm",
    )(*operands)


def _in_proj_kernel(x_ref, gain_ref, w_ref, p_ref, gates_ref, h_scr, *, n_mixer_tiles):
    j = pl.program_id(1)

    @pl.when(j == 0)
    def _():
        _rmsnorm_tile_to_bf16(x_ref, gain_ref, h_scr)

    y = jnp.dot(h_scr[...], w_ref[...].astype(BF16), preferred_element_type=F32)

    @pl.when(j < n_mixer_tiles)
    def _():
        p_ref[...] = y

    @pl.when(j >= n_mixer_tiles)
    def _():
        gates_ref[...] = y.astype(BF16)


def in_proj(x, gain, w, l, *, tm=TM, tn=TN_IN_PROJ):
    t, d = x.shape
    n = w.shape[2]
    assert COL_GA % tn == 0 and n % tn == 0
    n_mixer_tiles = COL_GA // tn
    return pl.pallas_call(
        functools.partial(_in_proj_kernel, n_mixer_tiles=n_mixer_tiles),
        grid=(t // tm, n // tn),
        in_specs=[
            _token_rows_spec(x, tm, room_for_two=x.dtype == BF16),
            pl.BlockSpec((None, 1, d), lambda i, j: (l, 0, 0)),
            pl.BlockSpec((None, d, tn), lambda i, j: (l, 0, j)),
        ],
        out_specs=[
            pl.BlockSpec((tm, tn), lambda i, j: (i, jnp.minimum(j, n_mixer_tiles - 1))),
            pl.BlockSpec((tm, tn), lambda i, j: (i, jnp.maximum(j - n_mixer_tiles, 0))),
        ],
        out_shape=[jax.ShapeDtypeStruct((t, COL_GA), F32), jax.ShapeDtypeStruct((t, n - COL_GA), BF16)],
        scratch_shapes=[pltpu.VMEM((tm, d), BF16)],
        compiler_params=_params("parallel", "arbitrary"),
        name="in_proj",
    )(x, gain.reshape(gain.shape[0], 1, d), w)


def _block_ref_rows(p, h, r):
    c, n = p.shape
    blk = 2 * h
    if blk >= SUBLANES:
        p3 = p.reshape(c // blk, blk, n)
        return jnp.broadcast_to(p3[:, r:r + 1, :], p3.shape).reshape(c, n)
    pos = lax.broadcasted_iota(jnp.int32, p.shape, 0) & (blk - 1)
    out = p
    for src in range(blk):
        if src == r:
            continue
        shifted = pltpu.roll(p, (src - r) % c, axis=0)
        out = jnp.where(pos == src, shifted, out)
    return out


def _chunk_scores(q, key, a, masks_ref, mask_base, reverse):
    c, n = q.shape
    row = lax.broadcasted_iota(jnp.int32, q.shape, 0)
    p = a
    scores = None
    h = 1
    level = 0
    while h < c:
        if h >= SUBLANES:
            nb = c // (2 * h)
            p4 = p.reshape(nb, 2, h, n)
            q4 = q.reshape(nb, 2, h, n)
            k4 = key.reshape(nb, 2, h, n)
            lo, hi = p4[:, 0], p4[:, 1]
            if reverse:
                t_row = hi[:, 0:1, :]
                qk = jnp.stack([q4[:, 0] * jnp.exp(lo), k4[:, 1] * jnp.exp(t_row - hi)], axis=1)
                p = jnp.stack([lo + t_row, hi], axis=1).reshape(c, n)
            else:
                t_row = lo[:, h - 1:h, :]
                qk = jnp.stack([k4[:, 0] * jnp.exp(t_row - lo), q4[:, 1] * jnp.exp(hi)], axis=1)
                p = jnp.stack([lo, hi + t_row], axis=1).reshape(c, n)
            qk = qk.reshape(c, n).astype(BF16)
        else:
            upper = (row & h) != 0
            qside = jnp.logical_not(upper) if reverse else upper
            t_rows = _block_ref_rows(p, h, h if reverse else h - 1)
            e = jnp.where(qside, p, t_rows - p)
            qk = (jnp.where(qside, q, key) * jnp.exp(e)).astype(BF16)
            p = p + jnp.where(qside, t_rows, 0.0)
        s = lax.dot_general(qk, qk, NT_DIMS, preferred_element_type=F32)
        s = s * masks_ref[mask_base + level]
        scores = s if scores is None else scores + s
        h *= 2
        level += 1
    return scores, p


def _log2_decay_and_key(z, lb, log1m_lb, one_m_lb):
    u = jnp.exp(-jnp.abs(z))
    one_u = 1.0 + u
    r = 1.0 / one_u
    pos = z >= 0.0
    sig = jnp.where(pos, 1.0, u) * r
    key = one_m_lb * (jnp.where(pos, u, 1.0) * r)
    y2 = (log1m_lb + jnp.minimum(z, 0.0)) * LOG2E - jnp.log2(one_u)
    return jnp.maximum(jnp.log2(lb + one_m_lb * sig), y2), key


def _cumsum_rows(a, tri_bf16):
    a1 = a.astype(BF16)
    r1 = a - a1.astype(F32)
    a2 = r1.astype(BF16)
    a3 = (r1 - a2.astype(F32)).astype(BF16)
    out = jnp.dot(tri_bf16, jnp.concatenate([a1, a2, a3], axis=1), preferred_element_type=F32)
    n = a.shape[1]
    return (out[:, 2 * n:] + out[:, n:2 * n]) + out[:, :n]


def _hgrn_kernel(q_ref, v_ref, zf_ref, zb_ref, g_ref, lbf_ref, lbb_ref, gain_ref, masks_ref, tri_ref, o_ref,
                 of_scr, ob_scr, qhf_scr, qhb_scr, updf_scr, updb_scr, decf_scr, decb_scr, stf_scr, stb_scr,
                 cumf_scr, cumb_scr, keyf_scr, keyb_scr, totf_scr, totb_scr,
                 *, chunk, n_chunks, n_levels):
    c = chunk

    def rows_of(i):
        return pl.ds(pl.multiple_of(i * c, c), c)

    dirs = (
        (False, zf_ref, lbf_ref, of_scr, qhf_scr, updf_scr, decf_scr, cumf_scr, keyf_scr, totf_scr),
        (True, zb_ref, lbb_ref, ob_scr, qhb_scr, updb_scr, decb_scr, cumb_scr, keyb_scr, totb_scr),
    )

    def gates(z_ref, lb_ref, rows):
        return _log2_decay_and_key(z_ref[0, rows, :], lb_ref[0, 0:1, :], lb_ref[0, 1:2, :], lb_ref[0, 2:3, :])

    def phase0(i, lowest):
        rows = rows_of(i)
        for reverse, z_ref, lb_ref, _, _, _, dec_scr, cum_scr, key_scr, tot_scr in dirs:
            a2, key = gates(z_ref, lb_ref, rows)
            cum = _cumsum_rows(a2, tri_ref[1 if reverse else 0])
            total = cum[0:1, :] if reverse else cum[c - 1:c, :]
            cum_scr[rows, :] = cum
            key_scr[rows, :] = key
            tot_scr[i] = jnp.broadcast_to(total, (SUBLANES, LANES))
            dec_scr[i] = jnp.broadcast_to(jnp.exp2(total), (SUBLANES, LANES))
            lowest = jnp.minimum(lowest, total)
        return lowest

    lowest = lax.fori_loop(0, n_chunks, phase0, jnp.zeros((1, LANES), F32), unroll=UNROLL_GATES)
    safe = jnp.min(lowest) > -SAFE_LOG2_RANGE

    @pl.when(safe)
    def _():
        ti = lax.broadcasted_iota(jnp.int32, (c, c), 0)
        si = lax.broadcasted_iota(jnp.int32, (c, c), 1)

        def body(i, carry):
            rows = rows_of(i)
            q = q_ref[0, rows, :]
            vb = v_ref[0, rows, :].astype(BF16)
            for reverse, _, _, o_scr, qh_scr, upd_scr, _, cum_scr, key_scr, tot_scr in dirs:
                cum = cum_scr[rows, :]
                half = tot_scr[i][0:1, :] * 0.5
                qt = q * jnp.exp2(cum - half)
                kt = key_scr[rows, :] * jnp.exp2(half - cum)
                s = lax.dot_general(qt.astype(BF16), kt.astype(BF16), NT_DIMS, preferred_element_type=F32)
                s = jnp.where((si >= ti) if reverse else (si <= ti), s, 0.0)
                o_scr[rows, :] = jnp.dot(s.astype(BF16), vb, preferred_element_type=F32)
                edge = jnp.exp2(half)
                qh_scr[rows, :] = (qt * edge).astype(BF16)
                upd_scr[i] = lax.dot_general(vb, (kt * edge).astype(BF16), TN_DIMS,
                                             preferred_element_type=F32)
            return carry

        lax.fori_loop(0, n_chunks, body, 0, unroll=True)

    @pl.when(jnp.logical_not(safe))
    def _():
        def body(i, carry):
            rows = rows_of(i)
            q = q_ref[0, rows, :]
            v = v_ref[0, rows, :]
            vb = v.astype(BF16)
            for reverse, z_ref, lb_ref, o_scr, qh_scr, upd_scr, _, _, _, _ in dirs:
                a2, key = gates(z_ref, lb_ref, rows)
                scores, cum = _chunk_scores(q, key, a2 * LN2, masks_ref, n_levels if reverse else 0, reverse)
                diag = jnp.sum(q * key, axis=-1, keepdims=True)
                o_scr[rows, :] = jnp.dot(scores.astype(BF16), vb, preferred_element_type=F32) + diag * v
                total = cum[0:1, :] if reverse else cum[c - 1:c, :]
                qh_scr[rows, :] = (q * jnp.exp(cum)).astype(BF16)
                kd = (key * jnp.exp(total - cum)).astype(BF16)
                upd_scr[i] = lax.dot_general(vb, kd, TN_DIMS, preferred_element_type=F32)
            return carry

        lax.fori_loop(0, n_chunks, body, 0)

    def phase2(i, carry):
        s_f, s_b = carry
        j = n_chunks - 1 - i
        stf_scr[i] = s_f.astype(BF16)
        stb_scr[j] = s_b.astype(BF16)
        s_f = s_f * decf_scr[i][0:1, :] + updf_scr[i]
        s_b = s_b * decb_scr[j][0:1, :] + updb_scr[j]
        return s_f, s_b

    zero = jnp.zeros((LANES, LANES), F32)
    lax.fori_loop(0, n_chunks, phase2, (zero, zero))

    def phase3(i, carry):
        rows = rows_of(i)
        o = of_scr[rows, :] + ob_scr[rows, :]
        o += lax.dot_general(jnp.concatenate([qhf_scr[rows, :], qhb_scr[rows, :]], axis=1),
                             jnp.concatenate([stf_scr[i], stb_scr[i]], axis=1), NT_DIMS,
                             preferred_element_type=F32)
        o = _rmsnorm_rows(o, gain_ref[0])
        g = g_ref[0, rows, :]
        o_ref[0, rows, :] = (o * (g * jax.nn.sigmoid(g))).astype(o_ref.dtype)
        return carry

    lax.fori_loop(0, n_chunks, phase3, 0, unroll=True)


def _level_masks(chunk):
    idx = jnp.arange(chunk)
    t, s = idx[:, None], idx[None, :]
    fwd, bwd = [], []
    h = 1
    while h < chunk:
        same = (t // (2 * h)) == (s // (2 * h))
        t_up = (t & h) != 0
        s_up = (s & h) != 0
        fwd.append(same & t_up & ~s_up)
        bwd.append(same & ~t_up & s_up)
        h *= 2
    return jnp.stack(fwd + bwd).astype(F32)


def hgrn2_mixer(p, lb_f, lb_b, out_gain, *, heads, chunk=HGRN_CHUNK):
    b, s, _ = p.shape
    n_chunks = s // chunk
    n_levels = int(math.log2(chunk))

    def lb_rows(lb):
        lb = lb.astype(F32).reshape(heads, 1, LANES)
        rows = jnp.concatenate([lb, jnp.log1p(-lb), 1.0 - lb], axis=1)
        return jnp.pad(rows, ((0, 0), (0, SUBLANES - 3), (0, 0)))

    def col(block0):
        return pl.BlockSpec((1, s, LANES), lambda bi, hi: (bi, 0, block0 + hi))

    per_head = pl.BlockSpec((1, SUBLANES, LANES), lambda bi, hi: (hi, 0, 0))
    masks = _level_masks(chunk)
    idx = jnp.arange(chunk)
    lower = idx[None, :] <= idx[:, None]
    tri = jnp.stack([lower, lower.T]).astype(BF16)
    kern = functools.partial(_hgrn_kernel, chunk=chunk, n_chunks=n_chunks, n_levels=n_levels)
    seq_f32 = pltpu.VMEM((s, LANES), F32)
    seq_bf16 = pltpu.VMEM((s, LANES), BF16)
    chunk_mat_f32 = pltpu.VMEM((n_chunks, LANES, LANES), F32)
    chunk_mat_bf16 = pltpu.VMEM((n_chunks, LANES, LANES), BF16)
    chunk_row = pltpu.VMEM((n_chunks, SUBLANES, LANES), F32)
    return pl.pallas_call(
        kern,
        grid=(b, heads),
        in_specs=[
            col(COL_RQ // LANES), col(COL_RI // LANES), col(COL_ZF // LANES), col(COL_ZB // LANES),
            col(COL_RG // LANES), per_head, per_head,
            pl.BlockSpec((1, 1, LANES), lambda bi, hi: (hi, 0, 0)),
            pl.BlockSpec(masks.shape, lambda bi, hi: (0, 0, 0)),
            pl.BlockSpec(tri.shape, lambda bi, hi: (0, 0, 0)),
        ],
        out_specs=pl.BlockSpec((1, s, LANES), lambda bi, hi: (bi, 0, hi)),
        out_shape=jax.ShapeDtypeStruct((b, s, heads * LANES), BF16),
        scratch_shapes=[
            seq_f32, seq_f32, seq_bf16, seq_bf16,
            chunk_mat_f32, chunk_mat_f32, chunk_row, chunk_row,
            chunk_mat_bf16, chunk_mat_bf16,
            seq_f32, seq_f32, seq_f32, seq_f32, chunk_row, chunk_row,
        ],
        compiler_params=_params("parallel", "parallel"),
        name="hgrn2",
    )(p, p, p, p, p, lb_rows(lb_f), lb_rows(lb_b), out_gain.astype(F32).reshape(heads, 1, LANES), masks, tri)


ATT_SUB = 2


def _attn_kernel(sink_ref, q_ref, k0_ref, k1_ref, k2_ref, k3_ref, v0_ref, v1_ref, v2_ref, v3_ref,
                 bias0_ref, bias1_ref, o_ref):
    qscale = LOG2E / math.sqrt(ATT_HD)
    qb = Q_BLOCK
    k_refs = (k0_ref, k1_ref, k2_ref, k3_ref)
    v_refs = (v0_ref, v1_ref, v2_ref, v3_ref)

    def head_cols(ref, head, rows=slice(None)):
        return ref[0, rows, head * ATT_HD:(head + 1) * ATT_HD]

    def band(refs, first, kvh):
        return jnp.concatenate([head_cols(r, kvh) for r in refs[first:first + 3]], axis=0).astype(BF16)

    for sub, bias_ref in enumerate((bias0_ref, bias1_ref)):
        rows = slice(sub * qb, (sub + 1) * qb)
        for kvh in range(ATT_KV_HEADS):
            k = band(k_refs, sub, kvh)
            v1 = jnp.concatenate([band(v_refs, sub, kvh), jnp.ones((3 * qb, ATT_HD), BF16)], axis=1)
            for g in range(ATT_GROUP):
                head = kvh * ATT_GROUP + g
                q = (head_cols(q_ref, head, rows) * qscale).astype(BF16)
                s = lax.dot_general(q, k, NT_DIMS, preferred_element_type=F32)
                s = s + bias_ref[0, head * qb:(head + 1) * qb]
                sink = sink_ref[head] * LOG2E
                m = jnp.broadcast_to(jnp.maximum(jnp.max(s, axis=-1, keepdims=True), sink), (qb, ATT_HD))
                e = jnp.exp2(s - jnp.concatenate([m, m, m], axis=1)).astype(BF16)
                ov = jnp.dot(e, v1, preferred_element_type=F32)
                o = ov[:, :ATT_HD] / (ov[:, ATT_HD:] + jnp.exp2(sink - m))
                o_ref[0, rows, head * ATT_HD:(head + 1) * ATT_HD] = o.astype(o_ref.dtype)


def _attn_bias():
    qi = jnp.arange(Q_BLOCK)[:, None]
    kj = jnp.arange(3 * Q_BLOCK)[None, :]
    dist = jnp.abs(kj - Q_BLOCK - qi)
    slopes = 2.0 ** (-8.0 * jnp.arange(1, ATT_HEADS + 1, dtype=F32) / ATT_HEADS)
    alibi = -(slopes[:, None, None] * dist.astype(F32)[None]) * LOG2E
    cases = []
    for has_prev, has_next in ((False, True), (True, True), (True, False)):
        valid = (dist <= WINDOW) & ((kj >= Q_BLOCK) | has_prev) & ((kj < 2 * Q_BLOCK) | has_next)
        cases.append(jnp.where(valid[None], alibi, -jnp.inf).reshape(ATT_HEADS * Q_BLOCK, 3 * Q_BLOCK))
    return jnp.stack(cases)


def window_gqa(p, sink, bias):
    b, s, _ = p.shape
    nb = s // Q_BLOCK
    assert nb >= 2 and nb % ATT_SUB == 0
    kcol, vcol = COL_AK // KV_WIDTH, COL_AV // KV_WIDTH

    def kv(col, off):
        return pl.BlockSpec((1, Q_BLOCK, KV_WIDTH),
                            lambda bi, n, sk: (bi, jnp.clip(ATT_SUB * n - 1 + off, 0, nb - 1), col))

    def bias_spec(sub):
        def case(bi, n, sk):
            blk = ATT_SUB * n + sub
            return (jnp.where(blk == 0, 0, jnp.where(blk == nb - 1, 2, 1)), 0, 0)
        return pl.BlockSpec((1, ATT_HEADS * Q_BLOCK, 3 * Q_BLOCK), case)

    rows = ATT_SUB * Q_BLOCK
    n_band = ATT_SUB + 2
    return pl.pallas_call(
        _attn_kernel,
        grid_spec=pltpu.PrefetchScalarGridSpec(
            num_scalar_prefetch=1,
            grid=(b, nb // ATT_SUB),
            in_specs=[pl.BlockSpec((1, rows, ATT_WIDTH), lambda bi, n, sk: (bi, n, COL_AQ // ATT_WIDTH))]
            + [kv(kcol, off) for off in range(n_band)]
            + [kv(vcol, off) for off in range(n_band)]
            + [bias_spec(sub) for sub in range(ATT_SUB)],
            out_specs=pl.BlockSpec((1, rows, ATT_WIDTH), lambda bi, n, sk: (bi, n, 0)),
        ),
        out_shape=jax.ShapeDtypeStruct((b, s, ATT_WIDTH), BF16),
        compiler_params=_params("parallel", "arbitrary"),
        name="window_gqa",
    )(sink.astype(F32), *([p] * (1 + 2 * n_band)), *([bias] * ATT_SUB))


def _merge_kernel(oa_ref, ob_ref, wa_ref, wb_ref, ga_ref, gb_ref, m_ref, wa_scr, wb_scr):
    @pl.when(pl.program_id(0) == 0)
    def _():
        wa_scr[...] = wa_ref[...].astype(BF16)
        wb_scr[...] = wb_ref[...].astype(BF16)

    oa = oa_ref[...]
    ob = ob_ref[...]
    for c0 in range(0, m_ref.shape[1], MERGE_COLS):
        cols = slice(c0, c0 + MERGE_COLS)
        ya = jnp.dot(oa, wa_scr[:, cols], preferred_element_type=F32)
        yb = jnp.dot(ob, wb_scr[:, cols], preferred_element_type=F32)
        ga = jax.nn.sigmoid(ga_ref[:, cols].astype(F32))
        gb = jax.nn.sigmoid(gb_ref[:, cols].astype(F32))
        m_ref[:, cols] = (ga * ya + gb * yb).astype(m_ref.dtype)


def merge_branches(oa, ob, wa, wb, gates, l, *, tm=TM_MERGE):
    t, ka = oa.shape
    kb = ob.shape[1]
    n = wa.shape[2]
    return pl.pallas_call(
        _merge_kernel,
        grid=(t // tm,),
        in_specs=[
            pl.BlockSpec((tm, ka), lambda i: (i, 0)),
            pl.BlockSpec((tm, kb), lambda i: (i, 0)),
            _once((None, ka, n), lambda i: (l, 0, 0)),
            _once((None, kb, n), lambda i: (l, 0, 0)),
            pl.BlockSpec((tm, n), lambda i: (i, 0)),
            pl.BlockSpec((tm, n), lambda i: (i, 1)),
        ],
        out_specs=pl.BlockSpec((tm, n), lambda i: (i, 0)),
        out_shape=jax.ShapeDtypeStruct((t, n), BF16),
        scratch_shapes=[pltpu.VMEM((ka, n), BF16), pltpu.VMEM((kb, n), BF16)],
        compiler_params=_params("arbitrary"),
        name="merge_branches",
    )(oa, ob, wa, wb, gates, gates)


def _final_norm_kernel(x_ref, gain_ref, o_ref):
    o_ref[...] = _rmsnorm_rows(x_ref[...], gain_ref[...])


def final_rmsnorm(x, gain, *, tm=TM_NORM):
    t, d = x.shape
    return pl.pallas_call(
        _final_norm_kernel,
        grid=(t // tm,),
        in_specs=[pl.BlockSpec((tm, d), lambda i: (i, 0)), pl.BlockSpec((1, d), lambda i: (0, 0))],
        out_specs=pl.BlockSpec((tm, d), lambda i: (i, 0)),
        out_shape=jax.ShapeDtypeStruct((t, d), F32),
        compiler_params=_params("parallel"),
        name="final_norm",
    )(x, gain.reshape(1, d))


def _lower_bounds(lb_logits):
    lb = jnp.cumsum(jax.nn.softmax(lb_logits.astype(F32), axis=0), axis=0)
    return lb - lb[0:1]


def kernel(x, ffn1_norm, ffn1_w_gate, ffn1_w_up, ffn1_w_down, mix_norm, w_in, lb_fwd_logits, lb_bwd_logits, rg_out_norm, attn_sink, w_branch_a, w_branch_b, w_out, ffn2_norm, ffn2_w_gate, ffn2_w_up, ffn2_w_down, final_norm):
    b, s, d = x.shape
    t = b * s
    depth = w_in.shape[0]
    lb_f_all = _lower_bounds(lb_fwd_logits)
    lb_b_all = _lower_bounds(lb_bwd_logits)
    xf = x.reshape(t, d)
    x_norm_in = xf
    attn_bias = _attn_bias()

    def ffn_down(a, xf, wd, l, second):
        return resid_mm(a, wd, xf, l, scale=0.5, tm=TM_DOWN, tn=TN_DOWN, second=second)

    for l in range(depth):
        a = ffn_up(x_norm_in, ffn1_norm, ffn1_w_gate, ffn1_w_up, l, normalized=False)
        xf, x_norm_in = ffn_down(a, xf, ffn1_w_down, l, "copy")
        p, gates = in_proj(x_norm_in, mix_norm, w_in, l)
        p3 = p.reshape(b, s, COL_GA)
        o_a = hgrn2_mixer(p3, lb_f_all[l], lb_b_all[l], rg_out_norm[l], heads=RG_HEADS)
        o_b = window_gqa(p3, attn_sink[l], attn_bias)
        merged = merge_branches(o_a.reshape(t, RG_WIDTH), o_b.reshape(t, ATT_WIDTH),
                                w_branch_a, w_branch_b, gates, l)
        xf, h = resid_mm(merged, w_out, xf, l, scale=1.0, tm=TM_OUT, tn=TN_OUT, second="norm",
                         norm_gain=ffn2_norm)
        a = ffn_up(h, ffn2_norm, ffn2_w_gate, ffn2_w_up, l, normalized=True)
        if l + 1 < depth:
            xf, x_norm_in = ffn_down(a, xf, ffn2_w_down, l, "copy")
        else:
            xf = ffn_down(a, xf, ffn2_w_down, l, None)
    return final_rmsnorm(xf, final_norm).reshape(b, s, d)
```

```python
import functools
import math

import jax
import jax.numpy as jnp
from jax import lax
from jax.experimental import pallas as pl
from jax.experimental.pallas import tpu as pltpu

D_MODEL = 2048
D_FF = 5632
RG_HEADS = 8
RG_DK = 128
RG_DV = 128
RG_WIDTH = RG_HEADS * RG_DV
ATT_HEADS = 8
ATT_KV_HEADS = 2
ATT_GROUP = ATT_HEADS // ATT_KV_HEADS
ATT_HD = 128
ATT_WIDTH = ATT_HEADS * ATT_HD
KV_WIDTH = ATT_KV_HEADS * ATT_HD
WINDOW = 128
Q_BLOCK = 128
IN_COLS = 5 * RG_WIDTH + ATT_WIDTH + 2 * KV_WIDTH + 2 * D_MODEL
EPS = 1e-6

COL_RQ = 0
COL_RI = RG_WIDTH
COL_ZF = 2 * RG_WIDTH
COL_ZB = 3 * RG_WIDTH
COL_RG = 4 * RG_WIDTH
COL_AQ = 5 * RG_WIDTH
COL_AK = COL_AQ + ATT_WIDTH
COL_AV = COL_AK + KV_WIDTH
COL_GA = COL_AV + KV_WIDTH
COL_GB = COL_GA + D_MODEL

LANES = 128
SUBLANES = 8
HGRN_CHUNK = 128
UNROLL_GATES = 16
VMEM_LIMIT_BYTES = 56 * 1024 * 1024
LOG2E = 1.4426950408889634
LN2 = 0.6931471805599453
SAFE_LOG2_RANGE = 240.0

TM = 2048
TM_DOWN = 1024
TM_MERGE = 512
TM_OUT = 512
TN_FFN_UP = 512
TN_IN_PROJ = 512
MERGE_COLS = 512
TN_DOWN = 512
TN_OUT = 2048
TM_NORM = 512
CAST_STRIP_ROWS = 512
NORM_STRIP_ROWS = 256
MXU_COLS = 256

BF16 = jnp.bfloat16
F32 = jnp.float32
NT_DIMS = (((1,), (1,)), ((), ()))
TN_DIMS = (((0,), (0,)), ((), ()))


def _params(*sem):
    return pltpu.CompilerParams(dimension_semantics=sem, vmem_limit_bytes=VMEM_LIMIT_BYTES)


def _rmsnorm_rows(x, gain):
    ms = jnp.mean(x * x, axis=-1, keepdims=True)
    return x * lax.rsqrt(ms + EPS) * gain


def _rmsnorm_tile_to_bf16(x_ref, gain_ref, h_ref):
    def strip(r, carry):
        rows = pl.ds(pl.multiple_of(r * NORM_STRIP_ROWS, NORM_STRIP_ROWS), NORM_STRIP_ROWS)
        h_ref[rows, :] = _rmsnorm_rows(x_ref[rows, :].astype(F32), gain_ref[...]).astype(BF16)
        return carry

    lax.fori_loop(0, x_ref.shape[0] // NORM_STRIP_ROWS, strip, 0)


def _once(block_shape, index_map):
    return pl.BlockSpec(block_shape, index_map, pipeline_mode=pl.Buffered(1))


def _token_rows_spec(x, tm, room_for_two):
    spec = pl.BlockSpec if room_for_two else _once
    return spec((tm, x.shape[1]), lambda i, j: (i, 0))


def _ffn_up_kernel(x_ref, gain_ref, wg_ref, wu_ref, a_ref, *maybe_h_scr):
    if maybe_h_scr:
        h_scr, = maybe_h_scr

        @pl.when(pl.program_id(1) == 0)
        def _():
            _rmsnorm_tile_to_bf16(x_ref, gain_ref, h_scr)

        h = h_scr[...]
    else:
        h = x_ref[...]
    for c0 in range(0, a_ref.shape[1], MXU_COLS):
        cols = slice(c0, c0 + MXU_COLS)
        g = jnp.dot(h, wg_ref[:, cols].astype(BF16), preferred_element_type=F32)
        u = jnp.dot(h, wu_ref[:, cols].astype(BF16), preferred_element_type=F32)
        a_ref[:, cols] = (g * jax.nn.sigmoid(g) * u).astype(BF16)


def ffn_up(x, gain, wg, wu, l, *, normalized, tm=TM, tn=TN_FFN_UP):
    t, d = x.shape
    f = wg.shape[2]
    return pl.pallas_call(
        _ffn_up_kernel,
        grid=(t // tm, f // tn),
        in_specs=[
            _token_rows_spec(x, tm, room_for_two=normalized),
            pl.BlockSpec((None, 1, d), lambda i, j: (l, 0, 0)),
            pl.BlockSpec((None, d, tn), lambda i, j: (l, 0, j)),
            pl.BlockSpec((None, d, tn), lambda i, j: (l, 0, j)),
        ],
        out_specs=pl.BlockSpec((tm, tn), lambda i, j: (i, j)),
        out_shape=jax.ShapeDtypeStruct((t, f), BF16),
        scratch_shapes=[] if normalized else [pltpu.VMEM((tm, d), BF16)],
        compiler_params=_params("parallel", "arbitrary"),
        name="ffn_up",
    )(x, gain.reshape(gain.shape[0], 1, d), wg, wu)


def _resid_mm_kernel(a_ref, w_ref, x_ref, *rest, scale, second):
    gain_ref, rest = (rest[0], rest[1:]) if second == "norm" else (None, rest)
    o_ref, *second_refs, w_scr = rest

    @pl.when(pl.program_id(1) == 0)
    def _():
        def strip(r, carry):
            rows = pl.ds(pl.multiple_of(r * CAST_STRIP_ROWS, CAST_STRIP_ROWS), CAST_STRIP_ROWS)
            w_scr[rows, :] = w_ref[rows, :].astype(BF16)
            return carry

        lax.fori_loop(0, w_ref.shape[0] // CAST_STRIP_ROWS, strip, 0)

    y = jnp.dot(a_ref[...], w_scr[...], preferred_element_type=F32)
    out = x_ref[...] + (y if scale == 1.0 else scale * y)
    o_ref[...] = out
    if second == "copy":
        second_refs[0][...] = out.astype(BF16)
    elif second == "norm":
        second_refs[0][...] = _rmsnorm_rows(out, gain_ref[...]).astype(BF16)


def resid_mm(a, w, x, l, *, scale, tm, tn, second=None, norm_gain=None):
    t, kdim = a.shape
    n = w.shape[2]
    assert kdim % CAST_STRIP_ROWS == 0 and (second != "norm" or tn == n)
    tile = pl.BlockSpec((tm, tn), lambda j, i: (i, j))
    shape = jax.ShapeDtypeStruct((t, n), F32)
    in_specs = [
        pl.BlockSpec((tm, kdim), lambda j, i: (i, 0)),
        _once((None, kdim, tn), lambda j, i: (l, 0, j)),
        tile,
    ]
    operands = [a, w, x]
    out_specs, out_shape = tile, shape
    if second is not None:
        out_specs, out_shape = [tile, tile], [shape, jax.ShapeDtypeStruct((t, n), BF16)]
    if second == "norm":
        in_specs.append(pl.BlockSpec((None, 1, n), lambda j, i: (l, 0, 0)))
        operands.append(norm_gain.reshape(norm_gain.shape[0], 1, n))
    return pl.pallas_call(
        functools.partial(_resid_mm_kernel, scale=scale, second=second),
        grid=(n // tn, t // tm),
        in_specs=in_specs,
        out_specs=out_specs,
        out_shape=out_shape,
        scratch_shapes=[pltpu.VMEM((kdim, tn), BF16)],
        compiler_params=_params("arbitrary", "arbitrary"),
        name="resid_mm",
    )(*operands)


def _in_proj_kernel(x_ref, gain_ref, w_ref, p_ref, gates_ref, h_scr, *, n_gate_tiles):
    j = pl.program_id(1)

    @pl.when(j == 0)
    def _():
        _rmsnorm_tile_to_bf16(x_ref, gain_ref, h_scr)

    y = jnp.dot(h_scr[...], w_ref[...].astype(BF16), preferred_element_type=F32)
    p_ref[...] = y

    @pl.when(j < n_gate_tiles)
    def _():
        gates_ref[...] = y.astype(BF16)


def in_proj(x, gain, w, l, *, tm=TM, tn=TN_IN_PROJ):
    t, d = x.shape
    n = w.shape[2]
    assert COL_GA % tn == 0 and n % tn == 0
    n_mixer_tiles = COL_GA // tn
    n_gate_tiles = n // tn - n_mixer_tiles
    return pl.pallas_call(
        functools.partial(_in_proj_kernel, n_gate_tiles=n_gate_tiles),
        grid=(t // tm, n // tn),
        in_specs=[
            _token_rows_spec(x, tm, room_for_two=x.dtype == BF16),
            pl.BlockSpec((None, 1, d), lambda i, j: (l, 0, 0)),
            pl.BlockSpec((None, d, tn),
                         lambda i, j: (l, 0, jnp.where(j < n_gate_tiles, j + n_mixer_tiles, j - n_gate_tiles))),
        ],
        out_specs=[
            pl.BlockSpec((tm, tn), lambda i, j: (i, jnp.maximum(j - n_gate_tiles, 0))),
            pl.BlockSpec((tm, tn), lambda i, j: (i, jnp.minimum(j, n_gate_tiles - 1))),
        ],
        out_shape=[jax.ShapeDtypeStruct((t, COL_GA), F32), jax.ShapeDtypeStruct((t, n - COL_GA), BF16)],
        scratch_shapes=[pltpu.VMEM((tm, d), BF16)],
        compiler_params=_params("parallel", "arbitrary"),
        name="in_proj",
    )(x, gain.reshape(gain.shape[0], 1, d), w)


def _block_ref_rows(p, h, r):
    c, n = p.shape
    blk = 2 * h
    if blk >= SUBLANES:
        p3 = p.reshape(c // blk, blk, n)
        return jnp.broadcast_to(p3[:, r:r + 1, :], p3.shape).reshape(c, n)
    pos = lax.broadcasted_iota(jnp.int32, p.shape, 0) & (blk - 1)
    out = p
    for src in range(blk):
        if src == r:
            continue
        shifted = pltpu.roll(p, (src - r) % c, axis=0)
        out = jnp.where(pos == src, shifted, out)
    return out


def _chunk_scores(q, key, a, masks_ref, mask_base, reverse):
    c, n = q.shape
    row = lax.broadcasted_iota(jnp.int32, q.shape, 0)
    p = a
    scores = None
    h = 1
    level = 0
    while h < c:
        if h >= SUBLANES:
            nb = c // (2 * h)
            p4 = p.reshape(nb, 2, h, n)
            q4 = q.reshape(nb, 2, h, n)
            k4 = key.reshape(nb, 2, h, n)
            lo, hi = p4[:, 0], p4[:, 1]
            if reverse:
                t_row = hi[:, 0:1, :]
                qk = jnp.stack([q4[:, 0] * jnp.exp(lo), k4[:, 1] * jnp.exp(t_row - hi)], axis=1)
                p = jnp.stack([lo + t_row, hi], axis=1).reshape(c, n)
            else:
                t_row = lo[:, h - 1:h, :]
                qk = jnp.stack([k4[:, 0] * jnp.exp(t_row - lo), q4[:, 1] * jnp.exp(hi)], axis=1)
                p = jnp.stack([lo, hi + t_row], axis=1).reshape(c, n)
            qk = qk.reshape(c, n).astype(BF16)
        else:
            upper = (row & h) != 0
            qside = jnp.logical_not(upper) if reverse else upper
            t_rows = _block_ref_rows(p, h, h if reverse else h - 1)
            e = jnp.where(qside, p, t_rows - p)
            qk = (jnp.where(qside, q, key) * jnp.exp(e)).astype(BF16)
            p = p + jnp.where(qside, t_rows, 0.0)
        s = lax.dot_general(qk, qk, NT_DIMS, preferred_element_type=F32)
        s = s * masks_ref[mask_base + level]
        scores = s if scores is None else scores + s
        h *= 2
        level += 1
    return scores, p


def _log2_decay_and_key(z, lb, log1m_lb, one_m_lb):
    u = jnp.exp(-jnp.abs(z))
    one_u = 1.0 + u
    r = 1.0 / one_u
    pos = z >= 0.0
    sig = jnp.where(pos, 1.0, u) * r
    key = one_m_lb * (jnp.where(pos, u, 1.0) * r)
    y2 = (log1m_lb + jnp.minimum(z, 0.0)) * LOG2E - jnp.log2(one_u)
    return jnp.maximum(jnp.log2(lb + one_m_lb * sig), y2), key


def _cumsum_rows(a, tri_bf16):
    a1 = a.astype(BF16)
    r1 = a - a1.astype(F32)
    a2 = r1.astype(BF16)
    a3 = (r1 - a2.astype(F32)).astype(BF16)
    out = jnp.dot(tri_bf16, jnp.concatenate([a1, a2, a3], axis=1), preferred_element_type=F32)
    n = a.shape[1]
    return (out[:, 2 * n:] + out[:, n:2 * n]) + out[:, :n]


def _hgrn_kernel(q_ref, v_ref, zf_ref, zb_ref, g_ref, lbf_ref, lbb_ref, gain_ref, masks_ref, tri_ref, o_ref,
                 of_scr, ob_scr, qhf_scr, qhb_scr, updf_scr, updb_scr, decf_scr, decb_scr, stf_scr, stb_scr,
                 cumf_scr, cumb_scr, keyf_scr, keyb_scr, totf_scr, totb_scr,
                 *, chunk, n_chunks, n_levels):
    c = chunk

    def rows_of(i):
        return pl.ds(pl.multiple_of(i * c, c), c)

    dirs = (
        (False, zf_ref, lbf_ref, of_scr, qhf_scr, updf_scr, decf_scr, cumf_scr, keyf_scr, totf_scr),
        (True, zb_ref, lbb_ref, ob_scr, qhb_scr, updb_scr, decb_scr, cumb_scr, keyb_scr, totb_scr),
    )

    def gates(z_ref, lb_ref, rows):
        return _log2_decay_and_key(z_ref[0, rows, :], lb_ref[0, 0:1, :], lb_ref[0, 1:2, :], lb_ref[0, 2:3, :])

    def phase0(i, lowest):
        rows = rows_of(i)
        for reverse, z_ref, lb_ref, _, _, _, dec_scr, cum_scr, key_scr, tot_scr in dirs:
            a2, key = gates(z_ref, lb_ref, rows)
            cum = _cumsum_rows(a2, tri_ref[1 if reverse else 0])
            total = cum[0:1, :] if reverse else cum[c - 1:c, :]
            cum_scr[rows, :] = cum
            key_scr[rows, :] = key
            tot_scr[i] = jnp.broadcast_to(total, (SUBLANES, LANES))
            dec_scr[i] = jnp.broadcast_to(jnp.exp2(total), (SUBLANES, LANES))
            lowest = jnp.minimum(lowest, total)
        return lowest

    lowest = lax.fori_loop(0, n_chunks, phase0, jnp.zeros((1, LANES), F32), unroll=UNROLL_GATES)
    safe = jnp.min(lowest) > -SAFE_LOG2_RANGE

    @pl.when(safe)
    def _():
        ti = lax.broadcasted_iota(jnp.int32, (c, c), 0)
        si = lax.broadcasted_iota(jnp.int32, (c, c), 1)

        def body(i, carry):
            rows = rows_of(i)
            q = q_ref[0, rows, :]
            vb = v_ref[0, rows, :].astype(BF16)
            for reverse, _, _, o_scr, qh_scr, upd_scr, _, cum_scr, key_scr, tot_scr in dirs:
                cum = cum_scr[rows, :]
                half = tot_scr[i][0:1, :] * 0.5
                qt = q * jnp.exp2(cum - half)
                kt = key_scr[rows, :] * jnp.exp2(half - cum)
                s = lax.dot_general(qt.astype(BF16), kt.astype(BF16), NT_DIMS, preferred_element_type=F32)
                s = jnp.where((si >= ti) if reverse else (si <= ti), s, 0.0)
                o_scr[rows, :] = jnp.dot(s.astype(BF16), vb, preferred_element_type=F32)
                edge = jnp.exp2(half)
                qh_scr[rows, :] = (qt * edge).astype(BF16)
                upd_scr[i] = lax.dot_general(vb, (kt * edge).astype(BF16), TN_DIMS,
                                             preferred_element_type=F32)
            return carry

        lax.fori_loop(0, n_chunks, body, 0, unroll=True)

    @pl.when(jnp.logical_not(safe))
    def _():
        def body(i, carry):
            rows = rows_of(i)
            q = q_ref[0, rows, :]
            v = v_ref[0, rows, :]
            vb = v.astype(BF16)
            for reverse, z_ref, lb_ref, o_scr, qh_scr, upd_scr, _, _, _, _ in dirs:
                a2, key = gates(z_ref, lb_ref, rows)
                scores, cum = _chunk_scores(q, key, a2 * LN2, masks_ref, n_levels if reverse else 0, reverse)
                diag = jnp.sum(q * key, axis=-1, keepdims=True)
                o_scr[rows, :] = jnp.dot(scores.astype(BF16), vb, preferred_element_type=F32) + diag * v
                total = cum[0:1, :] if reverse else cum[c - 1:c, :]
                qh_scr[rows, :] = (q * jnp.exp(cum)).astype(BF16)
                kd = (key * jnp.exp(total - cum)).astype(BF16)
                upd_scr[i] = lax.dot_general(vb, kd, TN_DIMS, preferred_element_type=F32)
            return carry

        lax.fori_loop(0, n_chunks, body, 0)

    def phase2(i, carry):
        s_f, s_b = carry
        j = n_chunks - 1 - i
        stf_scr[i] = s_f.astype(BF16)
        stb_scr[j] = s_b.astype(BF16)
        s_f = s_f * decf_scr[i][0:1, :] + updf_scr[i]
        s_b = s_b * decb_scr[j][0:1, :] + updb_scr[j]
        return s_f, s_b

    zero = jnp.zeros((LANES, LANES), F32)
    lax.fori_loop(0, n_chunks, phase2, (zero, zero))

    def phase3(i, carry):
        rows = rows_of(i)
        o = of_scr[rows, :] + ob_scr[rows, :]
        o += lax.dot_general(jnp.concatenate([qhf_scr[rows, :], qhb_scr[rows, :]], axis=1),
                             jnp.concatenate([stf_scr[i], stb_scr[i]], axis=1), NT_DIMS,
                             preferred_element_type=F32)
        o = _rmsnorm_rows(o, gain_ref[0])
        g = g_ref[0, rows, :]
        o_ref[0, rows, :] = (o * (g * jax.nn.sigmoid(g))).astype(o_ref.dtype)
        return carry

    lax.fori_loop(0, n_chunks, phase3, 0, unroll=True)


def _level_masks(chunk):
    idx = jnp.arange(chunk)
    t, s = idx[:, None], idx[None, :]
    fwd, bwd = [], []
    h = 1
    while h < chunk:
        same = (t // (2 * h)) == (s // (2 * h))
        t_up = (t & h) != 0
        s_up = (s & h) != 0
        fwd.append(same & t_up & ~s_up)
        bwd.append(same & ~t_up & s_up)
        h *= 2
    return jnp.stack(fwd + bwd).astype(F32)


def hgrn2_mixer(p, lb_f, lb_b, out_gain, *, heads, chunk=HGRN_CHUNK):
    b, s, _ = p.shape
    n_chunks = s // chunk
    n_levels = int(math.log2(chunk))

    def lb_rows(lb):
        lb = lb.astype(F32).reshape(heads, 1, LANES)
        rows = jnp.concatenate([lb, jnp.log1p(-lb), 1.0 - lb], axis=1)
        return jnp.pad(rows, ((0, 0), (0, SUBLANES - 3), (0, 0)))

    def col(block0):
        return pl.BlockSpec((1, s, LANES), lambda bi, hi: (bi, 0, block0 + hi))

    per_head = pl.BlockSpec((1, SUBLANES, LANES), lambda bi, hi: (hi, 0, 0))
    masks = _level_masks(chunk)
    idx = jnp.arange(chunk)
    lower = idx[None, :] <= idx[:, None]
    tri = jnp.stack([lower, lower.T]).astype(BF16)
    kern = functools.partial(_hgrn_kernel, chunk=chunk, n_chunks=n_chunks, n_levels=n_levels)
    seq_f32 = pltpu.VMEM((s, LANES), F32)
    seq_bf16 = pltpu.VMEM((s, LANES), BF16)
    chunk_mat_f32 = pltpu.VMEM((n_chunks, LANES, LANES), F32)
    chunk_mat_bf16 = pltpu.VMEM((n_chunks, LANES, LANES), BF16)
    chunk_row = pltpu.VMEM((n_chunks, SUBLANES, LANES), F32)
    return pl.pallas_call(
        kern,
        grid=(b, heads),
        in_specs=[
            col(COL_RQ // LANES), col(COL_RI // LANES), col(COL_ZF // LANES), col(COL_ZB // LANES),
            col(COL_RG // LANES), per_head, per_head,
            pl.BlockSpec((1, 1, LANES), lambda bi, hi: (hi, 0, 0)),
            pl.BlockSpec(masks.shape, lambda bi, hi: (0, 0, 0)),
            pl.BlockSpec(tri.shape, lambda bi, hi: (0, 0, 0)),
        ],
        out_specs=pl.BlockSpec((1, s, LANES), lambda bi, hi: (bi, 0, hi)),
        out_shape=jax.ShapeDtypeStruct((b, s, heads * LANES), BF16),
        scratch_shapes=[
            seq_f32, seq_f32, seq_bf16, seq_bf16,
            chunk_mat_f32, chunk_mat_f32, chunk_row, chunk_row,
            chunk_mat_bf16, chunk_mat_bf16,
            seq_f32, seq_f32, seq_f32, seq_f32, chunk_row, chunk_row,
        ],
        compiler_params=_params("parallel", "parallel"),
        name="hgrn2",
    )(p, p, p, p, p, lb_rows(lb_f), lb_rows(lb_b), out_gain.astype(F32).reshape(heads, 1, LANES), masks, tri)


ATT_SUB = 2


def _attn_kernel(sink_ref, q_ref, k0_ref, k1_ref, k2_ref, k3_ref, v0_ref, v1_ref, v2_ref, v3_ref,
                 bias0_ref, bias1_ref, o_ref):
    qscale = LOG2E / math.sqrt(ATT_HD)
    qb = Q_BLOCK
    k_refs = (k0_ref, k1_ref, k2_ref, k3_ref)
    v_refs = (v0_ref, v1_ref, v2_ref, v3_ref)

    def head_cols(ref, head, rows=slice(None)):
        return ref[0, rows, head * ATT_HD:(head + 1) * ATT_HD]

    def band(refs, first, kvh):
        return jnp.concatenate([head_cols(r, kvh) for r in refs[first:first + 3]], axis=0).astype(BF16)

    for sub, bias_ref in enumerate((bias0_ref, bias1_ref)):
        rows = slice(sub * qb, (sub + 1) * qb)
        for kvh in range(ATT_KV_HEADS):
            k = band(k_refs, sub, kvh)
            v1 = jnp.concatenate([band(v_refs, sub, kvh), jnp.ones((3 * qb, ATT_HD), BF16)], axis=1)
            for g in range(ATT_GROUP):
                head = kvh * ATT_GROUP + g
                q = (head_cols(q_ref, head, rows) * qscale).astype(BF16)
                s = lax.dot_general(q, k, NT_DIMS, preferred_element_type=F32)
                s = s + bias_ref[0, head * qb:(head + 1) * qb]
                sink = sink_ref[head] * LOG2E
                m = jnp.broadcast_to(jnp.maximum(jnp.max(s, axis=-1, keepdims=True), sink), (qb, ATT_HD))
                e = jnp.exp2(s - jnp.concatenate([m, m, m], axis=1)).astype(BF16)
                ov = jnp.dot(e, v1, preferred_element_type=F32)
                o = ov[:, :ATT_HD] / (ov[:, ATT_HD:] + jnp.exp2(sink - m))
                o_ref[0, rows, head * ATT_HD:(head + 1) * ATT_HD] = o.astype(o_ref.dtype)


def _attn_bias():
    qi = jnp.arange(Q_BLOCK)[:, None]
    kj = jnp.arange(3 * Q_BLOCK)[None, :]
    dist = jnp.abs(kj - Q_BLOCK - qi)
    slopes = 2.0 ** (-8.0 * jnp.arange(1, ATT_HEADS + 1, dtype=F32) / ATT_HEADS)
    alibi = -(slopes[:, None, None] * dist.astype(F32)[None]) * LOG2E
    cases = []
    for has_prev, has_next in ((False, True), (True, True), (True, False)):
        valid = (dist <= WINDOW) & ((kj >= Q_BLOCK) | has_prev) & ((kj < 2 * Q_BLOCK) | has_next)
        cases.append(jnp.where(valid[None], alibi, -jnp.inf).reshape(ATT_HEADS * Q_BLOCK, 3 * Q_BLOCK))
    return jnp.stack(cases)


def window_gqa(p, sink, bias):
    b, s, _ = p.shape
    nb = s // Q_BLOCK
    assert nb >= 2 and nb % ATT_SUB == 0
    kcol, vcol = COL_AK // KV_WIDTH, COL_AV // KV_WIDTH

    def kv(col, off):
        return pl.BlockSpec((1, Q_BLOCK, KV_WIDTH),
                            lambda bi, n, sk: (bi, jnp.clip(ATT_SUB * n - 1 + off, 0, nb - 1), col))

    def bias_spec(sub):
        def case(bi, n, sk):
            blk = ATT_SUB * n + sub
            return (jnp.where(blk == 0, 0, jnp.where(blk == nb - 1, 2, 1)), 0, 0)
        return pl.BlockSpec((1, ATT_HEADS * Q_BLOCK, 3 * Q_BLOCK), case)

    rows = ATT_SUB * Q_BLOCK
    n_band = ATT_SUB + 2
    return pl.pallas_call(
        _attn_kernel,
        grid_spec=pltpu.PrefetchScalarGridSpec(
            num_scalar_prefetch=1,
            grid=(b, nb // ATT_SUB),
            in_specs=[pl.BlockSpec((1, rows, ATT_WIDTH), lambda bi, n, sk: (bi, n, COL_AQ // ATT_WIDTH))]
            + [kv(kcol, off) for off in range(n_band)]
            + [kv(vcol, off) for off in range(n_band)]
            + [bias_spec(sub) for sub in range(ATT_SUB)],
            out_specs=pl.BlockSpec((1, rows, ATT_WIDTH), lambda bi, n, sk: (bi, n, 0)),
        ),
        out_shape=jax.ShapeDtypeStruct((b, s, ATT_WIDTH), BF16),
        compiler_params=_params("parallel", "arbitrary"),
        name="window_gqa",
    )(sink.astype(F32), *([p] * (1 + 2 * n_band)), *([bias] * ATT_SUB))


def _merge_kernel(oa_ref, ob_ref, wa_ref, wb_ref, ga_ref, gb_ref, m_ref, wa_scr, wb_scr):
    @pl.when(pl.program_id(0) == 0)
    def _():
        wa_scr[...] = wa_ref[...].astype(BF16)
        wb_scr[...] = wb_ref[...].astype(BF16)

    oa = oa_ref[...]
    ob = ob_ref[...]
    for c0 in range(0, m_ref.shape[1], MERGE_COLS):
        cols = slice(c0, c0 + MERGE_COLS)
        ya = jnp.dot(oa, wa_scr[:, cols], preferred_element_type=F32)
        yb = jnp.dot(ob, wb_scr[:, cols], preferred_element_type=F32)
        ga = jax.nn.sigmoid(ga_ref[:, cols].astype(F32))
        gb = jax.nn.sigmoid(gb_ref[:, cols].astype(F32))
        m_ref[:, cols] = (ga * ya + gb * yb).astype(m_ref.dtype)


def merge_branches(oa, ob, wa, wb, gates, l, *, tm=TM_MERGE):
    t, ka = oa.shape
    kb = ob.shape[1]
    n = wa.shape[2]
    return pl.pallas_call(
        _merge_kernel,
        grid=(t // tm,),
        in_specs=[
            pl.BlockSpec((tm, ka), lambda i: (i, 0)),
            pl.BlockSpec((tm, kb), lambda i: (i, 0)),
            _once((None, ka, n), lambda i: (l, 0, 0)),
            _once((None, kb, n), lambda i: (l, 0, 0)),
            pl.BlockSpec((tm, n), lambda i: (i, 0)),
            pl.BlockSpec((tm, n), lambda i: (i, 1)),
        ],
        out_specs=pl.BlockSpec((tm, n), lambda i: (i, 0)),
        out_shape=jax.ShapeDtypeStruct((t, n), BF16),
        scratch_shapes=[pltpu.VMEM((ka, n), BF16), pltpu.VMEM((kb, n), BF16)],
        compiler_params=_params("arbitrary"),
        name="merge_branches",
    )(oa, ob, wa, wb, gates, gates)


def _final_norm_kernel(x_ref, gain_ref, o_ref):
    o_ref[...] = _rmsnorm_rows(x_ref[...], gain_ref[...])


def final_rmsnorm(x, gain, *, tm=TM_NORM):
    t, d = x.shape
    return pl.pallas_call(
        _final_norm_kernel,
        grid=(t // tm,),
        in_specs=[pl.BlockSpec((tm, d), lambda i: (i, 0)), pl.BlockSpec((1, d), lambda i: (0, 0))],
        out_specs=pl.BlockSpec((tm, d), lambda i: (i, 0)),
        out_shape=jax.ShapeDtypeStruct((t, d), F32),
        compiler_params=_params("parallel"),
        name="final_norm",
    )(x, gain.reshape(1, d))


def _lower_bounds(lb_logits):
    lb = jnp.cumsum(jax.nn.softmax(lb_logits.astype(F32), axis=0), axis=0)
    return lb - lb[0:1]


def kernel(x, ffn1_norm, ffn1_w_gate, ffn1_w_up, ffn1_w_down, mix_norm, w_in, lb_fwd_logits, lb_bwd_logits, rg_out_norm, attn_sink, w_branch_a, w_branch_b, w_out, ffn2_norm, ffn2_w_gate, ffn2_w_up, ffn2_w_down, final_norm):
    b, s, d = x.shape
    t = b * s
    depth = w_in.shape[0]
    lb_f_all = _lower_bounds(lb_fwd_logits)
    lb_b_all = _lower_bounds(lb_bwd_logits)
    xf = x.reshape(t, d)
    x_norm_in = xf
    attn_bias = _attn_bias()

    def ffn_down(a, xf, wd, l, second):
        return resid_mm(a, wd, xf, l, scale=0.5, tm=TM_DOWN, tn=TN_DOWN, second=second)

    for l in range(depth):
        a = ffn_up(x_norm_in, ffn1_norm, ffn1_w_gate, ffn1_w_up, l, normalized=False)
        xf, x_norm_in = ffn_down(a, xf, ffn1_w_down, l, "copy")
        p, gates = in_proj(x_norm_in, mix_norm, w_in, l)
        p3 = p.reshape(b, s, COL_GA)
        o_a = hgrn2_mixer(p3, lb_f_all[l], lb_b_all[l], rg_out_norm[l], heads=RG_HEADS)
        o_b = window_gqa(p3, attn_sink[l], attn_bias)
        merged = merge_branches(o_a.reshape(t, RG_WIDTH), o_b.reshape(t, ATT_WIDTH),
                                w_branch_a, w_branch_b, gates, l)
        xf, h = resid_mm(merged, w_out, xf, l, scale=1.0, tm=TM_OUT, tn=TN_OUT, second="norm",
                         norm_gain=ffn2_norm)
        a = ffn_up(h, ffn2_norm, ffn2_w_gate, ffn2_w_up, l, normalized=True)
        if l + 1 < depth:
            xf, x_norm_in = ffn_down(a, xf, ffn2_w_down, l, "copy")
        else:
            xf = ffn_down(a, xf, ffn2_w_down, l, None)
    return final_rmsnorm(xf, final_norm).reshape(b, s, d)
```

```python
import functools
import math

import jax
import jax.numpy as jnp
from jax import lax
from jax.experimental import pallas as pl
from jax.experimental.pallas import tpu as pltpu

D_MODEL = 2048
RG_HEADS = 8
RG_DV = 128
RG_WIDTH = RG_HEADS * RG_DV
ATT_HEADS = 8
ATT_KV_HEADS = 2
ATT_GROUP = ATT_HEADS // ATT_KV_HEADS
ATT_HD = 128
ATT_WIDTH = ATT_HEADS * ATT_HD
KV_WIDTH = ATT_KV_HEADS * ATT_HD
WINDOW = 128
Q_BLOCK = 128
EPS = 1e-6

COL_RQ = 0
COL_RI = RG_WIDTH
COL_ZF = 2 * RG_WIDTH
COL_ZB = 3 * RG_WIDTH
COL_RG = 4 * RG_WIDTH
COL_AQ = 5 * RG_WIDTH
COL_AK = COL_AQ + ATT_WIDTH
COL_AV = COL_AK + KV_WIDTH
COL_GA = COL_AV + KV_WIDTH
COL_GB = COL_GA + D_MODEL

LANES = 128
SUBLANES = 8
HGRN_CHUNK = 128
UNROLL_GATES = 16
VMEM_LIMIT_BYTES = 56 * 1024 * 1024
LOG2E = 1.4426950408889634
LN2 = 0.6931471805599453
SAFE_LOG2_RANGE = 240.0

TM = 2048
TM_DOWN = 1024
TM_MERGE = 512
TM_OUT = 512
TN_FFN_UP = 512
TN_IN_PROJ = 512
MERGE_COLS = 512
TN_DOWN = 512
TN_OUT = 2048
TM_NORM = 1024
CAST_STRIP_ROWS = 512
NORM_STRIP_ROWS = 256
MXU_COLS = 256

BF16 = jnp.bfloat16
F32 = jnp.float32
NT_DIMS = (((1,), (1,)), ((), ()))
TN_DIMS = (((0,), (0,)), ((), ()))


def _params(*sem):
    return pltpu.CompilerParams(dimension_semantics=sem, vmem_limit_bytes=VMEM_LIMIT_BYTES)


def _rmsnorm_rows(x, gain):
    ms = jnp.mean(x * x, axis=-1, keepdims=True)
    return x * lax.rsqrt(ms + EPS) * gain


def _rmsnorm_tile_to_bf16(x_ref, gain_ref, h_ref):
    def strip(r, carry):
        rows = pl.ds(pl.multiple_of(r * NORM_STRIP_ROWS, NORM_STRIP_ROWS), NORM_STRIP_ROWS)
        h_ref[rows, :] = _rmsnorm_rows(x_ref[rows, :].astype(F32), gain_ref[...]).astype(BF16)
        return carry

    lax.fori_loop(0, x_ref.shape[0] // NORM_STRIP_ROWS, strip, 0)


def _once(block_shape, index_map):
    return pl.BlockSpec(block_shape, index_map, pipeline_mode=pl.Buffered(1))


def _token_rows_spec(x, tm, room_for_two):
    spec = pl.BlockSpec if room_for_two else _once
    return spec((tm, x.shape[1]), lambda i, j: (i, 0))


def _ffn_up_kernel(*refs, normalized):
    if normalized:
        h_ref, wg_ref, wu_ref, a_ref = refs
        h = h_ref[...]
    else:
        x_ref, gain_ref, wg_ref, wu_ref, a_ref, h_scr = refs

        @pl.when(pl.program_id(1) == 0)
        def _():
            _rmsnorm_tile_to_bf16(x_ref, gain_ref, h_scr)

        h = h_scr[...]
    for c0 in range(0, a_ref.shape[1], MXU_COLS):
        cols = slice(c0, c0 + MXU_COLS)
        g = jnp.dot(h, wg_ref[:, cols].astype(BF16), preferred_element_type=F32)
        u = jnp.dot(h, wu_ref[:, cols].astype(BF16), preferred_element_type=F32)
        a_ref[:, cols] = (g * jax.nn.sigmoid(g) * u).astype(BF16)


def ffn_up(x, gain, wg, wu, l, *, tm=TM, tn=TN_FFN_UP):
    t, d = x.shape
    f = wg.shape[2]
    normalized = gain is None
    weight_tile = pl.BlockSpec((None, d, tn), lambda i, j: (l, 0, j))
    in_specs = [_token_rows_spec(x, tm, room_for_two=normalized)]
    operands = [x]
    if not normalized:
        in_specs.append(pl.BlockSpec((None, 1, d), lambda i, j: (l, 0, 0)))
        operands.append(gain.reshape(gain.shape[0], 1, d))
    return pl.pallas_call(
        functools.partial(_ffn_up_kernel, normalized=normalized),
        grid=(t // tm, f // tn),
        in_specs=in_specs + [weight_tile, weight_tile],
        out_specs=pl.BlockSpec((tm, tn), lambda i, j: (i, j)),
        out_shape=jax.ShapeDtypeStruct((t, f), BF16),
        scratch_shapes=[] if normalized else [pltpu.VMEM((tm, d), BF16)],
        compiler_params=_params("parallel", "arbitrary"),
        name="ffn_up",
    )(*operands, wg, wu)


def _resid_mm_kernel(a_ref, w_ref, x_ref, *rest, scale, second):
    gain_ref, rest = (rest[0], rest[1:]) if second == "norm" else (None, rest)
    o_ref, *second_refs, w_scr = rest

    @pl.when(pl.program_id(1) == 0)
    def _():
        def strip(r, carry):
            rows = pl.ds(pl.multiple_of(r * CAST_STRIP_ROWS, CAST_STRIP_ROWS), CAST_STRIP_ROWS)
            w_scr[rows, :] = w_ref[rows, :].astype(BF16)
            return carry

        lax.fori_loop(0, w_ref.shape[0] // CAST_STRIP_ROWS, strip, 0)

    y = jnp.dot(a_ref[...], w_scr[...], preferred_element_type=F32)
    out = x_ref[...] + (y if scale == 1.0 else scale * y)
    o_ref[...] = out
    if second == "copy":
        second_refs[0][...] = out.astype(BF16)
    elif second == "norm":
        second_refs[0][...] = _rmsnorm_rows(out, gain_ref[...]).astype(BF16)


def resid_mm(a, w, x, l, *, scale, tm, tn, second=None, norm_gain=None):
    t, kdim = a.shape
    n = w.shape[2]
    assert kdim % CAST_STRIP_ROWS == 0 and (second != "norm" or tn == n)
    tile = pl.BlockSpec((tm, tn), lambda j, i: (i, j))
    shape = jax.ShapeDtypeStruct((t, n), F32)
    in_specs = [
        pl.BlockSpec((tm, kdim), lambda j, i: (i, 0)),
        _once((None, kdim, tn), lambda j, i: (l, 0, j)),
        tile,
    ]
    operands = [a, w, x]
    out_specs, out_shape = tile, shape
    if second is not None:
        out_specs, out_shape = [tile, tile], [shape, jax.ShapeDtypeStruct((t, n), BF16)]
    if second == "norm":
        in_specs.append(pl.BlockSpec((None, 1, n), lambda j, i: (l, 0, 0)))
        operands.append(norm_gain.reshape(norm_gain.shape[0], 1, n))
    return pl.pallas_call(
        functools.partial(_resid_mm_kernel, scale=scale, second=second),
        grid=(n // tn, t // tm),
        in_specs=in_specs,
        out_specs=out_specs,
        out_shape=out_shape,
        scratch_shapes=[pltpu.VMEM((kdim, tn), BF16)],
        compiler_params=_params("arbitrary", "arbitrary"),
        name="resid_mm",
    )(*operands)


def _in_proj_kernel(x_ref, gain_ref, w_ref, p_ref, gates_ref, h_scr, *, n_gate_tiles):
    j = pl.program_id(1)

    @pl.when(j == 0)
    def _():
        _rmsnorm_tile_to_bf16(x_ref, gain_ref, h_scr)

    y = jnp.dot(h_scr[...], w_ref[...].astype(BF16), preferred_element_type=F32)
    p_ref[...] = y

    @pl.when(j < n_gate_tiles)
    def _():
        gates_ref[...] = y.astype(BF16)


def in_proj(x, gain, w, l, *, tm=TM, tn=TN_IN_PROJ):
    t, d = x.shape
    n = w.shape[2]
    assert COL_GA % tn == 0 and n % tn == 0
    n_mixer_tiles = COL_GA // tn
    n_gate_tiles = n // tn - n_mixer_tiles
    return pl.pallas_call(
        functools.partial(_in_proj_kernel, n_gate_tiles=n_gate_tiles),
        grid=(t // tm, n // tn),
        in_specs=[
            _token_rows_spec(x, tm, room_for_two=x.dtype == BF16),
            pl.BlockSpec((None, 1, d), lambda i, j: (l, 0, 0)),
            pl.BlockSpec((None, d, tn),
                         lambda i, j: (l, 0, jnp.where(j < n_gate_tiles, j + n_mixer_tiles, j - n_gate_tiles))),
        ],
        out_specs=[
            pl.BlockSpec((tm, tn), lambda i, j: (i, jnp.maximum(j - n_gate_tiles, 0))),
            pl.BlockSpec((tm, tn), lambda i, j: (i, jnp.minimum(j, n_gate_tiles - 1))),
        ],
        out_shape=[jax.ShapeDtypeStruct((t, COL_GA), F32), jax.ShapeDtypeStruct((t, n - COL_GA), BF16)],
        scratch_shapes=[pltpu.VMEM((tm, d), BF16)],
        compiler_params=_params("parallel", "arbitrary"),
        name="in_proj",
    )(x, gain.reshape(gain.shape[0], 1, d), w)


def _block_ref_rows(p, h, r):
    c, n = p.shape
    blk = 2 * h
    if blk >= SUBLANES:
        p3 = p.reshape(c // blk, blk, n)
        return jnp.broadcast_to(p3[:, r:r + 1, :], p3.shape).reshape(c, n)
    pos = lax.broadcasted_iota(jnp.int32, p.shape, 0) & (blk - 1)
    out = p
    for src in range(blk):
        if src == r:
            continue
        shifted = pltpu.roll(p, (src - r) % c, axis=0)
        out = jnp.where(pos == src, shifted, out)
    return out


def _chunk_scores(q, key, a, masks_ref, mask_base, reverse):
    c, n = q.shape
    row = lax.broadcasted_iota(jnp.int32, q.shape, 0)
    p = a
    scores = None
    h = 1
    level = 0
    while h < c:
        if h >= SUBLANES:
            nb = c // (2 * h)
            p4 = p.reshape(nb, 2, h, n)
            q4 = q.reshape(nb, 2, h, n)
            k4 = key.reshape(nb, 2, h, n)
            lo, hi = p4[:, 0], p4[:, 1]
            if reverse:
                t_row = hi[:, 0:1, :]
                qk = jnp.stack([q4[:, 0] * jnp.exp(lo), k4[:, 1] * jnp.exp(t_row - hi)], axis=1)
                p = jnp.stack([lo + t_row, hi], axis=1).reshape(c, n)
            else:
                t_row = lo[:, h - 1:h, :]
                qk = jnp.stack([k4[:, 0] * jnp.exp(t_row - lo), q4[:, 1] * jnp.exp(hi)], axis=1)
                p = jnp.stack([lo, hi + t_row], axis=1).reshape(c, n)
            qk = qk.reshape(c, n).astype(BF16)
        else:
            upper = (row & h) != 0
            qside = jnp.logical_not(upper) if reverse else upper
            t_rows = _block_ref_rows(p, h, h if reverse else h - 1)
            e = jnp.where(qside, p, t_rows - p)
            qk = (jnp.where(qside, q, key) * jnp.exp(e)).astype(BF16)
            p = p + jnp.where(qside, t_rows, 0.0)
        s = lax.dot_general(qk, qk, NT_DIMS, preferred_element_type=F32)
        s = s * masks_ref[mask_base + level]
        scores = s if scores is None else scores + s
        h *= 2
        level += 1
    return scores, p


def _log2_decay_and_key(z, lb, log1m_lb, one_m_lb):
    u = jnp.exp(-jnp.abs(z))
    one_u = 1.0 + u
    r = 1.0 / one_u
    pos = z >= 0.0
    sig = jnp.where(pos, 1.0, u) * r
    key = one_m_lb * (jnp.where(pos, u, 1.0) * r)
    y2 = (log1m_lb + jnp.minimum(z, 0.0)) * LOG2E - jnp.log2(one_u)
    return jnp.maximum(jnp.log2(lb + one_m_lb * sig), y2), key


def _cumsum_rows(a, tri_bf16):
    a1 = a.astype(BF16)
    r1 = a - a1.astype(F32)
    a2 = r1.astype(BF16)
    a3 = (r1 - a2.astype(F32)).astype(BF16)
    out = jnp.dot(tri_bf16, jnp.concatenate([a1, a2, a3], axis=1), preferred_element_type=F32)
    n = a.shape[1]
    return (out[:, 2 * n:] + out[:, n:2 * n]) + out[:, :n]


def _hgrn_kernel(q_ref, v_ref, zf_ref, zb_ref, g_ref, lbf_ref, lbb_ref, gain_ref, masks_ref, tri_ref, o_ref,
                 of_scr, ob_scr, qhf_scr, qhb_scr, updf_scr, updb_scr, decf_scr, decb_scr, stf_scr, stb_scr,
                 cumf_scr, cumb_scr, keyf_scr, keyb_scr, totf_scr, totb_scr,
                 *, chunk, n_chunks, n_levels):
    c = chunk

    def rows_of(i):
        return pl.ds(pl.multiple_of(i * c, c), c)

    dirs = (
        (False, zf_ref, lbf_ref, of_scr, qhf_scr, updf_scr, decf_scr, cumf_scr, keyf_scr, totf_scr),
        (True, zb_ref, lbb_ref, ob_scr, qhb_scr, updb_scr, decb_scr, cumb_scr, keyb_scr, totb_scr),
    )

    def gates(z_ref, lb_ref, rows):
        return _log2_decay_and_key(z_ref[0, rows, :], lb_ref[0, 0:1, :], lb_ref[0, 1:2, :], lb_ref[0, 2:3, :])

    def phase0(i, lowest):
        rows = rows_of(i)
        for reverse, z_ref, lb_ref, _, _, _, dec_scr, cum_scr, key_scr, tot_scr in dirs:
            a2, key = gates(z_ref, lb_ref, rows)
            cum = _cumsum_rows(a2, tri_ref[1 if reverse else 0])
            total = cum[0:1, :] if reverse else cum[c - 1:c, :]
            cum_scr[rows, :] = cum
            key_scr[rows, :] = key
            tot_scr[i] = jnp.broadcast_to(total, (SUBLANES, LANES))
            dec_scr[i] = jnp.broadcast_to(jnp.exp2(total), (SUBLANES, LANES))
            lowest = jnp.minimum(lowest, total)
        return lowest

    lowest = lax.fori_loop(0, n_chunks, phase0, jnp.zeros((1, LANES), F32), unroll=UNROLL_GATES)
    safe = jnp.min(lowest) > -SAFE_LOG2_RANGE

    @pl.when(safe)
    def _():
        ti = lax.broadcasted_iota(jnp.int32, (c, c), 0)
        si = lax.broadcasted_iota(jnp.int32, (c, c), 1)

        def body(i, carry):
            rows = rows_of(i)
            q = q_ref[0, rows, :]
            vb = v_ref[0, rows, :].astype(BF16)
            for reverse, _, _, o_scr, qh_scr, upd_scr, _, cum_scr, key_scr, tot_scr in dirs:
                cum = cum_scr[rows, :]
                half = tot_scr[i][0:1, :] * 0.5
                qt = q * jnp.exp2(cum - half)
                kt = key_scr[rows, :] * jnp.exp2(half - cum)
                s = lax.dot_general(qt.astype(BF16), kt.astype(BF16), NT_DIMS, preferred_element_type=F32)
                s = jnp.where((si >= ti) if reverse else (si <= ti), s, 0.0)
                o_scr[rows, :] = jnp.dot(s.astype(BF16), vb, preferred_element_type=F32)
                edge = jnp.exp2(half)
                qh_scr[rows, :] = (qt * edge).astype(BF16)
                upd_scr[i] = lax.dot_general(vb, (kt * edge).astype(BF16), TN_DIMS,
                                             preferred_element_type=F32)
            return carry

        lax.fori_loop(0, n_chunks, body, 0, unroll=True)

    @pl.when(jnp.logical_not(safe))
    def _():
        def body(i, carry):
            rows = rows_of(i)
            q = q_ref[0, rows, :]
            v = v_ref[0, rows, :]
            vb = v.astype(BF16)
            for reverse, z_ref, lb_ref, o_scr, qh_scr, upd_scr, _, _, _, _ in dirs:
                a2, key = gates(z_ref, lb_ref, rows)
                scores, cum = _chunk_scores(q, key, a2 * LN2, masks_ref, n_levels if reverse else 0, reverse)
                diag = jnp.sum(q * key, axis=-1, keepdims=True)
                o_scr[rows, :] = jnp.dot(scores.astype(BF16), vb, preferred_element_type=F32) + diag * v
                total = cum[0:1, :] if reverse else cum[c - 1:c, :]
                qh_scr[rows, :] = (q * jnp.exp(cum)).astype(BF16)
                kd = (key * jnp.exp(total - cum)).astype(BF16)
                upd_scr[i] = lax.dot_general(vb, kd, TN_DIMS, preferred_element_type=F32)
            return carry

        lax.fori_loop(0, n_chunks, body, 0)

    def phase2(i, carry):
        s_f, s_b = carry
        j = n_chunks - 1 - i
        stf_scr[i] = s_f.astype(BF16)
        stb_scr[j] = s_b.astype(BF16)
        s_f = s_f * decf_scr[i][0:1, :] + updf_scr[i]
        s_b = s_b * decb_scr[j][0:1, :] + updb_scr[j]
        return s_f, s_b

    zero = jnp.zeros((LANES, LANES), F32)
    lax.fori_loop(0, n_chunks, phase2, (zero, zero))

    def phase3(i, carry):
        rows = rows_of(i)
        o = of_scr[rows, :] + ob_scr[rows, :]
        o += lax.dot_general(jnp.concatenate([qhf_scr[rows, :], qhb_scr[rows, :]], axis=1),
                             jnp.concatenate([stf_scr[i], stb_scr[i]], axis=1), NT_DIMS,
                             preferred_element_type=F32)
        o = _rmsnorm_rows(o, gain_ref[0])
        g = g_ref[0, rows, :]
        o_ref[0, rows, :] = (o * (g * jax.nn.sigmoid(g))).astype(o_ref.dtype)
        return carry

    lax.fori_loop(0, n_chunks, phase3, 0, unroll=True)


def _level_masks(chunk):
    idx = jnp.arange(chunk)
    t, s = idx[:, None], idx[None, :]
    fwd, bwd = [], []
    h = 1
    while h < chunk:
        same = (t // (2 * h)) == (s // (2 * h))
        t_up = (t & h) != 0
        s_up = (s & h) != 0
        fwd.append(same & t_up & ~s_up)
        bwd.append(same & ~t_up & s_up)
        h *= 2
    return jnp.stack(fwd + bwd).astype(F32)


def hgrn2_mixer(p, lb_f, lb_b, out_gain, *, heads, chunk=HGRN_CHUNK):
    b, s, _ = p.shape
    n_chunks = s // chunk
    n_levels = int(math.log2(chunk))

    def lb_rows(lb):
        lb = lb.astype(F32).reshape(heads, 1, LANES)
        rows = jnp.concatenate([lb, jnp.log1p(-lb), 1.0 - lb], axis=1)
        return jnp.pad(rows, ((0, 0), (0, SUBLANES - 3), (0, 0)))

    def col(block0):
        return pl.BlockSpec((1, s, LANES), lambda bi, hi: (bi, 0, block0 + hi))

    per_head = pl.BlockSpec((1, SUBLANES, LANES), lambda bi, hi: (hi, 0, 0))
    masks = _level_masks(chunk)
    idx = jnp.arange(chunk)
    lower = idx[None, :] <= idx[:, None]
    tri = jnp.stack([lower, lower.T]).astype(BF16)
    kern = functools.partial(_hgrn_kernel, chunk=chunk, n_chunks=n_chunks, n_levels=n_levels)
    seq_f32 = pltpu.VMEM((s, LANES), F32)
    seq_bf16 = pltpu.VMEM((s, LANES), BF16)
    chunk_mat_f32 = pltpu.VMEM((n_chunks, LANES, LANES), F32)
    chunk_mat_bf16 = pltpu.VMEM((n_chunks, LANES, LANES), BF16)
    chunk_row = pltpu.VMEM((n_chunks, SUBLANES, LANES), F32)
    return pl.pallas_call(
        kern,
        grid=(b, heads),
        in_specs=[
            col(COL_RQ // LANES), col(COL_RI // LANES), col(COL_ZF // LANES), col(COL_ZB // LANES),
            col(COL_RG // LANES), per_head, per_head,
            pl.BlockSpec((1, 1, LANES), lambda bi, hi: (hi, 0, 0)),
            pl.BlockSpec(masks.shape, lambda bi, hi: (0, 0, 0)),
            pl.BlockSpec(tri.shape, lambda bi, hi: (0, 0, 0)),
        ],
        out_specs=pl.BlockSpec((1, s, LANES), lambda bi, hi: (bi, 0, hi)),
        out_shape=jax.ShapeDtypeStruct((b, s, heads * LANES), BF16),
        scratch_shapes=[
            seq_f32, seq_f32, seq_bf16, seq_bf16,
            chunk_mat_f32, chunk_mat_f32, chunk_row, chunk_row,
            chunk_mat_bf16, chunk_mat_bf16,
            seq_f32, seq_f32, seq_f32, seq_f32, chunk_row, chunk_row,
        ],
        compiler_params=_params("parallel", "parallel"),
        name="hgrn2",
    )(p, p, p, p, p, lb_rows(lb_f), lb_rows(lb_b), out_gain.astype(F32).reshape(heads, 1, LANES), masks, tri)


ATT_SUB = 2


def _attn_kernel(sink_ref, q_ref, k0_ref, k1_ref, k2_ref, k3_ref, v0_ref, v1_ref, v2_ref, v3_ref,
                 bias0_ref, bias1_ref, o_ref):
    qscale = LOG2E / math.sqrt(ATT_HD)
    qb = Q_BLOCK
    k_refs = (k0_ref, k1_ref, k2_ref, k3_ref)
    v_refs = (v0_ref, v1_ref, v2_ref, v3_ref)

    def head_cols(ref, head, rows=slice(None)):
        return ref[0, rows, head * ATT_HD:(head + 1) * ATT_HD]

    def band(refs, first, kvh):
        return jnp.concatenate([head_cols(r, kvh) for r in refs[first:first + 3]], axis=0).astype(BF16)

    for sub, bias_ref in enumerate((bias0_ref, bias1_ref)):
        rows = slice(sub * qb, (sub + 1) * qb)
        for kvh in range(ATT_KV_HEADS):
            k = band(k_refs, sub, kvh)
            v1 = jnp.concatenate([band(v_refs, sub, kvh), jnp.ones((3 * qb, ATT_HD), BF16)], axis=1)
            for g in range(ATT_GROUP):
                head = kvh * ATT_GROUP + g
                q = (head_cols(q_ref, head, rows) * qscale).astype(BF16)
                s = lax.dot_general(q, k, NT_DIMS, preferred_element_type=F32)
                s = s + bias_ref[0, head * qb:(head + 1) * qb]
                sink = sink_ref[head] * LOG2E
                m = jnp.broadcast_to(jnp.maximum(jnp.max(s, axis=-1, keepdims=True), sink), (qb, ATT_HD))
                e = jnp.exp2(s - jnp.concatenate([m, m, m], axis=1)).astype(BF16)
                ov = jnp.dot(e, v1, preferred_element_type=F32)
                o = ov[:, :ATT_HD] / (ov[:, ATT_HD:] + jnp.exp2(sink - m))
                o_ref[0, rows, head * ATT_HD:(head + 1) * ATT_HD] = o.astype(o_ref.dtype)


def _attn_bias():
    qi = jnp.arange(Q_BLOCK)[:, None]
    kj = jnp.arange(3 * Q_BLOCK)[None, :]
    dist = jnp.abs(kj - Q_BLOCK - qi)
    slopes = 2.0 ** (-8.0 * jnp.arange(1, ATT_HEADS + 1, dtype=F32) / ATT_HEADS)
    alibi = -(slopes[:, None, None] * dist.astype(F32)[None]) * LOG2E
    cases = []
    for has_prev, has_next in ((False, True), (True, True), (True, False)):
        valid = (dist <= WINDOW) & ((kj >= Q_BLOCK) | has_prev) & ((kj < 2 * Q_BLOCK) | has_next)
        cases.append(jnp.where(valid[None], alibi, -jnp.inf).reshape(ATT_HEADS * Q_BLOCK, 3 * Q_BLOCK))
    return jnp.stack(cases)


def window_gqa(p, sink, bias):
    b, s, _ = p.shape
    nb = s // Q_BLOCK
    assert nb >= 2 and nb % ATT_SUB == 0
    kcol, vcol = COL_AK // KV_WIDTH, COL_AV // KV_WIDTH

    def kv(col, off):
        return pl.BlockSpec((1, Q_BLOCK, KV_WIDTH),
                            lambda bi, n, sk: (bi, jnp.clip(ATT_SUB * n - 1 + off, 0, nb - 1), col))

    def bias_spec(sub):
        def case(bi, n, sk):
            blk = ATT_SUB * n + sub
            return (jnp.where(blk == 0, 0, jnp.where(blk == nb - 1, 2, 1)), 0, 0)
        return pl.BlockSpec((1, ATT_HEADS * Q_BLOCK, 3 * Q_BLOCK), case)

    rows = ATT_SUB * Q_BLOCK
    n_band = ATT_SUB + 2
    return pl.pallas_call(
        _attn_kernel,
        grid_spec=pltpu.PrefetchScalarGridSpec(
            num_scalar_prefetch=1,
            grid=(b, nb // ATT_SUB),
            in_specs=[pl.BlockSpec((1, rows, ATT_WIDTH), lambda bi, n, sk: (bi, n, COL_AQ // ATT_WIDTH))]
            + [kv(kcol, off) for off in range(n_band)]
            + [kv(vcol, off) for off in range(n_band)]
            + [bias_spec(sub) for sub in range(ATT_SUB)],
            out_specs=pl.BlockSpec((1, rows, ATT_WIDTH), lambda bi, n, sk: (bi, n, 0)),
        ),
        out_shape=jax.ShapeDtypeStruct((b, s, ATT_WIDTH), BF16),
        compiler_params=_params("parallel", "arbitrary"),
        name="window_gqa",
    )(sink.astype(F32), *([p] * (1 + 2 * n_band)), *([bias] * ATT_SUB))


def _merge_kernel(oa_ref, ob_ref, wa_ref, wb_ref, ga_ref, gb_ref, m_ref, wa_scr, wb_scr):
    @pl.when(pl.program_id(0) == 0)
    def _():
        wa_scr[...] = wa_ref[...].astype(BF16)
        wb_scr[...] = wb_ref[...].astype(BF16)

    oa = oa_ref[...]
    ob = ob_ref[...]
    for c0 in range(0, m_ref.shape[1], MERGE_COLS):
        cols = slice(c0, c0 + MERGE_COLS)
        ya = jnp.dot(oa, wa_scr[:, cols], preferred_element_type=F32)
        yb = jnp.dot(ob, wb_scr[:, cols], preferred_element_type=F32)
        ga = jax.nn.sigmoid(ga_ref[:, cols].astype(F32))
        gb = jax.nn.sigmoid(gb_ref[:, cols].astype(F32))
        m_ref[:, cols] = (ga * ya + gb * yb).astype(m_ref.dtype)


def merge_branches(oa, ob, wa, wb, gates, l, *, tm=TM_MERGE):
    t, ka = oa.shape
    kb = ob.shape[1]
    n = wa.shape[2]
    return pl.pallas_call(
        _merge_kernel,
        grid=(t // tm,),
        in_specs=[
            pl.BlockSpec((tm, ka), lambda i: (i, 0)),
            pl.BlockSpec((tm, kb), lambda i: (i, 0)),
            _once((None, ka, n), lambda i: (l, 0, 0)),
            _once((None, kb, n), lambda i: (l, 0, 0)),
            pl.BlockSpec((tm, n), lambda i: (i, 0)),
            pl.BlockSpec((tm, n), lambda i: (i, 1)),
        ],
        out_specs=pl.BlockSpec((tm, n), lambda i: (i, 0)),
        out_shape=jax.ShapeDtypeStruct((t, n), BF16),
        scratch_shapes=[pltpu.VMEM((ka, n), BF16), pltpu.VMEM((kb, n), BF16)],
        compiler_params=_params("arbitrary"),
        name="merge_branches",
    )(oa, ob, wa, wb, gates, gates)


def _final_norm_kernel(x_ref, gain_ref, o_ref):
    o_ref[...] = _rmsnorm_rows(x_ref[...], gain_ref[...])


def final_rmsnorm(x, gain, *, tm=TM_NORM):
    t, d = x.shape
    return pl.pallas_call(
        _final_norm_kernel,
        grid=(t // tm,),
        in_specs=[pl.BlockSpec((tm, d), lambda i: (i, 0)), pl.BlockSpec((1, d), lambda i: (0, 0))],
        out_specs=pl.BlockSpec((tm, d), lambda i: (i, 0)),
        out_shape=jax.ShapeDtypeStruct((t, d), F32),
        compiler_params=_params("parallel"),
        name="final_norm",
    )(x, gain.reshape(1, d))


def _lower_bounds(lb_logits):
    lb = jnp.cumsum(jax.nn.softmax(lb_logits.astype(F32), axis=0), axis=0)
    return lb - lb[0:1]


def kernel(x, ffn1_norm, ffn1_w_gate, ffn1_w_up, ffn1_w_down, mix_norm, w_in, lb_fwd_logits, lb_bwd_logits, rg_out_norm, attn_sink, w_branch_a, w_branch_b, w_out, ffn2_norm, ffn2_w_gate, ffn2_w_up, ffn2_w_down, final_norm):
    b, s, d = x.shape
    t = b * s
    depth = w_in.shape[0]
    lb_f_all = _lower_bounds(lb_fwd_logits)
    lb_b_all = _lower_bounds(lb_bwd_logits)
    xf = x.reshape(t, d)
    x_norm_in = xf
    attn_bias = _attn_bias()

    def ffn_down(a, xf, wd, l, second):
        return resid_mm(a, wd, xf, l, scale=0.5, tm=TM_DOWN, tn=TN_DOWN, second=second)

    for l in range(depth):
        a = ffn_up(x_norm_in, ffn1_norm, ffn1_w_gate, ffn1_w_up, l)
        xf, x_norm_in = ffn_down(a, xf, ffn1_w_down, l, "copy")
        p, gates = in_proj(x_norm_in, mix_norm, w_in, l)
        p3 = p.reshape(b, s, COL_GA)
        o_a = hgrn2_mixer(p3, lb_f_all[l], lb_b_all[l], rg_out_norm[l], heads=RG_HEADS)
        o_b = window_gqa(p3, attn_sink[l], attn_bias)
        merged = merge_branches(o_a.reshape(t, RG_WIDTH), o_b.reshape(t, ATT_WIDTH),
                                w_branch_a, w_branch_b, gates, l)
        xf, h = resid_mm(merged, w_out, xf, l, scale=1.0, tm=TM_OUT, tn=TN_OUT, second="norm",
                         norm_gain=ffn2_norm)
        a = ffn_up(h, None, ffn2_w_gate, ffn2_w_up, l)
        if l + 1 < depth:
            xf, x_norm_in = ffn_down(a, xf, ffn2_w_down, l, "copy")
        else:
            xf = ffn_down(a, xf, ffn2_w_down, l, None)
    return final_rmsnorm(xf, final_norm).reshape(b, s, d)
```

```python
import functools
import math

import jax
import jax.numpy as jnp
from jax import lax
from jax.experimental import pallas as pl
from jax.experimental.pallas import tpu as pltpu

D_MODEL = 2048
RG_HEADS = 8
RG_DV = 128
RG_WIDTH = RG_HEADS * RG_DV
ATT_HEADS = 8
ATT_KV_HEADS = 2
ATT_GROUP = ATT_HEADS // ATT_KV_HEADS
ATT_HD = 128
ATT_WIDTH = ATT_HEADS * ATT_HD
KV_WIDTH = ATT_KV_HEADS * ATT_HD
WINDOW = 128
Q_BLOCK = 128
EPS = 1e-6

COL_RQ = 0
COL_RI = RG_WIDTH
COL_ZF = 2 * RG_WIDTH
COL_ZB = 3 * RG_WIDTH
COL_RG = 4 * RG_WIDTH
COL_AQ = 5 * RG_WIDTH
COL_AK = COL_AQ + ATT_WIDTH
COL_AV = COL_AK + KV_WIDTH
COL_GA = COL_AV + KV_WIDTH
COL_GB = COL_GA + D_MODEL

LANES = 128
SUBLANES = 8
HGRN_CHUNK = 128
UNROLL_GATES = 16
VMEM_LIMIT_BYTES = 56 * 1024 * 1024
LOG2E = 1.4426950408889634
LN2 = 0.6931471805599453
SAFE_LOG2_RANGE = 240.0

TM = 2048
TM_DOWN = 1024
TM_MERGE = 512
TM_OUT = 512
TN_FFN_UP = 512
TN_IN_PROJ = 512
MERGE_COLS = 512
TN_DOWN = 512
TN_OUT = 2048
TM_NORM = 1024
CAST_STRIP_ROWS = 512
NORM_STRIP_ROWS = 256
MXU_COLS = 256

BF16 = jnp.bfloat16
F32 = jnp.float32
NT_DIMS = (((1,), (1,)), ((), ()))
TN_DIMS = (((0,), (0,)), ((), ()))


def _params(*sem):
    return pltpu.CompilerParams(dimension_semantics=sem, vmem_limit_bytes=VMEM_LIMIT_BYTES)


def _rmsnorm_rows(x, gain):
    ms = jnp.mean(x * x, axis=-1, keepdims=True)
    return x * lax.rsqrt(ms + EPS) * gain


def _rmsnorm_tile_to_bf16(x_ref, gain_ref, h_ref):
    def strip(r, carry):
        rows = pl.ds(pl.multiple_of(r * NORM_STRIP_ROWS, NORM_STRIP_ROWS), NORM_STRIP_ROWS)
        h_ref[rows, :] = _rmsnorm_rows(x_ref[rows, :].astype(F32), gain_ref[...]).astype(BF16)
        return carry

    lax.fori_loop(0, x_ref.shape[0] // NORM_STRIP_ROWS, strip, 0)


def _once(block_shape, index_map):
    return pl.BlockSpec(block_shape, index_map, pipeline_mode=pl.Buffered(1))


def _token_rows_spec(x, tm, room_for_two):
    spec = pl.BlockSpec if room_for_two else _once
    return spec((tm, x.shape[1]), lambda i, j: (i, 0))


def _ffn_up_kernel(*refs, normalized):
    if normalized:
        h_ref, wg_ref, wu_ref, a_ref = refs
        h = h_ref[...]
    else:
        x_ref, gain_ref, wg_ref, wu_ref, a_ref, h_scr = refs

        @pl.when(pl.program_id(1) == 0)
        def _():
            _rmsnorm_tile_to_bf16(x_ref, gain_ref, h_scr)

        h = h_scr[...]
    for c0 in range(0, a_ref.shape[1], MXU_COLS):
        cols = slice(c0, c0 + MXU_COLS)
        g = jnp.dot(h, wg_ref[:, cols].astype(BF16), preferred_element_type=F32)
        u = jnp.dot(h, wu_ref[:, cols].astype(BF16), preferred_element_type=F32)
        a_ref[:, cols] = (g * jax.nn.sigmoid(g) * u).astype(BF16)


def ffn_up(x, gain, wg, wu, l, *, tm=TM, tn=TN_FFN_UP):
    t, d = x.shape
    f = wg.shape[2]
    normalized = gain is None
    weight_tile = pl.BlockSpec((None, d, tn), lambda i, j: (l, 0, j))
    in_specs = [_token_rows_spec(x, tm, room_for_two=normalized)]
    operands = [x]
    if not normalized:
        in_specs.append(pl.BlockSpec((None, 1, d), lambda i, j: (l, 0, 0)))
        operands.append(gain.reshape(gain.shape[0], 1, d))
    return pl.pallas_call(
        functools.partial(_ffn_up_kernel, normalized=normalized),
        grid=(t // tm, f // tn),
        in_specs=in_specs + [weight_tile, weight_tile],
        out_specs=pl.BlockSpec((tm, tn), lambda i, j: (i, j)),
        out_shape=jax.ShapeDtypeStruct((t, f), BF16),
        scratch_shapes=[] if normalized else [pltpu.VMEM((tm, d), BF16)],
        compiler_params=_params("parallel", "arbitrary"),
        name="ffn_up",
    )(*operands, wg, wu)


def _resid_mm_kernel(a_ref, w_ref, x_ref, *rest, scale, second):
    gain_ref, rest = (rest[0], rest[1:]) if second == "norm" else (None, rest)
    o_ref, *second_refs, w_scr = rest

    @pl.when(pl.program_id(1) == 0)
    def _():
        def strip(r, carry):
            rows = pl.ds(pl.multiple_of(r * CAST_STRIP_ROWS, CAST_STRIP_ROWS), CAST_STRIP_ROWS)
            w_scr[rows, :] = w_ref[rows, :].astype(BF16)
            return carry

        lax.fori_loop(0, w_ref.shape[0] // CAST_STRIP_ROWS, strip, 0)

    y = jnp.dot(a_ref[...], w_scr[...], preferred_element_type=F32)
    out = x_ref[...] + (y if scale == 1.0 else scale * y)
    o_ref[...] = out
    if second == "copy":
        second_refs[0][...] = out.astype(BF16)
    elif second == "norm":
        second_refs[0][...] = _rmsnorm_rows(out, gain_ref[...]).astype(BF16)


def resid_mm(a, w, x, l, *, scale, tm, tn, second=None, norm_gain=None):
    t, kdim = a.shape
    n = w.shape[2]
    assert kdim % CAST_STRIP_ROWS == 0 and (second != "norm" or tn == n)
    tile = pl.BlockSpec((tm, tn), lambda j, i: (i, j))
    shape = jax.ShapeDtypeStruct((t, n), F32)
    in_specs = [
        pl.BlockSpec((tm, kdim), lambda j, i: (i, 0)),
        _once((None, kdim, tn), lambda j, i: (l, 0, j)),
        tile,
    ]
    operands = [a, w, x]
    out_specs, out_shape = tile, shape
    if second is not None:
        out_specs, out_shape = [tile, tile], [shape, jax.ShapeDtypeStruct((t, n), BF16)]
    if second == "norm":
        in_specs.append(pl.BlockSpec((None, 1, n), lambda j, i: (l, 0, 0)))
        operands.append(norm_gain.reshape(norm_gain.shape[0], 1, n))
    return pl.pallas_call(
        functools.partial(_resid_mm_kernel, scale=scale, second=second),
        grid=(n // tn, t // tm),
        in_specs=in_specs,
        out_specs=out_specs,
        out_shape=out_shape,
        scratch_shapes=[pltpu.VMEM((kdim, tn), BF16)],
        compiler_params=_params("arbitrary", "arbitrary"),
        name="resid_mm",
    )(*operands)


def _in_proj_kernel(x_ref, gain_ref, w_ref, p_ref, gates_ref, h_scr, *, n_gate_tiles):
    j = pl.program_id(1)

    @pl.when(j == 0)
    def _():
        _rmsnorm_tile_to_bf16(x_ref, gain_ref, h_scr)

    y = jnp.dot(h_scr[...], w_ref[...].astype(BF16), preferred_element_type=F32)
    p_ref[...] = y

    @pl.when(j < n_gate_tiles)
    def _():
        gates_ref[...] = y.astype(BF16)


def in_proj(x, gain, w, l, *, tm=TM, tn=TN_IN_PROJ):
    t, d = x.shape
    n = w.shape[2]
    assert COL_GA % tn == 0 and n % tn == 0
    n_mixer_tiles = COL_GA // tn
    n_gate_tiles = n // tn - n_mixer_tiles
    return pl.pallas_call(
        functools.partial(_in_proj_kernel, n_gate_tiles=n_gate_tiles),
        grid=(t // tm, n // tn),
        in_specs=[
            _token_rows_spec(x, tm, room_for_two=x.dtype == BF16),
            pl.BlockSpec((None, 1, d), lambda i, j: (l, 0, 0)),
            pl.BlockSpec((None, d, tn),
                         lambda i, j: (l, 0, jnp.where(j < n_gate_tiles, j + n_mixer_tiles, j - n_gate_tiles))),
        ],
        out_specs=[
            pl.BlockSpec((tm, tn), lambda i, j: (i, jnp.maximum(j - n_gate_tiles, 0))),
            pl.BlockSpec((tm, tn), lambda i, j: (i, jnp.minimum(j, n_gate_tiles - 1))),
        ],
        out_shape=[jax.ShapeDtypeStruct((t, COL_GA), F32), jax.ShapeDtypeStruct((t, n - COL_GA), BF16)],
        scratch_shapes=[pltpu.VMEM((tm, d), BF16)],
        compiler_params=_params("parallel", "arbitrary"),
        name="in_proj",
    )(x, gain.reshape(gain.shape[0], 1, d), w)


def _block_ref_rows(p, h, r):
    c, n = p.shape
    blk = 2 * h
    if blk >= SUBLANES:
        p3 = p.reshape(c // blk, blk, n)
        return jnp.broadcast_to(p3[:, r:r + 1, :], p3.shape).reshape(c, n)
    pos = lax.broadcasted_iota(jnp.int32, p.shape, 0) & (blk - 1)
    out = p
    for src in range(blk):
        if src == r:
            continue
        shifted = pltpu.roll(p, (src - r) % c, axis=0)
        out = jnp.where(pos == src, shifted, out)
    return out


def _chunk_scores(q, key, a, masks_ref, mask_base, reverse):
    c, n = q.shape
    row = lax.broadcasted_iota(jnp.int32, q.shape, 0)
    p = a
    scores = None
    h = 1
    level = 0
    while h < c:
        if h >= SUBLANES:
            nb = c // (2 * h)
            p4 = p.reshape(nb, 2, h, n)
            q4 = q.reshape(nb, 2, h, n)
            k4 = key.reshape(nb, 2, h, n)
            lo, hi = p4[:, 0], p4[:, 1]
            if reverse:
                t_row = hi[:, 0:1, :]
                qk = jnp.stack([q4[:, 0] * jnp.exp(lo), k4[:, 1] * jnp.exp(t_row - hi)], axis=1)
                p = jnp.stack([lo + t_row, hi], axis=1).reshape(c, n)
            else:
                t_row = lo[:, h - 1:h, :]
                qk = jnp.stack([k4[:, 0] * jnp.exp(t_row - lo), q4[:, 1] * jnp.exp(hi)], axis=1)
                p = jnp.stack([lo, hi + t_row], axis=1).reshape(c, n)
            qk = qk.reshape(c, n).astype(BF16)
        else:
            upper = (row & h) != 0
            qside = jnp.logical_not(upper) if reverse else upper
            t_rows = _block_ref_rows(p, h, h if reverse else h - 1)
            e = jnp.where(qside, p, t_rows - p)
            qk = (jnp.where(qside, q, key) * jnp.exp(e)).astype(BF16)
            p = p + jnp.where(qside, t_rows, 0.0)
        s = lax.dot_general(qk, qk, NT_DIMS, preferred_element_type=F32)
        s = s * masks_ref[mask_base + level]
        scores = s if scores is None else scores + s
        h *= 2
        level += 1
    return scores, p


def _log2_decay_and_key(z, lb, log1m_lb, one_m_lb):
    u = jnp.exp(-jnp.abs(z))
    one_u = 1.0 + u
    r = 1.0 / one_u
    pos = z >= 0.0
    sig = jnp.where(pos, 1.0, u) * r
    key = one_m_lb * (jnp.where(pos, u, 1.0) * r)
    y2 = (log1m_lb + jnp.minimum(z, 0.0)) * LOG2E - jnp.log2(one_u)
    return jnp.maximum(jnp.log2(lb + one_m_lb * sig), y2), key


def _cumsum_rows(a, tri_bf16):
    a1 = a.astype(BF16)
    r1 = a - a1.astype(F32)
    a2 = r1.astype(BF16)
    a3 = (r1 - a2.astype(F32)).astype(BF16)
    out = jnp.dot(tri_bf16, jnp.concatenate([a1, a2, a3], axis=1), preferred_element_type=F32)
    n = a.shape[1]
    return (out[:, 2 * n:] + out[:, n:2 * n]) + out[:, :n]


def _hgrn_kernel(q_ref, v_ref, zf_ref, zb_ref, g_ref, lbf_ref, lbb_ref, gain_ref, masks_ref, tri_ref, o_ref,
                 of_scr, ob_scr, qhf_scr, qhb_scr, updf_scr, updb_scr, decf_scr, decb_scr, stf_scr, stb_scr,
                 cumf_scr, cumb_scr, keyf_scr, keyb_scr, totf_scr, totb_scr,
                 *, chunk, n_chunks, n_levels):
    c = chunk

    def rows_of(i):
        return pl.ds(pl.multiple_of(i * c, c), c)

    dirs = (
        (False, zf_ref, lbf_ref, of_scr, qhf_scr, updf_scr, decf_scr, cumf_scr, keyf_scr, totf_scr),
        (True, zb_ref, lbb_ref, ob_scr, qhb_scr, updb_scr, decb_scr, cumb_scr, keyb_scr, totb_scr),
    )

    def gates(z_ref, lb_ref, rows):
        return _log2_decay_and_key(z_ref[0, rows, :], lb_ref[0, 0:1, :], lb_ref[0, 1:2, :], lb_ref[0, 2:3, :])

    def phase0(i, lowest):
        rows = rows_of(i)
        for reverse, z_ref, lb_ref, _, _, _, dec_scr, cum_scr, key_scr, tot_scr in dirs:
            a2, key = gates(z_ref, lb_ref, rows)
            cum = _cumsum_rows(a2, tri_ref[1 if reverse else 0])
            total = cum[0:1, :] if reverse else cum[c - 1:c, :]
            cum_scr[rows, :] = cum
            key_scr[rows, :] = key
            tot_scr[i] = jnp.broadcast_to(total, (SUBLANES, LANES))
            dec_scr[i] = jnp.broadcast_to(jnp.exp2(total), (SUBLANES, LANES))
            lowest = jnp.minimum(lowest, total)
        return lowest

    lowest = lax.fori_loop(0, n_chunks, phase0, jnp.zeros((1, LANES), F32), unroll=UNROLL_GATES)
    safe = jnp.min(lowest) > -SAFE_LOG2_RANGE

    @pl.when(safe)
    def _():
        ti = lax.broadcasted_iota(jnp.int32, (c, c), 0)
        si = lax.broadcasted_iota(jnp.int32, (c, c), 1)

        def body(i, carry):
            rows = rows_of(i)
            q = q_ref[0, rows, :]
            vb = v_ref[0, rows, :].astype(BF16)
            for reverse, _, _, o_scr, qh_scr, upd_scr, _, cum_scr, key_scr, tot_scr in dirs:
                cum = cum_scr[rows, :]
                half = tot_scr[i][0:1, :] * 0.5
                qt = q * jnp.exp2(cum - half)
                kt = key_scr[rows, :] * jnp.exp2(half - cum)
                s = lax.dot_general(qt.astype(BF16), kt.astype(BF16), NT_DIMS, preferred_element_type=F32)
                s = jnp.where((si >= ti) if reverse else (si <= ti), s, 0.0)
                o_scr[rows, :] = jnp.dot(s.astype(BF16), vb, preferred_element_type=F32)
                edge = jnp.exp2(half)
                qh_scr[rows, :] = (qt * edge).astype(BF16)
                upd_scr[i] = lax.dot_general(vb, (kt * edge).astype(BF16), TN_DIMS,
                                             preferred_element_type=F32)
            return carry

        lax.fori_loop(0, n_chunks, body, 0, unroll=True)

    @pl.when(jnp.logical_not(safe))
    def _():
        def body(i, carry):
            rows = rows_of(i)
            q = q_ref[0, rows, :]
            v = v_ref[0, rows, :]
            vb = v.astype(BF16)
            for reverse, z_ref, lb_ref, o_scr, qh_scr, upd_scr, _, _, _, _ in dirs:
                a2, key = gates(z_ref, lb_ref, rows)
                scores, cum = _chunk_scores(q, key, a2 * LN2, masks_ref, n_levels if reverse else 0, reverse)
                diag = jnp.sum(q * key, axis=-1, keepdims=True)
                o_scr[rows, :] = jnp.dot(scores.astype(BF16), vb, preferred_element_type=F32) + diag * v
                total = cum[0:1, :] if reverse else cum[c - 1:c, :]
                qh_scr[rows, :] = (q * jnp.exp(cum)).astype(BF16)
                kd = (key * jnp.exp(total - cum)).astype(BF16)
                upd_scr[i] = lax.dot_general(vb, kd, TN_DIMS, preferred_element_type=F32)
            return carry

        lax.fori_loop(0, n_chunks, body, 0)

    def phase2(i, carry):
        s_f, s_b = carry
        j = n_chunks - 1 - i
        stf_scr[i] = s_f.astype(BF16)
        stb_scr[j] = s_b.astype(BF16)
        s_f = s_f * decf_scr[i][0:1, :] + updf_scr[i]
        s_b = s_b * decb_scr[j][0:1, :] + updb_scr[j]
        return s_f, s_b

    zero = jnp.zeros((LANES, LANES), F32)
    lax.fori_loop(0, n_chunks, phase2, (zero, zero))

    def phase3(i, carry):
        rows = rows_of(i)
        o = of_scr[rows, :] + ob_scr[rows, :]
        o += lax.dot_general(jnp.concatenate([qhf_scr[rows, :], qhb_scr[rows, :]], axis=1),
                             jnp.concatenate([stf_scr[i], stb_scr[i]], axis=1), NT_DIMS,
                             preferred_element_type=F32)
        o = _rmsnorm_rows(o, gain_ref[0])
        g = g_ref[0, rows, :]
        o_ref[0, rows, :] = (o * (g * jax.nn.sigmoid(g))).astype(o_ref.dtype)
        return carry

    lax.fori_loop(0, n_chunks, phase3, 0, unroll=True)


def _level_masks(chunk):
    idx = jnp.arange(chunk)
    t, s = idx[:, None], idx[None, :]
    fwd, bwd = [], []
    h = 1
    while h < chunk:
        same = (t // (2 * h)) == (s // (2 * h))
        t_up = (t & h) != 0
        s_up = (s & h) != 0
        fwd.append(same & t_up & ~s_up)
        bwd.append(same & ~t_up & s_up)
        h *= 2
    return jnp.stack(fwd + bwd).astype(F32)


def hgrn2_mixer(p, lb_f, lb_b, out_gain, *, heads, chunk=HGRN_CHUNK):
    b, s, _ = p.shape
    n_chunks = s // chunk
    n_levels = int(math.log2(chunk))

    def lb_rows(lb):
        lb = lb.astype(F32).reshape(heads, 1, LANES)
        rows = jnp.concatenate([lb, jnp.log1p(-lb), 1.0 - lb], axis=1)
        return jnp.pad(rows, ((0, 0), (0, SUBLANES - 3), (0, 0)))

    def col(block0):
        return pl.BlockSpec((1, s, LANES), lambda bi, hi: (bi, 0, block0 + hi))

    per_head = pl.BlockSpec((1, SUBLANES, LANES), lambda bi, hi: (hi, 0, 0))
    masks = _level_masks(chunk)
    idx = jnp.arange(chunk)
    lower = idx[None, :] <= idx[:, None]
    tri = jnp.stack([lower, lower.T]).astype(BF16)
    kern = functools.partial(_hgrn_kernel, chunk=chunk, n_chunks=n_chunks, n_levels=n_levels)
    seq_f32 = pltpu.VMEM((s, LANES), F32)
    seq_bf16 = pltpu.VMEM((s, LANES), BF16)
    chunk_mat_f32 = pltpu.VMEM((n_chunks, LANES, LANES), F32)
    chunk_mat_bf16 = pltpu.VMEM((n_chunks, LANES, LANES), BF16)
    chunk_row = pltpu.VMEM((n_chunks, SUBLANES, LANES), F32)
    return pl.pallas_call(
        kern,
        grid=(b, heads),
        in_specs=[
            col(COL_RQ // LANES), col(COL_RI // LANES), col(COL_ZF // LANES), col(COL_ZB // LANES),
            col(COL_RG // LANES), per_head, per_head,
            pl.BlockSpec((1, 1, LANES), lambda bi, hi: (hi, 0, 0)),
            pl.BlockSpec(masks.shape, lambda bi, hi: (0, 0, 0)),
            pl.BlockSpec(tri.shape, lambda bi, hi: (0, 0, 0)),
        ],
        out_specs=pl.BlockSpec((1, s, LANES), lambda bi, hi: (bi, 0, hi)),
        out_shape=jax.ShapeDtypeStruct((b, s, heads * LANES), BF16),
        scratch_shapes=[
            seq_f32, seq_f32, seq_bf16, seq_bf16,
            chunk_mat_f32, chunk_mat_f32, chunk_row, chunk_row,
            chunk_mat_bf16, chunk_mat_bf16,
            seq_f32, seq_f32, seq_f32, seq_f32, chunk_row, chunk_row,
        ],
        compiler_params=_params("parallel", "parallel"),
        name="hgrn2",
    )(p, p, p, p, p, lb_rows(lb_f), lb_rows(lb_b), out_gain.astype(F32).reshape(heads, 1, LANES), masks, tri)


ATT_SUB = 2


def _attn_kernel(sink_ref, q_ref, k0_ref, k1_ref, k2_ref, k3_ref, v0_ref, v1_ref, v2_ref, v3_ref,
                 bias0_ref, bias1_ref, o_ref):
    qscale = LOG2E / math.sqrt(ATT_HD)
    qb = Q_BLOCK
    k_refs = (k0_ref, k1_ref, k2_ref, k3_ref)
    v_refs = (v0_ref, v1_ref, v2_ref, v3_ref)

    def head_cols(ref, head, rows=slice(None)):
        return ref[0, rows, head * ATT_HD:(head + 1) * ATT_HD]

    def band(refs, first, kvh):
        return jnp.concatenate([head_cols(r, kvh) for r in refs[first:first + 3]], axis=0).astype(BF16)

    for sub, bias_ref in enumerate((bias0_ref, bias1_ref)):
        rows = slice(sub * qb, (sub + 1) * qb)
        for kvh in range(ATT_KV_HEADS):
            k = band(k_refs, sub, kvh)
            v1 = jnp.concatenate([band(v_refs, sub, kvh), jnp.ones((3 * qb, ATT_HD), BF16)], axis=1)
            for g in range(ATT_GROUP):
                head = kvh * ATT_GROUP + g
                q = (head_cols(q_ref, head, rows) * qscale).astype(BF16)
                s = lax.dot_general(q, k, NT_DIMS, preferred_element_type=F32)
                s = s + bias_ref[0, head * qb:(head + 1) * qb]
                sink = sink_ref[head] * LOG2E
                m = jnp.broadcast_to(jnp.maximum(jnp.max(s, axis=-1, keepdims=True), sink), (qb, ATT_HD))
                e = jnp.exp2(s - jnp.concatenate([m, m, m], axis=1)).astype(BF16)
                ov = jnp.dot(e, v1, preferred_element_type=F32)
                o = ov[:, :ATT_HD] / (ov[:, ATT_HD:] + jnp.exp2(sink - m))
                o_ref[0, rows, head * ATT_HD:(head + 1) * ATT_HD] = o.astype(o_ref.dtype)


def _attn_bias():
    qi = jnp.arange(Q_BLOCK)[:, None]
    kj = jnp.arange(3 * Q_BLOCK)[None, :]
    dist = jnp.abs(kj - Q_BLOCK - qi)
    slopes = 2.0 ** (-8.0 * jnp.arange(1, ATT_HEADS + 1, dtype=F32) / ATT_HEADS)
    alibi = -(slopes[:, None, None] * dist.astype(F32)[None]) * LOG2E
    cases = []
    for has_prev, has_next in ((False, True), (True, True), (True, False)):
        valid = (dist <= WINDOW) & ((kj >= Q_BLOCK) | has_prev) & ((kj < 2 * Q_BLOCK) | has_next)
        cases.append(jnp.where(valid[None], alibi, -jnp.inf).reshape(ATT_HEADS * Q_BLOCK, 3 * Q_BLOCK))
    return jnp.stack(cases)


def window_gqa(p, sink, bias):
    b, s, _ = p.shape
    nb = s // Q_BLOCK
    assert nb >= 2 and nb % ATT_SUB == 0
    kcol, vcol = COL_AK // KV_WIDTH, COL_AV // KV_WIDTH

    def kv(col, off):
        return pl.BlockSpec((1, Q_BLOCK, KV_WIDTH),
                            lambda bi, n, sk: (bi, jnp.clip(ATT_SUB * n - 1 + off, 0, nb - 1), col))

    def bias_spec(sub):
        def case(bi, n, sk):
            blk = ATT_SUB * n + sub
            return (jnp.where(blk == 0, 0, jnp.where(blk == nb - 1, 2, 1)), 0, 0)
        return pl.BlockSpec((1, ATT_HEADS * Q_BLOCK, 3 * Q_BLOCK), case)

    rows = ATT_SUB * Q_BLOCK
    n_band = ATT_SUB + 2
    return pl.pallas_call(
        _attn_kernel,
        grid_spec=pltpu.PrefetchScalarGridSpec(
            num_scalar_prefetch=1,
            grid=(b, nb // ATT_SUB),
            in_specs=[pl.BlockSpec((1, rows, ATT_WIDTH), lambda bi, n, sk: (bi, n, COL_AQ // ATT_WIDTH))]
            + [kv(kcol, off) for off in range(n_band)]
            + [kv(vcol, off) for off in range(n_band)]
            + [bias_spec(sub) for sub in range(ATT_SUB)],
            out_specs=pl.BlockSpec((1, rows, ATT_WIDTH), lambda bi, n, sk: (bi, n, 0)),
        ),
        out_shape=jax.ShapeDtypeStruct((b, s, ATT_WIDTH), BF16),
        compiler_params=_params("parallel", "arbitrary"),
        name="window_gqa",
    )(sink.astype(F32), *([p] * (1 + 2 * n_band)), *([bias] * ATT_SUB))


def _token_mixers_kernel(sink_ref, *refs, n_hgrn_in, n_attn_in, hgrn_static):
    hgrn_in = refs[:n_hgrn_in]
    attn_in = refs[n_hgrn_in:n_hgrn_in + n_attn_in]
    oa_ref, ob_ref = refs[n_hgrn_in + n_attn_in:n_hgrn_in + n_attn_in + 2]
    scratch = refs[n_hgrn_in + n_attn_in + 2:]
    _hgrn_kernel(*hgrn_in, oa_ref, *scratch, **hgrn_static)
    _attn_kernel(sink_ref, *attn_in, ob_ref)


def token_mixers(p, lb_f, lb_b, out_gain, sink, bias, *, heads=RG_HEADS, chunk=HGRN_CHUNK):
    b, s, _ = p.shape
    n_chunks = s // chunk
    n_levels = int(math.log2(chunk))
    nb = s // Q_BLOCK
    assert nb >= 2 and nb == ATT_SUB * heads

    def lb_rows(lb):
        lb = lb.astype(F32).reshape(heads, 1, LANES)
        rows = jnp.concatenate([lb, jnp.log1p(-lb), 1.0 - lb], axis=1)
        return jnp.pad(rows, ((0, 0), (0, SUBLANES - 3), (0, 0)))

    def col(block0):
        return pl.BlockSpec((1, s, LANES), lambda bi, hi, sk: (bi, 0, block0 + hi))

    per_head = pl.BlockSpec((1, SUBLANES, LANES), lambda bi, hi, sk: (hi, 0, 0))
    masks = _level_masks(chunk)
    idx = jnp.arange(chunk)
    lower = idx[None, :] <= idx[:, None]
    tri = jnp.stack([lower, lower.T]).astype(BF16)
    hgrn_specs = [
        col(COL_RQ // LANES), col(COL_RI // LANES), col(COL_ZF // LANES), col(COL_ZB // LANES),
        col(COL_RG // LANES), per_head, per_head,
        pl.BlockSpec((1, 1, LANES), lambda bi, hi, sk: (hi, 0, 0)),
        pl.BlockSpec(masks.shape, lambda bi, hi, sk: (0, 0, 0)),
        pl.BlockSpec(tri.shape, lambda bi, hi, sk: (0, 0, 0)),
    ]
    hgrn_args = [p, p, p, p, p, lb_rows(lb_f), lb_rows(lb_b), out_gain.astype(F32).reshape(heads, 1, LANES), masks, tri]

    kcol, vcol = COL_AK // KV_WIDTH, COL_AV // KV_WIDTH

    def kv(colb, off):
        return pl.BlockSpec((1, Q_BLOCK, KV_WIDTH),
                            lambda bi, n, sk: (bi, jnp.clip(ATT_SUB * n - 1 + off, 0, nb - 1), colb))

    def bias_spec(sub):
        def case(bi, n, sk):
            blk = ATT_SUB * n + sub
            return (jnp.where(blk == 0, 0, jnp.where(blk == nb - 1, 2, 1)), 0, 0)
        return pl.BlockSpec((1, ATT_HEADS * Q_BLOCK, 3 * Q_BLOCK), case)

    rows = ATT_SUB * Q_BLOCK
    n_band = ATT_SUB + 2
    attn_specs = ([pl.BlockSpec((1, rows, ATT_WIDTH), lambda bi, n, sk: (bi, n, COL_AQ // ATT_WIDTH))]
                  + [kv(kcol, off) for off in range(n_band)] + [kv(vcol, off) for off in range(n_band)]
                  + [bias_spec(sub) for sub in range(ATT_SUB)])
    attn_args = [p] * (1 + 2 * n_band) + [bias] * ATT_SUB

    seq_f32 = pltpu.VMEM((s, LANES), F32)
    seq_bf16 = pltpu.VMEM((s, LANES), BF16)
    chunk_mat_f32 = pltpu.VMEM((n_chunks, LANES, LANES), F32)
    chunk_mat_bf16 = pltpu.VMEM((n_chunks, LANES, LANES), BF16)
    chunk_row = pltpu.VMEM((n_chunks, SUBLANES, LANES), F32)
    kern = functools.partial(_token_mixers_kernel, n_hgrn_in=len(hgrn_specs), n_attn_in=len(attn_specs),
                             hgrn_static=dict(chunk=chunk, n_chunks=n_chunks, n_levels=n_levels))
    return pl.pallas_call(
        kern,
        grid_spec=pltpu.PrefetchScalarGridSpec(
            num_scalar_prefetch=1,
            grid=(b, heads),
            in_specs=hgrn_specs + attn_specs,
            out_specs=[pl.BlockSpec((1, s, LANES), lambda bi, hi, sk: (bi, 0, hi)),
                       pl.BlockSpec((1, rows, ATT_WIDTH), lambda bi, n, sk: (bi, n, 0))],
            scratch_shapes=[
                seq_f32, seq_f32, seq_bf16, seq_bf16,
                chunk_mat_f32, chunk_mat_f32, chunk_row, chunk_row,
                chunk_mat_bf16, chunk_mat_bf16,
                seq_f32, seq_f32, seq_f32, seq_f32, chunk_row, chunk_row,
            ],
        ),
        out_shape=[jax.ShapeDtypeStruct((b, s, heads * LANES), BF16), jax.ShapeDtypeStruct((b, s, ATT_WIDTH), BF16)],
        compiler_params=_params("parallel", "arbitrary"),
        name="token_mixers",
    )(sink.astype(F32), *hgrn_args, *attn_args)


def _merge_kernel(oa_ref, ob_ref, wa_ref, wb_ref, ga_ref, gb_ref, m_ref, wa_scr, wb_scr):
    @pl.when(pl.program_id(0) == 0)
    def _():
        wa_scr[...] = wa_ref[...].astype(BF16)
        wb_scr[...] = wb_ref[...].astype(BF16)

    oa = oa_ref[...]
    ob = ob_ref[...]
    for c0 in range(0, m_ref.shape[1], MERGE_COLS):
        cols = slice(c0, c0 + MERGE_COLS)
        ya = jnp.dot(oa, wa_scr[:, cols], preferred_element_type=F32)
        yb = jnp.dot(ob, wb_scr[:, cols], preferred_element_type=F32)
        ga = jax.nn.sigmoid(ga_ref[:, cols].astype(F32))
        gb = jax.nn.sigmoid(gb_ref[:, cols].astype(F32))
        m_ref[:, cols] = (ga * ya + gb * yb).astype(m_ref.dtype)


def merge_branches(oa, ob, wa, wb, gates, l, *, tm=TM_MERGE):
    t, ka = oa.shape
    kb = ob.shape[1]
    n = wa.shape[2]
    return pl.pallas_call(
        _merge_kernel,
        grid=(t // tm,),
        in_specs=[
            pl.BlockSpec((tm, ka), lambda i: (i, 0)),
            pl.BlockSpec((tm, kb), lambda i: (i, 0)),
            _once((None, ka, n), lambda i: (l, 0, 0)),
            _once((None, kb, n), lambda i: (l, 0, 0)),
            pl.BlockSpec((tm, n), lambda i: (i, 0)),
            pl.BlockSpec((tm, n), lambda i: (i, 1)),
        ],
        out_specs=pl.BlockSpec((tm, n), lambda i: (i, 0)),
        out_shape=jax.ShapeDtypeStruct((t, n), BF16),
        scratch_shapes=[pltpu.VMEM((ka, n), BF16), pltpu.VMEM((kb, n), BF16)],
        compiler_params=_params("arbitrary"),
        name="merge_branches",
    )(oa, ob, wa, wb, gates, gates)


def _final_norm_kernel(x_ref, gain_ref, o_ref):
    o_ref[...] = _rmsnorm_rows(x_ref[...], gain_ref[...])


def final_rmsnorm(x, gain, *, tm=TM_NORM):
    t, d = x.shape
    return pl.pallas_call(
        _final_norm_kernel,
        grid=(t // tm,),
        in_specs=[pl.BlockSpec((tm, d), lambda i: (i, 0)), pl.BlockSpec((1, d), lambda i: (0, 0))],
        out_specs=pl.BlockSpec((tm, d), lambda i: (i, 0)),
        out_shape=jax.ShapeDtypeStruct((t, d), F32),
        compiler_params=_params("parallel"),
        name="final_norm",
    )(x, gain.reshape(1, d))


def _lower_bounds(lb_logits):
    lb = jnp.cumsum(jax.nn.softmax(lb_logits.astype(F32), axis=0), axis=0)
    return lb - lb[0:1]


def kernel(x, ffn1_norm, ffn1_w_gate, ffn1_w_up, ffn1_w_down, mix_norm, w_in, lb_fwd_logits, lb_bwd_logits, rg_out_norm, attn_sink, w_branch_a, w_branch_b, w_out, ffn2_norm, ffn2_w_gate, ffn2_w_up, ffn2_w_down, final_norm):
    b, s, d = x.shape
    t = b * s
    depth = w_in.shape[0]
    lb_f_all = _lower_bounds(lb_fwd_logits)
    lb_b_all = _lower_bounds(lb_bwd_logits)
    xf = x.reshape(t, d)
    x_norm_in = xf
    attn_bias = _attn_bias()

    def ffn_down(a, xf, wd, l, second):
        return resid_mm(a, wd, xf, l, scale=0.5, tm=TM_DOWN, tn=TN_DOWN, second=second)

    for l in range(depth):
        a = ffn_up(x_norm_in, ffn1_norm, ffn1_w_gate, ffn1_w_up, l)
        xf, x_norm_in = ffn_down(a, xf, ffn1_w_down, l, "copy")
        p, gates = in_proj(x_norm_in, mix_norm, w_in, l)
        p3 = p.reshape(b, s, COL_GA)
        o_a, o_b = token_mixers(p3, lb_f_all[l], lb_b_all[l], rg_out_norm[l], attn_sink[l], attn_bias)
        merged = merge_branches(o_a.reshape(t, RG_WIDTH), o_b.reshape(t, ATT_WIDTH),
                                w_branch_a, w_branch_b, gates, l)
        xf, h = resid_mm(merged, w_out, xf, l, scale=1.0, tm=TM_OUT, tn=TN_OUT, second="norm",
                         norm_gain=ffn2_norm)
        a = ffn_up(h, None, ffn2_w_gate, ffn2_w_up, l)
        if l + 1 < depth:
            xf, x_norm_in = ffn_down(a, xf, ffn2_w_down, l, "copy")
        else:
            xf = ffn_down(a, xf, ffn2_w_down, l, None)
    return final_rmsnorm(xf, final_norm).reshape(b, s, d)
```

```python
import functools
import math

import jax
import jax.numpy as jnp
from jax import lax
from jax.experimental import pallas as pl
from jax.experimental.pallas import tpu as pltpu

D_MODEL = 2048
RG_HEADS = 8
RG_DV = 128
RG_WIDTH = RG_HEADS * RG_DV
ATT_HEADS = 8
ATT_KV_HEADS = 2
ATT_GROUP = ATT_HEADS // ATT_KV_HEADS
ATT_HD = 128
ATT_WIDTH = ATT_HEADS * ATT_HD
KV_WIDTH = ATT_KV_HEADS * ATT_HD
WINDOW = 128
Q_BLOCK = 128
EPS = 1e-6

COL_RQ = 0
COL_RI = RG_WIDTH
COL_ZF = 2 * RG_WIDTH
COL_ZB = 3 * RG_WIDTH
COL_RG = 4 * RG_WIDTH
COL_AQ = 5 * RG_WIDTH
COL_AK = COL_AQ + ATT_WIDTH
COL_AV = COL_AK + KV_WIDTH
COL_GA = COL_AV + KV_WIDTH
COL_GB = COL_GA + D_MODEL

LANES = 128
SUBLANES = 8
HGRN_CHUNK = 128
UNROLL_GATES = 16
VMEM_LIMIT_BYTES = 56 * 1024 * 1024
LOG2E = 1.4426950408889634
LN2 = 0.6931471805599453
SAFE_LOG2_RANGE = 240.0

TM = 2048
TM_DOWN = 1024
TM_MERGE = 512
TM_OUT = 512
TN_FFN_UP = 512
TN_IN_PROJ = 512
MERGE_COLS = 512
TN_DOWN = 512
TN_OUT = 2048
TM_NORM = 1024
CAST_STRIP_ROWS = 512
NORM_STRIP_ROWS = 256
MXU_COLS = 256

BF16 = jnp.bfloat16
F32 = jnp.float32
NT_DIMS = (((1,), (1,)), ((), ()))
TN_DIMS = (((0,), (0,)), ((), ()))


def _params(*sem):
    return pltpu.CompilerParams(dimension_semantics=sem, vmem_limit_bytes=VMEM_LIMIT_BYTES)


def _rmsnorm_rows(x, gain):
    ms = jnp.mean(x * x, axis=-1, keepdims=True)
    return x * lax.rsqrt(ms + EPS) * gain


def _rmsnorm_tile_to_bf16(x_ref, gain_ref, h_ref):
    def strip(r, carry):
        rows = pl.ds(pl.multiple_of(r * NORM_STRIP_ROWS, NORM_STRIP_ROWS), NORM_STRIP_ROWS)
        h_ref[rows, :] = _rmsnorm_rows(x_ref[rows, :].astype(F32), gain_ref[...]).astype(BF16)
        return carry

    lax.fori_loop(0, x_ref.shape[0] // NORM_STRIP_ROWS, strip, 0)


def _once(block_shape, index_map):
    return pl.BlockSpec(block_shape, index_map, pipeline_mode=pl.Buffered(1))


def _token_rows_spec(x, tm, room_for_two):
    spec = pl.BlockSpec if room_for_two else _once
    return spec((tm, x.shape[1]), lambda i, j: (i, 0))


def _ffn_up_kernel(*refs, normalized):
    if normalized:
        h_ref, wg_ref, wu_ref, a_ref = refs
        h = h_ref[...]
    else:
        x_ref, gain_ref, wg_ref, wu_ref, a_ref, h_scr = refs

        @pl.when(pl.program_id(1) == 0)
        def _():
            _rmsnorm_tile_to_bf16(x_ref, gain_ref, h_scr)

        h = h_scr[...]
    for c0 in range(0, a_ref.shape[1], MXU_COLS):
        cols = slice(c0, c0 + MXU_COLS)
        g = jnp.dot(h, wg_ref[:, cols].astype(BF16), preferred_element_type=F32)
        u = jnp.dot(h, wu_ref[:, cols].astype(BF16), preferred_element_type=F32)
        a_ref[:, cols] = (g * jax.nn.sigmoid(g) * u).astype(BF16)


def ffn_up(x, gain, wg, wu, l, *, tm=TM, tn=TN_FFN_UP):
    t, d = x.shape
    f = wg.shape[2]
    normalized = gain is None
    weight_tile = pl.BlockSpec((None, d, tn), lambda i, j: (l, 0, j))
    in_specs = [_token_rows_spec(x, tm, room_for_two=normalized)]
    operands = [x]
    if not normalized:
        in_specs.append(pl.BlockSpec((None, 1, d), lambda i, j: (l, 0, 0)))
        operands.append(gain.reshape(gain.shape[0], 1, d))
    return pl.pallas_call(
        functools.partial(_ffn_up_kernel, normalized=normalized),
        grid=(t // tm, f // tn),
        in_specs=in_specs + [weight_tile, weight_tile],
        out_specs=pl.BlockSpec((tm, tn), lambda i, j: (i, j)),
        out_shape=jax.ShapeDtypeStruct((t, f), BF16),
        scratch_shapes=[] if normalized else [pltpu.VMEM((tm, d), BF16)],
        compiler_params=_params("parallel", "arbitrary"),
        name="ffn_up",
    )(*operands, wg, wu)


def _resid_mm_kernel(a_ref, w_ref, x_ref, *rest, scale, second):
    gain_ref, rest = (rest[0], rest[1:]) if second == "norm" else (None, rest)
    o_ref, *second_refs, w_scr = rest

    @pl.when(pl.program_id(1) == 0)
    def _():
        def strip(r, carry):
            rows = pl.ds(pl.multiple_of(r * CAST_STRIP_ROWS, CAST_STRIP_ROWS), CAST_STRIP_ROWS)
            w_scr[rows, :] = w_ref[rows, :].astype(BF16)
            return carry

        lax.fori_loop(0, w_ref.shape[0] // CAST_STRIP_ROWS, strip, 0)

    y = jnp.dot(a_ref[...], w_scr[...], preferred_element_type=F32)
    out = x_ref[...] + (y if scale == 1.0 else scale * y)
    o_ref[...] = out
    if second == "copy":
        second_refs[0][...] = out.astype(BF16)
    elif second == "norm":
        second_refs[0][...] = _rmsnorm_rows(out, gain_ref[...]).astype(BF16)


def resid_mm(a, w, x, l, *, scale, tm, tn, second=None, norm_gain=None):
    t, kdim = a.shape
    n = w.shape[2]
    assert kdim % CAST_STRIP_ROWS == 0 and (second != "norm" or tn == n)
    tile = pl.BlockSpec((tm, tn), lambda j, i: (i, j))
    shape = jax.ShapeDtypeStruct((t, n), F32)
    in_specs = [
        pl.BlockSpec((tm, kdim), lambda j, i: (i, 0)),
        _once((None, kdim, tn), lambda j, i: (l, 0, j)),
        tile,
    ]
    operands = [a, w, x]
    out_specs, out_shape = tile, shape
    if second is not None:
        out_specs, out_shape = [tile, tile], [shape, jax.ShapeDtypeStruct((t, n), BF16)]
    if second == "norm":
        in_specs.append(pl.BlockSpec((None, 1, n), lambda j, i: (l, 0, 0)))
        operands.append(norm_gain.reshape(norm_gain.shape[0], 1, n))
    return pl.pallas_call(
        functools.partial(_resid_mm_kernel, scale=scale, second=second),
        grid=(n // tn, t // tm),
        in_specs=in_specs,
        out_specs=out_specs,
        out_shape=out_shape,
        scratch_shapes=[pltpu.VMEM((kdim, tn), BF16)],
        compiler_params=_params("arbitrary", "arbitrary"),
        name="resid_mm",
    )(*operands)


def _in_proj_kernel(x_ref, gain_ref, w_ref, p_ref, gates_ref, h_scr, *, n_gate_tiles):
    j = pl.program_id(1)

    @pl.when(j == 0)
    def _():
        _rmsnorm_tile_to_bf16(x_ref, gain_ref, h_scr)

    y = jnp.dot(h_scr[...], w_ref[...].astype(BF16), preferred_element_type=F32)
    p_ref[...] = y

    @pl.when(j < n_gate_tiles)
    def _():
        gates_ref[...] = y.astype(BF16)


def in_proj(x, gain, w, l, *, tm=TM, tn=TN_IN_PROJ):
    t, d = x.shape
    n = w.shape[2]
    assert COL_GA % tn == 0 and n % tn == 0
    n_mixer_tiles = COL_GA // tn
    n_gate_tiles = n // tn - n_mixer_tiles
    return pl.pallas_call(
        functools.partial(_in_proj_kernel, n_gate_tiles=n_gate_tiles),
        grid=(t // tm, n // tn),
        in_specs=[
            _token_rows_spec(x, tm, room_for_two=x.dtype == BF16),
            pl.BlockSpec((None, 1, d), lambda i, j: (l, 0, 0)),
            pl.BlockSpec((None, d, tn),
                         lambda i, j: (l, 0, jnp.where(j < n_gate_tiles, j + n_mixer_tiles, j - n_gate_tiles))),
        ],
        out_specs=[
            pl.BlockSpec((tm, tn), lambda i, j: (i, jnp.maximum(j - n_gate_tiles, 0))),
            pl.BlockSpec((tm, tn), lambda i, j: (i, jnp.minimum(j, n_gate_tiles - 1))),
        ],
        out_shape=[jax.ShapeDtypeStruct((t, COL_GA), F32), jax.ShapeDtypeStruct((t, n - COL_GA), BF16)],
        scratch_shapes=[pltpu.VMEM((tm, d), BF16)],
        compiler_params=_params("parallel", "arbitrary"),
        name="in_proj",
    )(x, gain.reshape(gain.shape[0], 1, d), w)


def _block_ref_rows(p, h, r):
    c, n = p.shape
    blk = 2 * h
    if blk >= SUBLANES:
        p3 = p.reshape(c // blk, blk, n)
        return jnp.broadcast_to(p3[:, r:r + 1, :], p3.shape).reshape(c, n)
    pos = lax.broadcasted_iota(jnp.int32, p.shape, 0) & (blk - 1)
    out = p
    for src in range(blk):
        if src == r:
            continue
        shifted = pltpu.roll(p, (src - r) % c, axis=0)
        out = jnp.where(pos == src, shifted, out)
    return out


def _chunk_scores(q, key, a, masks_ref, mask_base, reverse):
    c, n = q.shape
    row = lax.broadcasted_iota(jnp.int32, q.shape, 0)
    p = a
    scores = None
    h = 1
    level = 0
    while h < c:
        if h >= SUBLANES:
            nb = c // (2 * h)
            p4 = p.reshape(nb, 2, h, n)
            q4 = q.reshape(nb, 2, h, n)
            k4 = key.reshape(nb, 2, h, n)
            lo, hi = p4[:, 0], p4[:, 1]
            if reverse:
                t_row = hi[:, 0:1, :]
                qk = jnp.stack([q4[:, 0] * jnp.exp(lo), k4[:, 1] * jnp.exp(t_row - hi)], axis=1)
                p = jnp.stack([lo + t_row, hi], axis=1).reshape(c, n)
            else:
                t_row = lo[:, h - 1:h, :]
                qk = jnp.stack([k4[:, 0] * jnp.exp(t_row - lo), q4[:, 1] * jnp.exp(hi)], axis=1)
                p = jnp.stack([lo, hi + t_row], axis=1).reshape(c, n)
            qk = qk.reshape(c, n).astype(BF16)
        else:
            upper = (row & h) != 0
            qside = jnp.logical_not(upper) if reverse else upper
            t_rows = _block_ref_rows(p, h, h if reverse else h - 1)
            e = jnp.where(qside, p, t_rows - p)
            qk = (jnp.where(qside, q, key) * jnp.exp(e)).astype(BF16)
            p = p + jnp.where(qside, t_rows, 0.0)
        s = lax.dot_general(qk, qk, NT_DIMS, preferred_element_type=F32)
        s = s * masks_ref[mask_base + level]
        scores = s if scores is None else scores + s
        h *= 2
        level += 1
    return scores, p


def _log2_decay_and_key(z, lb, log1m_lb, one_m_lb):
    u = jnp.exp(-jnp.abs(z))
    one_u = 1.0 + u
    r = 1.0 / one_u
    pos = z >= 0.0
    sig = jnp.where(pos, 1.0, u) * r
    key = one_m_lb * (jnp.where(pos, u, 1.0) * r)
    y2 = (log1m_lb + jnp.minimum(z, 0.0)) * LOG2E - jnp.log2(one_u)
    return jnp.maximum(jnp.log2(lb + one_m_lb * sig), y2), key


def _cumsum_rows(a, tri_bf16):
    a1 = a.astype(BF16)
    a2 = (a - a1.astype(F32)).astype(BF16)
    out = jnp.dot(tri_bf16, jnp.concatenate([a1, a2], axis=1), preferred_element_type=F32)
    n = a.shape[1]
    return out[:, n:] + out[:, :n]


def _hgrn_kernel(q_ref, v_ref, zf_ref, zb_ref, g_ref, lbf_ref, lbb_ref, gain_ref, masks_ref, tri_ref, o_ref,
                 of_scr, ob_scr, qhf_scr, qhb_scr, updf_scr, updb_scr, decf_scr, decb_scr, stf_scr, stb_scr,
                 cumf_scr, cumb_scr, keyf_scr, keyb_scr, totf_scr, totb_scr,
                 *, chunk, n_chunks, n_levels):
    c = chunk

    def rows_of(i):
        return pl.ds(pl.multiple_of(i * c, c), c)

    dirs = (
        (False, zf_ref, lbf_ref, of_scr, qhf_scr, updf_scr, decf_scr, cumf_scr, keyf_scr, totf_scr),
        (True, zb_ref, lbb_ref, ob_scr, qhb_scr, updb_scr, decb_scr, cumb_scr, keyb_scr, totb_scr),
    )

    def gates(z_ref, lb_ref, rows):
        return _log2_decay_and_key(z_ref[0, rows, :], lb_ref[0, 0:1, :], lb_ref[0, 1:2, :], lb_ref[0, 2:3, :])

    def phase0(i, lowest):
        rows = rows_of(i)
        for reverse, z_ref, lb_ref, _, _, _, dec_scr, cum_scr, key_scr, tot_scr in dirs:
            a2, key = gates(z_ref, lb_ref, rows)
            cum = _cumsum_rows(a2, tri_ref[1 if reverse else 0])
            total = cum[0:1, :] if reverse else cum[c - 1:c, :]
            cum_scr[rows, :] = cum
            key_scr[rows, :] = key
            tot_scr[i] = jnp.broadcast_to(total, (SUBLANES, LANES))
            dec_scr[i] = jnp.broadcast_to(jnp.exp2(total), (SUBLANES, LANES))
            lowest = jnp.minimum(lowest, total)
        return lowest

    lowest = lax.fori_loop(0, n_chunks, phase0, jnp.zeros((1, LANES), F32), unroll=UNROLL_GATES)
    safe = jnp.min(lowest) > -SAFE_LOG2_RANGE

    @pl.when(safe)
    def _():
        ti = lax.broadcasted_iota(jnp.int32, (c, c), 0)
        si = lax.broadcasted_iota(jnp.int32, (c, c), 1)

        def body(i, carry):
            rows = rows_of(i)
            q = q_ref[0, rows, :]
            vb = v_ref[0, rows, :].astype(BF16)
            for reverse, _, _, o_scr, qh_scr, upd_scr, _, cum_scr, key_scr, tot_scr in dirs:
                cum = cum_scr[rows, :]
                half = tot_scr[i][0:1, :] * 0.5
                qt = q * jnp.exp2(cum - half)
                kt = key_scr[rows, :] * jnp.exp2(half - cum)
                s = lax.dot_general(qt.astype(BF16), kt.astype(BF16), NT_DIMS, preferred_element_type=F32)
                s = jnp.where((si >= ti) if reverse else (si <= ti), s, 0.0)
                o_scr[rows, :] = jnp.dot(s.astype(BF16), vb, preferred_element_type=F32)
                edge = jnp.exp2(half)
                qh_scr[rows, :] = (qt * edge).astype(BF16)
                upd_scr[i] = lax.dot_general(vb, (kt * edge).astype(BF16), TN_DIMS,
                                             preferred_element_type=F32)
            return carry

        lax.fori_loop(0, n_chunks, body, 0, unroll=True)

    @pl.when(jnp.logical_not(safe))
    def _():
        def body(i, carry):
            rows = rows_of(i)
            q = q_ref[0, rows, :]
            v = v_ref[0, rows, :]
            vb = v.astype(BF16)
            for reverse, z_ref, lb_ref, o_scr, qh_scr, upd_scr, _, _, _, _ in dirs:
                a2, key = gates(z_ref, lb_ref, rows)
                scores, cum = _chunk_scores(q, key, a2 * LN2, masks_ref, n_levels if reverse else 0, reverse)
                diag = jnp.sum(q * key, axis=-1, keepdims=True)
                o_scr[rows, :] = jnp.dot(scores.astype(BF16), vb, preferred_element_type=F32) + diag * v
                total = cum[0:1, :] if reverse else cum[c - 1:c, :]
                qh_scr[rows, :] = (q * jnp.exp(cum)).astype(BF16)
                kd = (key * jnp.exp(total - cum)).astype(BF16)
                upd_scr[i] = lax.dot_general(vb, kd, TN_DIMS, preferred_element_type=F32)
            return carry

        lax.fori_loop(0, n_chunks, body, 0)

    def phase2(i, carry):
        s_f, s_b = carry
        j = n_chunks - 1 - i
        stf_scr[i] = s_f.astype(BF16)
        stb_scr[j] = s_b.astype(BF16)
        s_f = s_f * decf_scr[i][0:1, :] + updf_scr[i]
        s_b = s_b * decb_scr[j][0:1, :] + updb_scr[j]
        return s_f, s_b

    zero = jnp.zeros((LANES, LANES), F32)
    lax.fori_loop(0, n_chunks, phase2, (zero, zero))

    def phase3(i, carry):
        rows = rows_of(i)
        o = of_scr[rows, :] + ob_scr[rows, :]
        o += lax.dot_general(jnp.concatenate([qhf_scr[rows, :], qhb_scr[rows, :]], axis=1),
                             jnp.concatenate([stf_scr[i], stb_scr[i]], axis=1), NT_DIMS,
                             preferred_element_type=F32)
        o = _rmsnorm_rows(o, gain_ref[0])
        g = g_ref[0, rows, :]
        o_ref[0, rows, :] = (o * (g * jax.nn.sigmoid(g))).astype(o_ref.dtype)
        return carry

    lax.fori_loop(0, n_chunks, phase3, 0, unroll=True)


def _level_masks(chunk):
    idx = jnp.arange(chunk)
    t, s = idx[:, None], idx[None, :]
    fwd, bwd = [], []
    h = 1
    while h < chunk:
        same = (t // (2 * h)) == (s // (2 * h))
        t_up = (t & h) != 0
        s_up = (s & h) != 0
        fwd.append(same & t_up & ~s_up)
        bwd.append(same & ~t_up & s_up)
        h *= 2
    return jnp.stack(fwd + bwd).astype(F32)


def hgrn2_mixer(p, lb_f, lb_b, out_gain, *, heads, chunk=HGRN_CHUNK):
    b, s, _ = p.shape
    n_chunks = s // chunk
    n_levels = int(math.log2(chunk))

    def lb_rows(lb):
        lb = lb.astype(F32).reshape(heads, 1, LANES)
        rows = jnp.concatenate([lb, jnp.log1p(-lb), 1.0 - lb], axis=1)
        return jnp.pad(rows, ((0, 0), (0, SUBLANES - 3), (0, 0)))

    def col(block0):
        return pl.BlockSpec((1, s, LANES), lambda bi, hi: (bi, 0, block0 + hi))

    per_head = pl.BlockSpec((1, SUBLANES, LANES), lambda bi, hi: (hi, 0, 0))
    masks = _level_masks(chunk)
    idx = jnp.arange(chunk)
    lower = idx[None, :] <= idx[:, None]
    tri = jnp.stack([lower, lower.T]).astype(BF16)
    kern = functools.partial(_hgrn_kernel, chunk=chunk, n_chunks=n_chunks, n_levels=n_levels)
    seq_f32 = pltpu.VMEM((s, LANES), F32)
    seq_bf16 = pltpu.VMEM((s, LANES), BF16)
    chunk_mat_f32 = pltpu.VMEM((n_chunks, LANES, LANES), F32)
    chunk_mat_bf16 = pltpu.VMEM((n_chunks, LANES, LANES), BF16)
    chunk_row = pltpu.VMEM((n_chunks, SUBLANES, LANES), F32)
    return pl.pallas_call(
        kern,
        grid=(b, heads),
        in_specs=[
            col(COL_RQ // LANES), col(COL_RI // LANES), col(COL_ZF // LANES), col(COL_ZB // LANES),
            col(COL_RG // LANES), per_head, per_head,
            pl.BlockSpec((1, 1, LANES), lambda bi, hi: (hi, 0, 0)),
            pl.BlockSpec(masks.shape, lambda bi, hi: (0, 0, 0)),
            pl.BlockSpec(tri.shape, lambda bi, hi: (0, 0, 0)),
        ],
        out_specs=pl.BlockSpec((1, s, LANES), lambda bi, hi: (bi, 0, hi)),
        out_shape=jax.ShapeDtypeStruct((b, s, heads * LANES), BF16),
        scratch_shapes=[
            seq_f32, seq_f32, seq_bf16, seq_bf16,
            chunk_mat_f32, chunk_mat_f32, chunk_row, chunk_row,
            chunk_mat_bf16, chunk_mat_bf16,
            seq_f32, seq_f32, seq_f32, seq_f32, chunk_row, chunk_row,
        ],
        compiler_params=_params("parallel", "parallel"),
        name="hgrn2",
    )(p, p, p, p, p, lb_rows(lb_f), lb_rows(lb_b), out_gain.astype(F32).reshape(heads, 1, LANES), masks, tri)


ATT_SUB = 2


def _attn_kernel(sink_ref, q_ref, k0_ref, k1_ref, k2_ref, k3_ref, v0_ref, v1_ref, v2_ref, v3_ref,
                 bias0_ref, bias1_ref, o_ref):
    qscale = LOG2E / math.sqrt(ATT_HD)
    qb = Q_BLOCK
    k_refs = (k0_ref, k1_ref, k2_ref, k3_ref)
    v_refs = (v0_ref, v1_ref, v2_ref, v3_ref)

    def head_cols(ref, head, rows=slice(None)):
        return ref[0, rows, head * ATT_HD:(head + 1) * ATT_HD]

    def band(refs, first, kvh):
        return jnp.concatenate([head_cols(r, kvh) for r in refs[first:first + 3]], axis=0).astype(BF16)

    for sub, bias_ref in enumerate((bias0_ref, bias1_ref)):
        rows = slice(sub * qb, (sub + 1) * qb)
        for kvh in range(ATT_KV_HEADS):
            k = band(k_refs, sub, kvh)
            v1 = jnp.concatenate([band(v_refs, sub, kvh), jnp.ones((3 * qb, ATT_HD), BF16)], axis=1)
            for g in range(ATT_GROUP):
                head = kvh * ATT_GROUP + g
                q = (head_cols(q_ref, head, rows) * qscale).astype(BF16)
                s = lax.dot_general(q, k, NT_DIMS, preferred_element_type=F32)
                s = s + bias_ref[0, head * qb:(head + 1) * qb]
                sink = sink_ref[head] * LOG2E
                m = jnp.broadcast_to(jnp.maximum(jnp.max(s, axis=-1, keepdims=True), sink), (qb, ATT_HD))
                e = jnp.exp2(s - jnp.concatenate([m, m, m], axis=1)).astype(BF16)
                ov = jnp.dot(e, v1, preferred_element_type=F32)
                o = ov[:, :ATT_HD] / (ov[:, ATT_HD:] + jnp.exp2(sink - m))
                o_ref[0, rows, head * ATT_HD:(head + 1) * ATT_HD] = o.astype(o_ref.dtype)


def _attn_bias():
    qi = jnp.arange(Q_BLOCK)[:, None]
    kj = jnp.arange(3 * Q_BLOCK)[None, :]
    dist = jnp.abs(kj - Q_BLOCK - qi)
    slopes = 2.0 ** (-8.0 * jnp.arange(1, ATT_HEADS + 1, dtype=F32) / ATT_HEADS)
    alibi = -(slopes[:, None, None] * dist.astype(F32)[None]) * LOG2E
    cases = []
    for has_prev, has_next in ((False, True), (True, True), (True, False)):
        valid = (dist <= WINDOW) & ((kj >= Q_BLOCK) | has_prev) & ((kj < 2 * Q_BLOCK) | has_next)
        cases.append(jnp.where(valid[None], alibi, -jnp.inf).reshape(ATT_HEADS * Q_BLOCK, 3 * Q_BLOCK))
    return jnp.stack(cases)


def window_gqa(p, sink, bias):
    b, s, _ = p.shape
    nb = s // Q_BLOCK
    assert nb >= 2 and nb % ATT_SUB == 0
    kcol, vcol = COL_AK // KV_WIDTH, COL_AV // KV_WIDTH

    def kv(col, off):
        return pl.BlockSpec((1, Q_BLOCK, KV_WIDTH),
                            lambda bi, n, sk: (bi, jnp.clip(ATT_SUB * n - 1 + off, 0, nb - 1), col))

    def bias_spec(sub):
        def case(bi, n, sk):
            blk = ATT_SUB * n + sub
            return (jnp.where(blk == 0, 0, jnp.where(blk == nb - 1, 2, 1)), 0, 0)
        return pl.BlockSpec((1, ATT_HEADS * Q_BLOCK, 3 * Q_BLOCK), case)

    rows = ATT_SUB * Q_BLOCK
    n_band = ATT_SUB + 2
    return pl.pallas_call(
        _attn_kernel,
        grid_spec=pltpu.PrefetchScalarGridSpec(
            num_scalar_prefetch=1,
            grid=(b, nb // ATT_SUB),
            in_specs=[pl.BlockSpec((1, rows, ATT_WIDTH), lambda bi, n, sk: (bi, n, COL_AQ // ATT_WIDTH))]
            + [kv(kcol, off) for off in range(n_band)]
            + [kv(vcol, off) for off in range(n_band)]
            + [bias_spec(sub) for sub in range(ATT_SUB)],
            out_specs=pl.BlockSpec((1, rows, ATT_WIDTH), lambda bi, n, sk: (bi, n, 0)),
        ),
        out_shape=jax.ShapeDtypeStruct((b, s, ATT_WIDTH), BF16),
        compiler_params=_params("parallel", "arbitrary"),
        name="window_gqa",
    )(sink.astype(F32), *([p] * (1 + 2 * n_band)), *([bias] * ATT_SUB))


def _token_mixers_kernel(sink_ref, *refs, n_hgrn_in, n_attn_in, hgrn_static):
    hgrn_in = refs[:n_hgrn_in]
    attn_in = refs[n_hgrn_in:n_hgrn_in + n_attn_in]
    oa_ref, ob_ref = refs[n_hgrn_in + n_attn_in:n_hgrn_in + n_attn_in + 2]
    scratch = refs[n_hgrn_in + n_attn_in + 2:]
    _hgrn_kernel(*hgrn_in, oa_ref, *scratch, **hgrn_static)
    _attn_kernel(sink_ref, *attn_in, ob_ref)


def token_mixers(p, lb_f, lb_b, out_gain, sink, bias, *, heads=RG_HEADS, chunk=HGRN_CHUNK):
    b, s, _ = p.shape
    n_chunks = s // chunk
    n_levels = int(math.log2(chunk))
    nb = s // Q_BLOCK
    assert nb >= 2 and nb == ATT_SUB * heads

    def lb_rows(lb):
        lb = lb.astype(F32).reshape(heads, 1, LANES)
        rows = jnp.concatenate([lb, jnp.log1p(-lb), 1.0 - lb], axis=1)
        return jnp.pad(rows, ((0, 0), (0, SUBLANES - 3), (0, 0)))

    def col(block0):
        return pl.BlockSpec((1, s, LANES), lambda bi, hi, sk: (bi, 0, block0 + hi))

    per_head = pl.BlockSpec((1, SUBLANES, LANES), lambda bi, hi, sk: (hi, 0, 0))
    masks = _level_masks(chunk)
    idx = jnp.arange(chunk)
    lower = idx[None, :] <= idx[:, None]
    tri = jnp.stack([lower, lower.T]).astype(BF16)
    hgrn_specs = [
        col(COL_RQ // LANES), col(COL_RI // LANES), col(COL_ZF // LANES), col(COL_ZB // LANES),
        col(COL_RG // LANES), per_head, per_head,
        pl.BlockSpec((1, 1, LANES), lambda bi, hi, sk: (hi, 0, 0)),
        pl.BlockSpec(masks.shape, lambda bi, hi, sk: (0, 0, 0)),
        pl.BlockSpec(tri.shape, lambda bi, hi, sk: (0, 0, 0)),
    ]
    hgrn_args = [p, p, p, p, p, lb_rows(lb_f), lb_rows(lb_b), out_gain.astype(F32).reshape(heads, 1, LANES), masks, tri]

    kcol, vcol = COL_AK // KV_WIDTH, COL_AV // KV_WIDTH

    def kv(colb, off):
        return pl.BlockSpec((1, Q_BLOCK, KV_WIDTH),
                            lambda bi, n, sk: (bi, jnp.clip(ATT_SUB * n - 1 + off, 0, nb - 1), colb))

    def bias_spec(sub):
        def case(bi, n, sk):
            blk = ATT_SUB * n + sub
            return (jnp.where(blk == 0, 0, jnp.where(blk == nb - 1, 2, 1)), 0, 0)
        return pl.BlockSpec((1, ATT_HEADS * Q_BLOCK, 3 * Q_BLOCK), case)

    rows = ATT_SUB * Q_BLOCK
    n_band = ATT_SUB + 2
    attn_specs = ([pl.BlockSpec((1, rows, ATT_WIDTH), lambda bi, n, sk: (bi, n, COL_AQ // ATT_WIDTH))]
                  + [kv(kcol, off) for off in range(n_band)] + [kv(vcol, off) for off in range(n_band)]
                  + [bias_spec(sub) for sub in range(ATT_SUB)])
    attn_args = [p] * (1 + 2 * n_band) + [bias] * ATT_SUB

    seq_f32 = pltpu.VMEM((s, LANES), F32)
    seq_bf16 = pltpu.VMEM((s, LANES), BF16)
    chunk_mat_f32 = pltpu.VMEM((n_chunks, LANES, LANES), F32)
    chunk_mat_bf16 = pltpu.VMEM((n_chunks, LANES, LANES), BF16)
    chunk_row = pltpu.VMEM((n_chunks, SUBLANES, LANES), F32)
    kern = functools.partial(_token_mixers_kernel, n_hgrn_in=len(hgrn_specs), n_attn_in=len(attn_specs),
                             hgrn_static=dict(chunk=chunk, n_chunks=n_chunks, n_levels=n_levels))
    return pl.pallas_call(
        kern,
        grid_spec=pltpu.PrefetchScalarGridSpec(
            num_scalar_prefetch=1,
            grid=(b, heads),
            in_specs=hgrn_specs + attn_specs,
            out_specs=[pl.BlockSpec((1, s, LANES), lambda bi, hi, sk: (bi, 0, hi)),
                       pl.BlockSpec((1, rows, ATT_WIDTH), lambda bi, n, sk: (bi, n, 0))],
            scratch_shapes=[
                seq_f32, seq_f32, seq_bf16, seq_bf16,
                chunk_mat_f32, chunk_mat_f32, chunk_row, chunk_row,
                chunk_mat_bf16, chunk_mat_bf16,
                seq_f32, seq_f32, seq_f32, seq_f32, chunk_row, chunk_row,
            ],
        ),
        out_shape=[jax.ShapeDtypeStruct((b, s, heads * LANES), BF16), jax.ShapeDtypeStruct((b, s, ATT_WIDTH), BF16)],
        compiler_params=_params("parallel", "arbitrary"),
        name="token_mixers",
    )(sink.astype(F32), *hgrn_args, *attn_args)


def _merge_kernel(oa_ref, ob_ref, wa_ref, wb_ref, ga_ref, gb_ref, m_ref, wa_scr, wb_scr):
    @pl.when(pl.program_id(0) == 0)
    def _():
        wa_scr[...] = wa_ref[...].astype(BF16)
        wb_scr[...] = wb_ref[...].astype(BF16)

    oa = oa_ref[...]
    ob = ob_ref[...]
    for c0 in range(0, m_ref.shape[1], MERGE_COLS):
        cols = slice(c0, c0 + MERGE_COLS)
        ya = jnp.dot(oa, wa_scr[:, cols], preferred_element_type=F32)
        yb = jnp.dot(ob, wb_scr[:, cols], preferred_element_type=F32)
        ga = jax.nn.sigmoid(ga_ref[:, cols].astype(F32))
        gb = jax.nn.sigmoid(gb_ref[:, cols].astype(F32))
        m_ref[:, cols] = (ga * ya + gb * yb).astype(m_ref.dtype)


def merge_branches(oa, ob, wa, wb, gates, l, *, tm=TM_MERGE):
    t, ka = oa.shape
    kb = ob.shape[1]
    n = wa.shape[2]
    return pl.pallas_call(
        _merge_kernel,
        grid=(t // tm,),
        in_specs=[
            pl.BlockSpec((tm, ka), lambda i: (i, 0)),
            pl.BlockSpec((tm, kb), lambda i: (i, 0)),
            _once((None, ka, n), lambda i: (l, 0, 0)),
            _once((None, kb, n), lambda i: (l, 0, 0)),
            pl.BlockSpec((tm, n), lambda i: (i, 0)),
            pl.BlockSpec((tm, n), lambda i: (i, 1)),
        ],
        out_specs=pl.BlockSpec((tm, n), lambda i: (i, 0)),
        out_shape=jax.ShapeDtypeStruct((t, n), BF16),
        scratch_shapes=[pltpu.VMEM((ka, n), BF16), pltpu.VMEM((kb, n), BF16)],
        compiler_params=_params("arbitrary"),
        name="merge_branches",
    )(oa, ob, wa, wb, gates, gates)


def _final_norm_kernel(x_ref, gain_ref, o_ref):
    o_ref[...] = _rmsnorm_rows(x_ref[...], gain_ref[...])


def final_rmsnorm(x, gain, *, tm=TM_NORM):
    t, d = x.shape
    return pl.pallas_call(
        _final_norm_kernel,
        grid=(t // tm,),
        in_specs=[pl.BlockSpec((tm, d), lambda i: (i, 0)), pl.BlockSpec((1, d), lambda i: (0, 0))],
        out_specs=pl.BlockSpec((tm, d), lambda i: (i, 0)),
        out_shape=jax.ShapeDtypeStruct((t, d), F32),
        compiler_params=_params("parallel"),
        name="final_norm",
    )(x, gain.reshape(1, d))


def _lower_bounds(lb_logits):
    lb = jnp.cumsum(jax.nn.softmax(lb_logits.astype(F32), axis=0), axis=0)
    return lb - lb[0:1]


def kernel(x, ffn1_norm, ffn1_w_gate, ffn1_w_up, ffn1_w_down, mix_norm, w_in, lb_fwd_logits, lb_bwd_logits, rg_out_norm, attn_sink, w_branch_a, w_branch_b, w_out, ffn2_norm, ffn2_w_gate, ffn2_w_up, ffn2_w_down, final_norm):
    b, s, d = x.shape
    t = b * s
    depth = w_in.shape[0]
    lb_f_all = _lower_bounds(lb_fwd_logits)
    lb_b_all = _lower_bounds(lb_bwd_logits)
    xf = x.reshape(t, d)
    x_norm_in = xf
    attn_bias = _attn_bias()

    def ffn_down(a, xf, wd, l, second):
        return resid_mm(a, wd, xf, l, scale=0.5, tm=TM_DOWN, tn=TN_DOWN, second=second)

    for l in range(depth):
        a = ffn_up(x_norm_in, ffn1_norm, ffn1_w_gate, ffn1_w_up, l)
        xf, x_norm_in = ffn_down(a, xf, ffn1_w_down, l, "copy")
        p, gates = in_proj(x_norm_in, mix_norm, w_in, l)
        p3 = p.reshape(b, s, COL_GA)
        o_a, o_b = token_mixers(p3, lb_f_all[l], lb_b_all[l], rg_out_norm[l], attn_sink[l], attn_bias)
        merged = merge_branches(o_a.reshape(t, RG_WIDTH), o_b.reshape(t, ATT_WIDTH),
                                w_branch_a, w_branch_b, gates, l)
        xf, h = resid_mm(merged, w_out, xf, l, scale=1.0, tm=TM_OUT, tn=TN_OUT, second="norm",
                         norm_gain=ffn2_norm)
        a = ffn_up(h, None, ffn2_w_gate, ffn2_w_up, l)
        if l + 1 < depth:
            xf, x_norm_in = ffn_down(a, xf, ffn2_w_down, l, "copy")
        else:
            xf = ffn_down(a, xf, ffn2_w_down, l, None)
    return final_rmsnorm(xf, final_norm).reshape(b, s, d)
```
